```python
import math
import jax, jax.numpy as jnp
from jax import lax
import numpy as np

D_MODEL = 1024
BATCH = 8
SEQ = 2048
DEPTH = 2
DEC_BATCH = 8
DEC_SEQ = 32
PAST_LEN = 1024

CHUNK = 64
Q_BLOCK = 128
HEAD_DIM = 64
H_A = 4
H_B = 8
H_C = 4
W_A = H_A * HEAD_DIM
MLA_NOPE = 64
MLA_ROPE = 32
MLA_V = 64
W_B = H_B * MLA_V
W_C = H_C * HEAD_DIM
MIX_WIDTH = W_A + W_B + W_C
Q_LORA = 256
KV_LORA = 128
ROPE_THETA = 10000.0
D_FF = 2816
PLE_DIM = 256
EPS = 1e-6
FFN_RES = 0.5
SB_SCALE = HEAD_DIM ** -0.5
MLA_SCALE = (MLA_NOPE + MLA_ROPE) ** -0.5
FOX_SCALE = HEAD_DIM ** -0.5
A_COLS = 3 * W_A
B_COLS = Q_LORA + KV_LORA + MLA_ROPE
C_COLS = 3 * W_C + H_C
IN_COLS = A_COLS + B_COLS + C_COLS
SPLITS = (W_A, 2 * W_A, 3 * W_A, 3 * W_A + Q_LORA, 3 * W_A + Q_LORA + KV_LORA, A_COLS + B_COLS,
          A_COLS + B_COLS + W_C, A_COLS + B_COLS + 2 * W_C, A_COLS + B_COLS + 3 * W_C)
N_STATE = 7

kernel_name = "hybrid_stream_sb_mla_fox_step"


def rmsnorm(x, g):
    xf = x.astype(jnp.float32)
    y = xf * lax.rsqrt(jnp.mean(xf * xf, axis=-1, keepdims=True) + EPS)
    return (y * g.astype(jnp.float32)).astype(x.dtype)


def swiglu(x, w_gu, w_down):
    g, u = jnp.split(x @ w_gu, 2, axis=-1)
    return (jax.nn.silu(g) * u) @ w_down


def rope(x, pos):
    half = MLA_ROPE // 2
    inv = ROPE_THETA ** (-jnp.arange(half, dtype=jnp.float32) / half)
    ang = pos.astype(jnp.float32)[:, None] * inv[None, :]
    shape = (pos.shape[0],) + (1,) * (x.ndim - 3) + (half,)
    cos, sin = jnp.cos(ang).reshape(shape), jnp.sin(ang).reshape(shape)
    xf = x.astype(jnp.float32)
    x1, x2 = xf[..., :half], xf[..., half:]
    return jnp.concatenate([x1 * cos - x2 * sin, x1 * sin + x2 * cos], axis=-1).astype(x.dtype)


def blockify(a):
    b, s = a.shape[0], a.shape[1]
    return jnp.moveaxis(a.reshape((b, s // Q_BLOCK, Q_BLOCK) + a.shape[2:]), 1, 0)


def unblockify(a):
    a = jnp.moveaxis(a, 0, 1)
    return a.reshape((a.shape[0], a.shape[1] * a.shape[2]) + a.shape[3:])


def stick_breaking(q, k, v, q_pos, k_pos):
    z = jnp.einsum('bqhd,bkhd->bhqk', q, k).astype(jnp.float32) * SB_SCALE
    mask = k_pos[None, :] < q_pos[:, None]
    log_1m = jnp.where(mask, jax.nn.log_sigmoid(-z), 0.0)
    between = lax.cumsum(log_1m, axis=3, reverse=True) - log_1m
    w = jnp.where(mask, jnp.exp(jax.nn.log_sigmoid(z) + between), 0.0)
    return jnp.einsum('bhqk,bkhd->bqhd', w.astype(v.dtype), v)


def chunk_softmax(q, k, v, q_pos, k_pos):
    z = jnp.einsum('bqhd,bkhd->bhqk', q, k).astype(jnp.float32) * MLA_SCALE
    mask = (k_pos // CHUNK)[None, :] <= (q_pos // CHUNK)[:, None]
    pr = jax.nn.softmax(jnp.where(mask, z, -jnp.inf), axis=-1)
    return jnp.einsum('bhqk,bkhd->bqhd', pr.astype(v.dtype), v)


def forgetting(q, k, v, q_pos, k_pos, fq, fk):
    z = jnp.einsum('bqhd,bkhd->bhqk', q, k).astype(jnp.float32) * FOX_SCALE
    z = z + (jnp.swapaxes(fq, 1, 2)[..., :, None] - jnp.swapaxes(fk, 1, 2)[..., None, :])
    mask = k_pos[None, :] <= q_pos[:, None]
    pr = jax.nn.softmax(jnp.where(mask, z, -jnp.inf), axis=-1)
    return jnp.einsum('bhqk,bkhd->bqhd', pr.astype(v.dtype), v)


def token_mix(xn, q_pos, past, w_in, b_f, g_bq, g_bkv, w_uq, w_ukv, g_grp, w_out):
    bsz, t = xn.shape[0], xn.shape[1]
    qa, ka, va, cq, ckv, kr, qc, kc, vc, fl = jnp.split(xn @ w_in, SPLITS, axis=-1)
    qa, ka, va = (a.reshape(bsz, t, H_A, HEAD_DIM) for a in (qa, ka, va))
    qc, kc, vc = (a.reshape(bsz, t, H_C, HEAD_DIM) for a in (qc, kc, vc))
    q_b = (rmsnorm(cq, g_bq) @ w_uq).reshape(bsz, t, H_B, MLA_NOPE + MLA_ROPE)
    q_b = jnp.concatenate([q_b[..., :MLA_NOPE], rope(q_b[..., MLA_NOPE:], q_pos)], axis=-1)
    ckv = rmsnorm(ckv, g_bkv)
    kr = rope(kr, q_pos)
    logf = jax.nn.log_sigmoid((fl + b_f).astype(jnp.float32)).astype(xn.dtype)
    new_state = (ka, va, ckv, kr, kc, vc, logf)
    if past is None:
        ka_all, va_all, ckv_all, kr_all, kc_all, vc_all, logf_all = new_state
        k_pos = q_pos
    else:
        ka_all, va_all, ckv_all, kr_all, kc_all, vc_all, logf_all = [
            jnp.concatenate([c, n.astype(c.dtype)], axis=1) for c, n in zip(past, new_state)]
        k_pos = jnp.arange(past[0].shape[1] + t, dtype=jnp.int32)
    tk = ka_all.shape[1]
    kv_b = (ckv_all @ w_ukv).reshape(bsz, tk, H_B, MLA_NOPE + MLA_V)
    k_b = jnp.concatenate([kv_b[..., :MLA_NOPE],
                           jnp.broadcast_to(kr_all[:, :, None, :], (bsz, tk, H_B, MLA_ROPE)).astype(kv_b.dtype)], axis=-1)
    v_b = kv_b[..., MLA_NOPE:]
    f_all = jnp.cumsum(logf_all.astype(jnp.float32), axis=1)
    f_q = f_all[:, tk - t:]

    def attend(qa_blk, qb_blk, qc_blk, fq_blk, pos_blk):
        return (stick_breaking(qa_blk, ka_all, va_all, pos_blk, k_pos),
                chunk_softmax(qb_blk, k_b, v_b, pos_blk, k_pos),
                forgetting(qc_blk, kc_all, vc_all, pos_blk, k_pos, fq_blk, f_all))

    if past is None:
        oa, ob, oc = lax.map(lambda a: attend(*a),
                             (blockify(qa), blockify(q_b), blockify(qc), blockify(f_q), q_pos.reshape(-1, Q_BLOCK)))
        oa, ob, oc = unblockify(oa), unblockify(ob), unblockify(oc)
    else:
        oa, ob, oc = attend(qa, q_b, qc, f_q, q_pos)
    g_a, g_b, g_c = jnp.split(g_grp, [W_A, W_A + W_B])
    o = jnp.concatenate([rmsnorm(oa.reshape(bsz, t, W_A), g_a),
                         rmsnorm(ob.reshape(bsz, t, W_B), g_b),
                         rmsnorm(oc.reshape(bsz, t, W_C), g_c)], axis=-1)
    return o @ w_out, new_state


def layer(h, p, q_pos, past, g_ff1_pre, g_ff1_post, w_ff1_gu, w_ff1_down, g_mix_pre, g_mix_post, w_in, b_f,
          g_bq, g_bkv, w_uq, w_ukv, g_grp, w_out, g_ff2_pre, g_ff2_post, w_ff2_gu, w_ff2_down,
          g_ple_pre, w_ple_gate, w_ple_proj, g_ple_post):
    h = h + FFN_RES * rmsnorm(swiglu(rmsnorm(h, g_ff1_pre), w_ff1_gu, w_ff1_down), g_ff1_post)
    m, state = token_mix(rmsnorm(h, g_mix_pre), q_pos, past, w_in, b_f, g_bq, g_bkv, w_uq, w_ukv, g_grp, w_out)
    h = h + rmsnorm(m, g_mix_post)
    h = h + FFN_RES * rmsnorm(swiglu(rmsnorm(h, g_ff2_pre), w_ff2_gu, w_ff2_down), g_ff2_post)
    gate = jax.nn.sigmoid(rmsnorm(h, g_ple_pre) @ w_ple_gate)
    h = h + rmsnorm((p @ w_ple_proj) * gate, g_ple_post)
    return h, state


def run_trunk(x, p, q_pos, caches, weights):
    h = x
    per_layer = []
    for i in range(DEPTH):
        lw = [w[i] for w in weights]
        past = None if caches is None else [c[i] for c in caches]
        h, st = layer(h, p[i], q_pos, past, *lw)
        per_layer.append(st)
    stacked = [jnp.stack([st[j] for st in per_layer]) for j in range(N_STATE)]
    return h, stacked


def setup_inputs(seed: int = 0) -> dict:
    key = jax.random.key(seed)
    ks = iter(jax.random.split(key, 48))

    def nrm(shape, scale=1.0):
        return jax.random.normal(next(ks), shape, jnp.float32) * scale

    def gain(width):
        return 1.0 + nrm((DEPTH, width), 0.05)

    d = D_MODEL
    return {
        "x_prompt": nrm((BATCH, SEQ, d)),
        "x_sample": nrm((DEC_BATCH, DEC_SEQ, d)),
        "p_prompt": nrm((DEPTH, BATCH, SEQ, PLE_DIM)),
        "p_sample": nrm((DEPTH, DEC_BATCH, DEC_SEQ, PLE_DIM)),
        "cache_a_k": nrm((DEPTH, DEC_BATCH, PAST_LEN, H_A, HEAD_DIM)),
        "cache_a_v": nrm((DEPTH, DEC_BATCH, PAST_LEN, H_A, HEAD_DIM)),
        "cache_b_ckv": nrm((DEPTH, DEC_BATCH, PAST_LEN, KV_LORA)),
        "cache_b_krope": nrm((DEPTH, DEC_BATCH, PAST_LEN, MLA_ROPE)),
        "cache_c_k": nrm((DEPTH, DEC_BATCH, PAST_LEN, H_C, HEAD_DIM)),
        "cache_c_v": nrm((DEPTH, DEC_BATCH, PAST_LEN, H_C, HEAD_DIM)),
        "cache_c_logf": jax.nn.log_sigmoid(2.0 + nrm((DEPTH, DEC_BATCH, PAST_LEN, H_C))),
        "g_ff1_pre": gain(d),
        "g_ff1_post": gain(d),
        "w_ff1_gu": nrm((DEPTH, d, 2 * D_FF), d ** -0.5),
        "w_ff1_down": nrm((DEPTH, D_FF, d), D_FF ** -0.5),
        "g_mix_pre": gain(d),
        "g_mix_post": gain(d),
        "w_in": nrm((DEPTH, d, IN_COLS), d ** -0.5),
        "b_f": 2.0 + nrm((DEPTH, H_C), 0.1),
        "g_bq": gain(Q_LORA),
        "g_bkv": gain(KV_LORA),
        "w_uq": nrm((DEPTH, Q_LORA, H_B * (MLA_NOPE + MLA_ROPE)), Q_LORA ** -0.5),
        "w_ukv": nrm((DEPTH, KV_LORA, H_B * (MLA_NOPE + MLA_V)), KV_LORA ** -0.5),
        "g_grp": gain(MIX_WIDTH),
        "w_out": nrm((DEPTH, MIX_WIDTH, d), MIX_WIDTH ** -0.5),
        "g_ff2_pre": gain(d),
        "g_ff2_post": gain(d),
        "w_ff2_gu": nrm((DEPTH, d, 2 * D_FF), d ** -0.5),
        "w_ff2_down": nrm((DEPTH, D_FF, d), D_FF ** -0.5),
        "g_ple_pre": gain(d),
        "w_ple_gate": nrm((DEPTH, d, d), d ** -0.5),
        "w_ple_proj": nrm((DEPTH, PLE_DIM, d), PLE_DIM ** -0.5),
        "g_ple_post": gain(d),
    }


def reference(x_prompt, x_sample, p_prompt, p_sample, cache_a_k, cache_a_v, cache_b_ckv, cache_b_krope,
              cache_c_k, cache_c_v, cache_c_logf, g_ff1_pre, g_ff1_post, w_ff1_gu, w_ff1_down,
              g_mix_pre, g_mix_post, w_in, b_f, g_bq, g_bkv, w_uq, w_ukv, g_grp, w_out,
              g_ff2_pre, g_ff2_post, w_ff2_gu, w_ff2_down, g_ple_pre, w_ple_gate, w_ple_proj, g_ple_post):
    weights = (g_ff1_pre, g_ff1_post, w_ff1_gu, w_ff1_down, g_mix_pre, g_mix_post, w_in, b_f,
               g_bq, g_bkv, w_uq, w_ukv, g_grp, w_out, g_ff2_pre, g_ff2_post, w_ff2_gu, w_ff2_down,
               g_ple_pre, w_ple_gate, w_ple_proj, g_ple_post)
    pos_p = jnp.arange(x_prompt.shape[1], dtype=jnp.int32)
    y_prompt, sp = run_trunk(x_prompt, p_prompt, pos_p, None, weights)
    past_len = cache_a_k.shape[2]
    pos_s = past_len + jnp.arange(x_sample.shape[1], dtype=jnp.int32)
    caches = (cache_a_k, cache_a_v, cache_b_ckv, cache_b_krope, cache_c_k, cache_c_v, cache_c_logf)
    y_sample, ss = run_trunk(x_sample, p_sample, pos_s, caches, weights)
    a_k_p, a_v_p, b_ckv_p, b_krope_p, c_k_p, c_v_p, c_logf_p = sp
    a_k_s, a_v_s, b_ckv_s, b_krope_s, c_k_s, c_v_s, c_logf_s = ss
    return (y_prompt, y_sample, a_k_p, a_v_p, b_ckv_p, b_krope_p, c_k_p, c_v_p, c_logf_p,
            a_k_s, a_v_s, b_ckv_s, b_krope_s, c_k_s, c_v_s, c_logf_s)
```

```python
import functools
import math

import jax
import jax.numpy as jnp
from jax import lax
from jax.experimental import pallas as pl
from jax.experimental.pallas import tpu as pltpu

CHUNK = 64
HEAD_DIM = 64
H_A = 4
H_B = 8
H_C = 4
W_A = H_A * HEAD_DIM
MLA_NOPE = 64
MLA_ROPE = 32
MLA_V = 64
W_B = H_B * MLA_V
W_C = H_C * HEAD_DIM
Q_LORA = 256
KV_LORA = 128
ROPE_THETA = 10000.0
EPS = 1e-6
FFN_RES = 0.5
SB_SCALE = HEAD_DIM ** -0.5
MLA_SCALE = (MLA_NOPE + MLA_ROPE) ** -0.5
FOX_SCALE = HEAD_DIM ** -0.5

LANES = 128
KEY_BLOCK = 256
F_ROWS = 16
VMEM_LIMIT = 56 * 1024 * 1024

COL_A = 0
COL_CQ = COL_A + 3 * W_A
COL_CKV = COL_CQ + Q_LORA
COL_KRA = COL_CKV + KV_LORA
COL_KRB = COL_KRA + LANES
COL_C = COL_KRB + LANES
COL_F = COL_C + 3 * W_C
IN_COLS_P = COL_F + LANES

BF16 = jnp.bfloat16
F32 = jnp.float32


def _dot(a, b):
    return jnp.dot(a, b, preferred_element_type=F32)


def _dot_nt(a, b):
    return lax.dot_general(a, b, (((1,), (1,)), ((), ())), preferred_element_type=F32)


def _rms(x, g):
    ms = jnp.mean(x * x, axis=-1, keepdims=True)
    return x * lax.rsqrt(ms + EPS) * g


def _log_sigmoid(x):
    return jnp.minimum(x, 0.0) - jnp.log1p(jnp.exp(-jnp.abs(x)))


def _split2(x):
    hi = x.astype(BF16)
    lo = (x - hi.astype(F32)).astype(BF16)
    return hi, lo


def _const_spec(shape):
    nd = len(shape)
    return pl.BlockSpec(shape, lambda *_: (0,) * nd, pipeline_mode=pl.Buffered(1))


def _params(n_axes):
    return pltpu.CompilerParams(dimension_semantics=("arbitrary",) * n_axes, vmem_limit_bytes=VMEM_LIMIT)


def _row_tile(n):
    for tm in (512, 256):
        if n % tm == 0:
            return tm
    return n


def _ffn_kernel(h_ref, gpre_ref, gpost_ref, wgu_ref, wdown_ref, o_ref, *, d_ff, chunks):
    h = h_ref[...]
    xn = _rms(h, gpre_ref[...]).astype(BF16)
    acc = jnp.zeros(h.shape, F32)
    for c0, c1 in chunks:
        g = _dot(xn, wgu_ref[:, c0:c1])
        u = _dot(xn, wgu_ref[:, d_ff + c0:d_ff + c1])
        a = (g * jax.nn.sigmoid(g) * u).astype(BF16)
        acc = acc + _dot(a, wdown_ref[c0:c1, :])
    o_ref[...] = h + FFN_RES * _rms(acc, gpost_ref[...])


def _ffn(h, gpre, gpost, wgu, wdown):
    n, d = h.shape
    d_ff = wdown.shape[0]
    tm = _row_tile(n)
    step = 4 * KEY_BLOCK
    chunks = tuple((c, min(c + step, d_ff)) for c in range(0, d_ff, step))
    row = pl.BlockSpec((tm, d), lambda i: (i, 0))
    return pl.pallas_call(
        functools.partial(_ffn_kernel, d_ff=d_ff, chunks=chunks),
        grid=(n // tm,),
        in_specs=[row, _const_spec(gpre.shape), _const_spec(gpost.shape), _const_spec(wgu.shape), _const_spec(wdown.shape)],
        out_specs=row,
        out_shape=jax.ShapeDtypeStruct((n, d), F32),
        compiler_params=_params(1),
        name="ffn",
    )(h, gpre, gpost, wgu, wdown)


def _wcomb_kernel(wq_ref, wk_ref, o_ref):
    o_ref[...] = _dot_nt(wq_ref[...], wk_ref[...]).astype(BF16)


def _wcomb(wq_n, wk_n):
    nh, ql, dn = wq_n.shape
    kl = wk_n.shape[1]
    return pl.pallas_call(
        _wcomb_kernel,
        grid=(nh,),
        in_specs=[pl.BlockSpec((None, ql, dn), lambda h: (h, 0, 0)), pl.BlockSpec((None, kl, dn), lambda h: (h, 0, 0))],
        out_specs=pl.BlockSpec((ql, kl), lambda h: (0, h)),
        out_shape=jax.ShapeDtypeStruct((ql, nh * kl), BF16),
        compiler_params=_params(1),
        name="wcomb",
    )(wq_n, wk_n)


def _inproj_kernel(h_ref, gpre_ref, win_ref, bf_ref, gbq_ref, gbkv_ref, wcomb_ref, wqr_ref, cos_ref, sin_ref,
                   ka_st, va_st, ckv_st, kr_st, kc_st, vc_st, lf_st,
                   qa_b, ka_b, va_b, qm_b, kl_b, qc_b, kc_b, vc_b):
    xn = _rms(h_ref[...], gpre_ref[...]).astype(BF16)
    proj = _dot(xn, win_ref[...])
    cos = cos_ref[...]
    sin = sin_ref[...]

    qa_b[...] = (proj[:, COL_A:COL_A + W_A] * SB_SCALE).astype(BF16)
    ka = proj[:, COL_A + W_A:COL_A + 2 * W_A]
    va = proj[:, COL_A + 2 * W_A:COL_A + 3 * W_A]
    ka_st[...] = ka
    va_st[...] = va
    ka_b[...] = ka.astype(BF16)
    va_b[...] = va.astype(BF16)

    qc_b[...] = (proj[:, COL_C:COL_C + W_C] * FOX_SCALE).astype(BF16)
    kc = proj[:, COL_C + W_C:COL_C + 2 * W_C]
    vc = proj[:, COL_C + 2 * W_C:COL_C + 3 * W_C]
    kc_st[...] = kc
    vc_st[...] = vc
    kc_b[...] = kc.astype(BF16)
    vc_b[...] = vc.astype(BF16)

    lf = _log_sigmoid(proj[:, COL_F:COL_F + LANES] + bf_ref[...])
    lf_st[...] = lf[:, :H_C]

    ckv = _rms(proj[:, COL_CKV:COL_CKV + KV_LORA], gbkv_ref[...])
    ckv_st[...] = ckv
    kr = proj[:, COL_KRA:COL_KRA + LANES] * cos + proj[:, COL_KRB:COL_KRB + LANES] * sin
    kr_st[...] = kr[:, :MLA_ROPE]
    kl_b[:, :KV_LORA] = ckv.astype(BF16)
    kl_b[:, KV_LORA:] = kr.astype(BF16)

    cqn = _rms(proj[:, COL_CQ:COL_CQ + Q_LORA], gbq_ref[...]).astype(BF16)
    qlat = _dot(cqn, wcomb_ref[...])
    qr = _dot(cqn, wqr_ref[...])
    half = H_B * LANES
    for hh in range(H_B):
        sl = slice(hh * LANES, (hh + 1) * LANES)
        rope = qr[:, sl] * cos + qr[:, half + hh * LANES:half + (hh + 1) * LANES] * sin
        qm_b[hh, :, :KV_LORA] = (qlat[:, sl] * MLA_SCALE).astype(BF16)
        qm_b[hh, :, KV_LORA:] = (rope * MLA_SCALE).astype(BF16)


def _inproj(h, gpre, win, bf, gbq, gbkv, wcomb, wqr, cos_t, sin_t):
    n, d = h.shape
    tm = _row_tile(n)
    row = lambda w: pl.BlockSpec((tm, w), lambda i: (i, 0))
    tab = row(LANES)
    outs = [
        (row(W_A), (n, W_A), F32), (row(W_A), (n, W_A), F32), (row(KV_LORA), (n, KV_LORA), F32),
        (row(MLA_ROPE), (n, MLA_ROPE), F32), (row(W_C), (n, W_C), F32), (row(W_C), (n, W_C), F32),
        (row(H_C), (n, H_C), F32),
        (row(W_A), (n, W_A), BF16), (row(W_A), (n, W_A), BF16), (row(W_A), (n, W_A), BF16),
        (pl.BlockSpec((H_B, tm, 2 * LANES), lambda i: (0, i, 0)), (H_B, n, 2 * LANES), BF16),
        (row(2 * LANES), (n, 2 * LANES), BF16),
        (row(W_C), (n, W_C), BF16), (row(W_C), (n, W_C), BF16), (row(W_C), (n, W_C), BF16),
    ]
    return pl.pallas_call(
        _inproj_kernel,
        grid=(n // tm,),
        in_specs=[row(d), _const_spec(gpre.shape), _const_spec(win.shape), _const_spec(bf.shape),
                  _const_spec(gbq.shape), _const_spec(gbkv.shape), _const_spec(wcomb.shape), _const_spec(wqr.shape),
                  tab, tab],
        out_specs=[o[0] for o in outs],
        out_shape=[jax.ShapeDtypeStruct(o[1], o[2]) for o in outs],
        compiler_params=_params(1),
        name="inproj",
    )(h, gpre, win, bf, gbq, gbkv, wcomb, wqr, cos_t, sin_t)


def _cumsum_kernel(x_ref, o_ref, *, n_blocks):
    r = lax.broadcasted_iota(jnp.int32, (KEY_BLOCK, KEY_BLOCK), 0)
    c = lax.broadcasted_iota(jnp.int32, (KEY_BLOCK, KEY_BLOCK), 1)
    upper = jnp.where(r <= c, 1.0, 0.0).astype(BF16)
    carry = jnp.zeros((x_ref.shape[0], 1), F32)
    for j in range(n_blocks):
        sl = slice(j * KEY_BLOCK, (j + 1) * KEY_BLOCK)
        x = x_ref[:, sl]
        hi = x.astype(BF16)
        mid, lo = _split2(x - hi.astype(F32))
        f = _dot(hi, upper) + _dot(mid, upper) + _dot(lo, upper) + carry
        o_ref[:, sl] = f
        carry = f[:, KEY_BLOCK - 1:KEY_BLOCK]


def _cumsum_rows(x):
    b, r, tp = x.shape
    spec = pl.BlockSpec((None, r, tp), lambda i: (i, 0, 0))
    return pl.pallas_call(
        functools.partial(_cumsum_kernel, n_blocks=tp // KEY_BLOCK),
        grid=(b,),
        in_specs=[spec],
        out_specs=spec,
        out_shape=jax.ShapeDtypeStruct(x.shape, F32),
        compiler_params=_params(1),
        name="cumsum_logf",
    )(x)


def _block_range(q_start, tq, n_valid):
    n_full = q_start // KEY_BLOCK
    last = jnp.minimum(((q_start + tq + CHUNK - 1) // CHUNK) * CHUNK, n_valid)
    n_total = (last + KEY_BLOCK - 1) // KEY_BLOCK
    return n_full, n_total


def _positions(q_start, tq, j, rows):
    q_pos = q_start + lax.broadcasted_iota(jnp.int32, (rows, KEY_BLOCK), 0) % tq
    k_pos = j * KEY_BLOCK + lax.broadcasted_iota(jnp.int32, (rows, KEY_BLOCK), 1)
    return q_pos, k_pos


def _finite_or_zero(m):
    return jnp.where(m == -jnp.inf, 0.0, m)


def _head_masks(n_heads):
    lane = lax.broadcasted_iota(jnp.int32, (1, n_heads * HEAD_DIM), 1)
    return [(lane >= hh * HEAD_DIM) & (lane < (hh + 1) * HEAD_DIM) for hh in range(n_heads)]


def _sb_kernel(q_ref, k_ref, v_ref, o_ref, qm_ref, acc_ref, c_ref, *, tq, q_pos0, n_valid):
    q_start = q_pos0 + pl.program_id(1) * tq
    n_full, n_total = _block_range(q_start, tq, n_valid)
    hmask = _head_masks(H_A)
    q = q_ref[...]
    for hh in range(H_A):
        qm_ref[hh] = jnp.where(hmask[hh], q, jnp.zeros_like(q))
    acc_ref[...] = jnp.zeros(acc_ref.shape, F32)
    c_ref[...] = jnp.zeros(c_ref.shape, F32)
    r = lax.broadcasted_iota(jnp.int32, (KEY_BLOCK, KEY_BLOCK), 0)
    c = lax.broadcasted_iota(jnp.int32, (KEY_BLOCK, KEY_BLOCK), 1)
    lower = jnp.where(r >= c, 1.0, 0.0).astype(BF16)

    def block(j, masked):
        off = pl.multiple_of(j * KEY_BLOCK, KEY_BLOCK)
        kb = k_ref[pl.ds(off, KEY_BLOCK), :]
        vb = v_ref[pl.ds(off, KEY_BLOCK), :]
        if masked:
            q_pos, k_pos = _positions(q_start, tq, j, tq)
            mask = k_pos < q_pos
        for hh in range(H_A):
            z = _dot_nt(qm_ref[hh], kb)
            ls = _log_sigmoid(z)
            l1m = ls - z
            if masked:
                l1m = jnp.where(mask, l1m, 0.0)
            hi, lo = _split2(l1m)
            cum = _dot(hi, lower) + _dot(lo, lower)
            w = jnp.exp(ls + (cum - l1m) + c_ref[hh])
            if masked:
                w = jnp.where(mask, w, 0.0)
            vh = jnp.where(hmask[hh], vb, jnp.zeros_like(vb))
            acc_ref[...] += _dot(w.astype(BF16), vh)
            c_ref[hh] += cum[:, 0:1]

    def masked_body(i, carry):
        block(n_total - 1 - i, True)
        return carry

    def full_body(i, carry):
        block(n_full - 1 - i, False)
        return carry

    lax.fori_loop(0, n_total - n_full, masked_body, 0)
    lax.fori_loop(0, n_full, full_body, 0)
    o_ref[...] = acc_ref[...]


def _sb_attention(q, k, v, q_pos0, n_valid):
    b, tq_all, w = q.shape
    tk = k.shape[1]
    tq = min(tq_all, KEY_BLOCK)
    qspec = pl.BlockSpec((None, tq, w), lambda bi, qi: (bi, qi, 0))
    kspec = pl.BlockSpec((None, tk, w), lambda bi, qi: (bi, 0, 0))
    return pl.pallas_call(
        functools.partial(_sb_kernel, tq=tq, q_pos0=q_pos0, n_valid=n_valid),
        grid=(b, tq_all // tq),
        in_specs=[qspec, kspec, kspec],
        out_specs=qspec,
        out_shape=jax.ShapeDtypeStruct((b, tq_all, w), F32),
        scratch_shapes=[pltpu.VMEM((H_A, tq, w), BF16), pltpu.VMEM((tq, w), F32), pltpu.VMEM((H_A, tq, 1), F32)],
        compiler_params=_params(2),
        name="sb_attention",
    )(q, k, v)


def _fox_kernel(q_ref, k_ref, v_ref, fq_ref, fk_ref, o_ref, qm_ref, acc_ref, m_ref, l_ref, *, tq, q_pos0, n_valid):
    q_start = q_pos0 + pl.program_id(1) * tq
    n_full, n_total = _block_range(q_start, tq, n_valid)
    hmask = _head_masks(H_C)
    q = q_ref[...]
    for hh in range(H_C):
        qm_ref[hh] = jnp.where(hmask[hh], q, jnp.zeros_like(q))
    acc_ref[...] = jnp.zeros(acc_ref.shape, F32)
    m_ref[...] = jnp.full(m_ref.shape, -jnp.inf, F32)
    l_ref[...] = jnp.zeros(l_ref.shape, F32)

    def block(j, masked):
        off = pl.multiple_of(j * KEY_BLOCK, KEY_BLOCK)
        kb = k_ref[pl.ds(off, KEY_BLOCK), :]
        vb = v_ref[pl.ds(off, KEY_BLOCK), :]
        fk = fk_ref[j]
        fq = fq_ref[...]
        if masked:
            q_pos, k_pos = _positions(q_start, tq, j, tq)
            mask = k_pos <= q_pos
        for hh in range(H_C):
            z = _dot_nt(qm_ref[hh], kb) + (fq[:, hh:hh + 1] - fk[hh:hh + 1, :])
            if masked:
                z = jnp.where(mask, z, -jnp.inf)
            m_prev = m_ref[hh]
            m_new = jnp.maximum(m_prev, jnp.max(z, axis=-1, keepdims=True))
            m_use = _finite_or_zero(m_new)
            alpha = jnp.exp(m_prev - m_use)
            p = jnp.exp(z - m_use)
            l_ref[hh] = alpha * l_ref[hh] + jnp.sum(p, axis=-1, keepdims=True)
            m_ref[hh] = m_new
            vh = jnp.where(hmask[hh], vb, jnp.zeros_like(vb))
            acc_ref[...] = acc_ref[...] * jnp.where(hmask[hh], alpha, 1.0) + _dot(p.astype(BF16), vh)

    def masked_body(i, carry):
        block(n_total - 1 - i, True)
        return carry

    def full_body(i, carry):
        block(n_full - 1 - i, False)
        return carry

    lax.fori_loop(0, n_total - n_full, masked_body, 0)
    lax.fori_loop(0, n_full, full_body, 0)
    inv = jnp.zeros(acc_ref.shape, F32)
    for hh in range(H_C):
        inv = jnp.where(hmask[hh], 1.0 / l_ref[hh], inv)
    o_ref[...] = acc_ref[...] * inv


def _fox_attention(q, k, v, fq, fk, q_pos0, n_valid):
    b, tq_all, w = q.shape
    tk = k.shape[1]
    tq = min(tq_all, KEY_BLOCK)
    qspec = pl.BlockSpec((None, tq, w), lambda bi, qi: (bi, qi, 0))
    kspec = pl.BlockSpec((None, tk, w), lambda bi, qi: (bi, 0, 0))
    fqspec = pl.BlockSpec((None, tq, fq.shape[2]), lambda bi, qi: (bi, qi, 0))
    fkspec = pl.BlockSpec((None,) + fk.shape[1:], lambda bi, qi: (bi, 0, 0, 0))
    return pl.pallas_call(
        functools.partial(_fox_kernel, tq=tq, q_pos0=q_pos0, n_valid=n_valid),
        grid=(b, tq_all // tq),
        in_specs=[qspec, kspec, kspec, fqspec, fkspec],
        out_specs=qspec,
        out_shape=jax.ShapeDtypeStruct((b, tq_all, w), F32),
        scratch_shapes=[pltpu.VMEM((H_C, tq, w), BF16), pltpu.VMEM((tq, w), F32),
                        pltpu.VMEM((H_C, tq, 1), F32), pltpu.VMEM((H_C, tq, 1), F32)],
        compiler_params=_params(2),
        name="fox_attention",
    )(q, k, v, fq, fk)


def _mla_kernel(q_ref, kl_ref, wuv_ref, o_ref, acc_ref, m_ref, l_ref, *, tq, q_pos0, n_valid):
    q_start = q_pos0 + pl.program_id(1) * tq
    n_full, n_total = _block_range(q_start, tq, n_valid)
    rows = H_B * tq
    qs = q_ref[...].reshape(rows, 2 * LANES)
    acc_ref[...] = jnp.zeros(acc_ref.shape, F32)
    m_ref[...] = jnp.full(m_ref.shape, -jnp.inf, F32)
    l_ref[...] = jnp.zeros(l_ref.shape, F32)

    def block(j, masked):
        off = pl.multiple_of(j * KEY_BLOCK, KEY_BLOCK)
        kb = kl_ref[pl.ds(off, KEY_BLOCK), :]
        z = _dot_nt(qs, kb)
        if masked:
            q_pos, k_pos = _positions(q_start, tq, j, rows)
            mask = (k_pos // CHUNK <= q_pos // CHUNK) & (k_pos < n_valid)
            z = jnp.where(mask, z, -jnp.inf)
        m_prev = m_ref[...]
        m_new = jnp.maximum(m_prev, jnp.max(z, axis=-1, keepdims=True))
        m_use = _finite_or_zero(m_new)
        alpha = jnp.exp(m_prev - m_use)
        p = jnp.exp(z - m_use)
        l_ref[...] = alpha * l_ref[...] + jnp.sum(p, axis=-1, keepdims=True)
        m_ref[...] = m_new
        acc_ref[...] = acc_ref[...] * alpha + _dot(p.astype(BF16), kb[:, :KV_LORA])

    def masked_body(i, carry):
        block(n_total - 1 - i, True)
        return carry

    def full_body(i, carry):
        block(n_full - 1 - i, False)
        return carry

    lax.fori_loop(0, n_total - n_full, masked_body, 0)
    lax.fori_loop(0, n_full, full_body, 0)
    lat = (acc_ref[...] * (1.0 / l_ref[...])).astype(BF16)
    cat = jnp.concatenate([lat[hh * tq:(hh + 1) * tq] for hh in range(H_B)], axis=-1)
    o_ref[...] = _dot(cat, wuv_ref[...])


def _mla_attention(qm, kl, wuv, tq_all, q_pos0, n_valid):
    b, tk, w = kl.shape
    tq = min(tq_all, LANES)
    rows = H_B * tq
    return pl.pallas_call(
        functools.partial(_mla_kernel, tq=tq, q_pos0=q_pos0, n_valid=n_valid),
        grid=(b, tq_all // tq),
        in_specs=[pl.BlockSpec((H_B, None, tq, w), lambda bi, qi: (0, bi, qi, 0)),
                  pl.BlockSpec((None, tk, w), lambda bi, qi: (bi, 0, 0)),
                  pl.BlockSpec(wuv.shape, lambda bi, qi: (0, 0), pipeline_mode=pl.Buffered(1))],
        out_specs=pl.BlockSpec((None, tq, W_B), lambda bi, qi: (bi, qi, 0)),
        out_shape=jax.ShapeDtypeStruct((b, tq_all, W_B), F32),
        scratch_shapes=[pltpu.VMEM((rows, KV_LORA), F32), pltpu.VMEM((rows, 1), F32), pltpu.VMEM((rows, 1), F32)],
        compiler_params=_params(2),
        name="mla_attention",
    )(qm, kl, wuv)


def _mixout_kernel(h_ref, oa_ref, ob_ref, oc_ref, ga_ref, gb_ref, gc_ref, wa_ref, wb_ref, wc_ref, gpost_ref, o_ref):
    m = _dot(_rms(oa_ref[...], ga_ref[...]).astype(BF16), wa_ref[...])
    m = m + _dot(_rms(ob_ref[...], gb_ref[...]).astype(BF16), wb_ref[...])
    m = m + _dot(_rms(oc_ref[...], gc_ref[...]).astype(BF16), wc_ref[...])
    o_ref[...] = h_ref[...] + _rms(m, gpost_ref[...])


def _mixout(h, oa, ob, oc, ga, gb, gc, wa, wb, wc, gpost):
    n, d = h.shape
    tm = _row_tile(n)
    row = lambda w: pl.BlockSpec((tm, w), lambda i: (i, 0))
    consts = [ga, gb, gc, wa, wb, wc, gpost]
    return pl.pallas_call(
        _mixout_kernel,
        grid=(n // tm,),
        in_specs=[row(d), row(W_A), row(W_B), row(W_C)] + [_const_spec(c.shape) for c in consts],
        out_specs=row(d),
        out_shape=jax.ShapeDtypeStruct((n, d), F32),
        compiler_params=_params(1),
        name="mixout",
    )(h, oa, ob, oc, *consts)


def _ple_kernel(h_ref, p_ref, gpre_ref, wg_ref, wp_ref, gpost_ref, o_ref):
    h = h_ref[...]
    gate = jax.nn.sigmoid(_dot(_rms(h, gpre_ref[...]).astype(BF16), wg_ref[...]))
    e = _dot(p_ref[...].astype(BF16), wp_ref[...]) * gate
    o_ref[...] = h + _rms(e, gpost_ref[...])


def _ple(h, p, gpre, wg, wp, gpost):
    n, d = h.shape
    tm = _row_tile(n)
    row = lambda w: pl.BlockSpec((tm, w), lambda i: (i, 0))
    consts = [gpre, wg, wp, gpost]
    return pl.pallas_call(
        _ple_kernel,
        grid=(n // tm,),
        in_specs=[row(d), row(p.shape[1])] + [_const_spec(c.shape) for c in consts],
        out_specs=row(d),
        out_shape=jax.ShapeDtypeStruct((n, d), F32),
        compiler_params=_params(1),
        name="ple",
    )(h, p, *consts)


def _rope_tables(pos):
    half = MLA_ROPE // 2
    inv = ROPE_THETA ** (-jnp.arange(half, dtype=F32) / half)
    ang = pos.astype(F32)[:, None] * inv[None, :]
    cos, sin = jnp.cos(ang), jnp.sin(ang)
    pad = jnp.zeros((pos.shape[0], LANES - MLA_ROPE), F32)
    return jnp.concatenate([cos, cos, pad], axis=1), jnp.concatenate([-sin, sin, pad], axis=1)


def _prep_layer(lw):
    (g_ff1_pre, g_ff1_post, w_ff1_gu, w_ff1_down, g_mix_pre, g_mix_post, w_in, b_f, g_bq, g_bkv, w_uq, w_ukv,
     g_grp, w_out, g_ff2_pre, g_ff2_post, w_ff2_gu, w_ff2_down, g_ple_pre, w_ple_gate, w_ple_proj, g_ple_post) = lw
    d = w_in.shape[0]
    half = MLA_ROPE // 2
    row = lambda g: g.reshape(1, -1).astype(F32)
    c_kr = 3 * W_A + Q_LORA + KV_LORA
    c_c = c_kr + MLA_ROPE
    c_f = c_c + 3 * W_C
    kr = w_in[:, c_kr:c_kr + MLA_ROPE]
    kr_sw = jnp.concatenate([kr[:, half:], kr[:, :half]], axis=1)
    zpad = jnp.zeros((d, LANES - MLA_ROPE), F32)
    win_p = jnp.concatenate([w_in[:, :c_kr], kr, zpad, kr_sw, zpad, w_in[:, c_c:c_f], w_in[:, c_f:],
                             jnp.zeros((d, LANES - H_C), F32)], axis=1).astype(BF16)
    assert win_p.shape[1] == IN_COLS_P
    bf_p = jnp.concatenate([b_f.astype(F32), jnp.zeros((LANES - H_C,), F32)]).reshape(1, LANES)

    wq3 = w_uq.reshape(Q_LORA, H_B, MLA_NOPE + MLA_ROPE)
    wkv3 = w_ukv.reshape(KV_LORA, H_B, MLA_NOPE + MLA_V)
    wq_n = jnp.transpose(wq3[:, :, :MLA_NOPE], (1, 0, 2)).astype(BF16)
    wk_n = jnp.transpose(wkv3[:, :, :MLA_NOPE], (1, 0, 2)).astype(BF16)
    wcomb = _wcomb(wq_n, wk_n)
    x1 = wq3[:, :, MLA_NOPE:MLA_NOPE + half]
    x2 = wq3[:, :, MLA_NOPE + half:]
    zq = jnp.zeros((Q_LORA, H_B, LANES - MLA_ROPE), F32)
    wqr = jnp.concatenate([jnp.concatenate([x1, x2, zq], axis=2).reshape(Q_LORA, H_B * LANES),
                           jnp.concatenate([x2, x1, zq], axis=2).reshape(Q_LORA, H_B * LANES)], axis=1).astype(BF16)
    wuv = jnp.zeros((H_B, KV_LORA, H_B, MLA_V), F32)
    wv = jnp.transpose(wkv3[:, :, MLA_NOPE:], (1, 0, 2))
    wuv = wuv.at[jnp.arange(H_B), :, jnp.arange(H_B), :].set(wv).reshape(H_B * KV_LORA, W_B).astype(BF16)

    return dict(
        ff1=(row(g_ff1_pre), row(g_ff1_post), w_ff1_gu.astype(BF16), w_ff1_down.astype(BF16)),
        ff2=(row(g_ff2_pre), row(g_ff2_post), w_ff2_gu.astype(BF16), w_ff2_down.astype(BF16)),
        inproj=(row(g_mix_pre), win_p, bf_p, row(g_bq), row(g_bkv), wcomb, wqr),
        wuv=wuv,
        mixout=(row(g_grp[:W_A]), row(g_grp[W_A:W_A + W_B]), row(g_grp[W_A + W_B:]),
                w_out[:W_A].astype(BF16), w_out[W_A:W_A + W_B].astype(BF16), w_out[W_A + W_B:].astype(BF16),
                row(g_mix_post)),
        ple=(row(g_ple_pre), w_ple_gate.astype(BF16), w_ple_proj.astype(BF16), row(g_ple_post)),
    )


def _pad_keys(a, tk_pad):
    return jnp.pad(a, ((0, 0), (0, tk_pad - a.shape[1])) + ((0, 0),) * (a.ndim - 2))


def _forget_sums(logf_all, tk_pad):
    b, tk, _ = logf_all.shape
    x = jnp.transpose(logf_all, (0, 2, 1))
    x = jnp.pad(x, ((0, 0), (0, F_ROWS - H_C), (0, tk_pad - tk)))
    return _cumsum_rows(x)


def _layer(h, p, seq_len, q_pos0, past, lp, tables):
    n, d = h.shape
    b = n // seq_len
    h = _ffn(h, *lp["ff1"])
    (ka_st, va_st, ckv_st, kr_st, kc_st, vc_st, lf_st,
     qa_b, ka_b, va_b, qm_b, kl_b, qc_b, kc_b, vc_b) = _inproj(h, *lp["inproj"], *tables)
    state = (ka_st.reshape(b, seq_len, H_A, HEAD_DIM), va_st.reshape(b, seq_len, H_A, HEAD_DIM),
             ckv_st.reshape(b, seq_len, KV_LORA), kr_st.reshape(b, seq_len, MLA_ROPE),
             kc_st.reshape(b, seq_len, H_C, HEAD_DIM), vc_st.reshape(b, seq_len, H_C, HEAD_DIM),
             lf_st.reshape(b, seq_len, H_C))
    seq3 = lambda a: a.reshape(b, seq_len, a.shape[-1])
    if past is None:
        n_valid = seq_len
        tk_pad = seq_len
        ka_all, va_all, kl_all, kc_all, vc_all = (seq3(a) for a in (ka_b, va_b, kl_b, kc_b, vc_b))
        lf_all = state[6]
    else:
        pa_k, pa_v, pb_ckv, pb_kr, pc_k, pc_v, pc_lf = past
        past_len = pa_k.shape[1]
        n_valid = past_len + seq_len
        tk_pad = -(-n_valid // KEY_BLOCK) * KEY_BLOCK
        join = lambda c, new: _pad_keys(jnp.concatenate([c.reshape(b, past_len, -1).astype(BF16), seq3(new)], axis=1), tk_pad)
        ka_all, va_all, kc_all, vc_all = join(pa_k, ka_b), join(pa_v, va_b), join(pc_k, kc_b), join(pc_v, vc_b)
        kl_past = jnp.concatenate([pb_ckv, pb_kr, jnp.zeros((b, past_len, LANES - MLA_ROPE), F32)], axis=-1)
        kl_all = join(kl_past, kl_b)
        lf_all = jnp.concatenate([pc_lf, state[6]], axis=1)
    assert tk_pad % KEY_BLOCK == 0
    f_rows = _forget_sums(lf_all, tk_pad)
    f_q = jnp.transpose(f_rows[:, :, q_pos0:q_pos0 + seq_len], (0, 2, 1))
    f_k = jnp.transpose(f_rows.reshape(b, F_ROWS, tk_pad // KEY_BLOCK, KEY_BLOCK), (0, 2, 1, 3))

    oa = _sb_attention(seq3(qa_b), ka_all, va_all, q_pos0, n_valid)
    ob = _mla_attention(qm_b.reshape(H_B, b, seq_len, 2 * LANES), kl_all, lp["wuv"], seq_len, q_pos0, n_valid)
    oc = _fox_attention(seq3(qc_b), kc_all, vc_all, f_q, f_k, q_pos0, n_valid)
    flat = lambda a: a.reshape(n, a.shape[-1])
    h = _mixout(h, flat(oa), flat(ob), flat(oc), *lp["mixout"])
    h = _ffn(h, *lp["ff2"])
    h = _ple(h, p, *lp["ple"])
    return h, state


def _trunk(x, p, q_pos0, caches, layers):
    b, t, d = x.shape
    depth = len(layers)
    tables = tuple(jnp.tile(tab, (b, 1)) for tab in _rope_tables(q_pos0 + jnp.arange(t, dtype=jnp.int32)))
    h = x.reshape(b * t, d)
    states = []
    for i in range(depth):
        past = None if caches is None else [c[i] for c in caches]
        h, st = _layer(h, p[i].reshape(b * t, -1), t, q_pos0, past, layers[i], tables)
        states.append(st)
    stacked = [jnp.stack([st[j] for st in states]) for j in range(len(states[0]))]
    return h.reshape(b, t, d), stacked


def kernel(x_prompt, x_sample, p_prompt, p_sample, cache_a_k, cache_a_v, cache_b_ckv, cache_b_krope, cache_c_k, cache_c_v, cache_c_logf, g_ff1_pre, g_ff1_post, w_ff1_gu, w_ff1_down, g_mix_pre, g_mix_post, w_in, b_f, g_bq, g_bkv, w_uq, w_ukv, g_grp, w_out, g_ff2_pre, g_ff2_post, w_ff2_gu, w_ff2_down, g_ple_pre, w_ple_gate, w_ple_proj, g_ple_post):
    weights = (g_ff1_pre, g_ff1_post, w_ff1_gu, w_ff1_down, g_mix_pre, g_mix_post, w_in, b_f,
               g_bq, g_bkv, w_uq, w_ukv, g_grp, w_out, g_ff2_pre, g_ff2_post, w_ff2_gu, w_ff2_down,
               g_ple_pre, w_ple_gate, w_ple_proj, g_ple_post)
    depth = w_in.shape[0]
    layers = [_prep_layer([w[i] for w in weights]) for i in range(depth)]
    y_prompt, sp = _trunk(x_prompt, p_prompt, 0, None, layers)
    caches = (cache_a_k, cache_a_v, cache_b_ckv, cache_b_krope, cache_c_k, cache_c_v, cache_c_logf)
    y_sample, ss = _trunk(x_sample, p_sample, cache_a_k.shape[2], caches, layers)
    return (y_prompt, y_sample, *sp, *ss)
```

```python
import functools
import math

import jax
import jax.numpy as jnp
from jax import lax
from jax.experimental import pallas as pl
from jax.experimental.pallas import tpu as pltpu

CHUNK = 64
HEAD_DIM = 64
H_A = 4
H_B = 8
H_C = 4
W_A = H_A * HEAD_DIM
MLA_NOPE = 64
MLA_ROPE = 32
MLA_V = 64
W_B = H_B * MLA_V
W_C = H_C * HEAD_DIM
Q_LORA = 256
KV_LORA = 128
ROPE_THETA = 10000.0
EPS = 1e-6
FFN_RES = 0.5
SB_SCALE = HEAD_DIM ** -0.5
MLA_SCALE = (MLA_NOPE + MLA_ROPE) ** -0.5
FOX_SCALE = HEAD_DIM ** -0.5

LANES = 128
KEY_BLOCK = 256
F_ROWS = 16
VMEM_LIMIT = 56 * 1024 * 1024

COL_A = 0
COL_CQ = COL_A + 3 * W_A
COL_CKV = COL_CQ + Q_LORA
COL_KRA = COL_CKV + KV_LORA
COL_KRB = COL_KRA + LANES
COL_C = COL_KRB + LANES
COL_F = COL_C + 3 * W_C
IN_COLS_P = COL_F + LANES

BF16 = jnp.bfloat16
F32 = jnp.float32


def _dot(a, b):
    return jnp.dot(a, b, preferred_element_type=F32)


def _dot_nt(a, b):
    return lax.dot_general(a, b, (((1,), (1,)), ((), ())), preferred_element_type=F32)


def _rms(x, g):
    ms = jnp.mean(x * x, axis=-1, keepdims=True)
    return x * lax.rsqrt(ms + EPS) * g


def _log_sigmoid(x):
    return jnp.minimum(x, 0.0) - jnp.log1p(jnp.exp(-jnp.abs(x)))


def _split2(x):
    hi = x.astype(BF16)
    lo = (x - hi.astype(F32)).astype(BF16)
    return hi, lo


def _const_spec(shape):
    nd = len(shape)
    return pl.BlockSpec(shape, lambda *_: (0,) * nd, pipeline_mode=pl.Buffered(1))


def _params(n_axes):
    return pltpu.CompilerParams(dimension_semantics=("arbitrary",) * n_axes, vmem_limit_bytes=VMEM_LIMIT)


def _row_tile(n):
    for tm in (512, 256):
        if n % tm == 0:
            return tm
    return n


def _ffn_kernel(h_ref, gpre_ref, gpost_ref, wgu_ref, wdown_ref, o_ref, *, d_ff, chunks):
    h = h_ref[...]
    xn = _rms(h, gpre_ref[...]).astype(BF16)
    acc = jnp.zeros(h.shape, F32)
    for c0, c1 in chunks:
        g = _dot(xn, wgu_ref[:, c0:c1])
        u = _dot(xn, wgu_ref[:, d_ff + c0:d_ff + c1])
        a = (g * jax.nn.sigmoid(g) * u).astype(BF16)
        acc = acc + _dot(a, wdown_ref[c0:c1, :])
    o_ref[...] = h + FFN_RES * _rms(acc, gpost_ref[...])


def _ffn(h, gpre, gpost, wgu, wdown):
    n, d = h.shape
    d_ff = wdown.shape[0]
    tm = _row_tile(n)
    step = 4 * KEY_BLOCK
    chunks = tuple((c, min(c + step, d_ff)) for c in range(0, d_ff, step))
    row = pl.BlockSpec((tm, d), lambda i: (i, 0))
    return pl.pallas_call(
        functools.partial(_ffn_kernel, d_ff=d_ff, chunks=chunks),
        grid=(n // tm,),
        in_specs=[row, _const_spec(gpre.shape), _const_spec(gpost.shape), _const_spec(wgu.shape), _const_spec(wdown.shape)],
        out_specs=row,
        out_shape=jax.ShapeDtypeStruct((n, d), F32),
        compiler_params=_params(1),
        name="ffn",
    )(h, gpre, gpost, wgu, wdown)


def _wcomb_kernel(wq_ref, wk_ref, o_ref):
    o_ref[...] = _dot_nt(wq_ref[...], wk_ref[...]).astype(BF16)


def _wcomb(wq_n, wk_n):
    nh, ql, dn = wq_n.shape
    kl = wk_n.shape[1]
    return pl.pallas_call(
        _wcomb_kernel,
        grid=(nh,),
        in_specs=[pl.BlockSpec((None, ql, dn), lambda h: (h, 0, 0)), pl.BlockSpec((None, kl, dn), lambda h: (h, 0, 0))],
        out_specs=pl.BlockSpec((ql, kl), lambda h: (0, h)),
        out_shape=jax.ShapeDtypeStruct((ql, nh * kl), BF16),
        compiler_params=_params(1),
        name="wcomb",
    )(wq_n, wk_n)


def _inproj_kernel(h_ref, gpre_ref, win_ref, bf_ref, gbq_ref, gbkv_ref, wcomb_ref, wqr_ref, cos_ref, sin_ref,
                   ka_st, va_st, ckv_st, kr_st, kc_st, vc_st, lf_st,
                   qa_b, ka_b, va_b, qm_b, kl_b, qc_b, kc_b, vc_b):
    xn = _rms(h_ref[...], gpre_ref[...]).astype(BF16)
    proj = _dot(xn, win_ref[...])
    cos = cos_ref[...]
    sin = sin_ref[...]

    qa_b[...] = (proj[:, COL_A:COL_A + W_A] * SB_SCALE).astype(BF16)
    ka = proj[:, COL_A + W_A:COL_A + 2 * W_A]
    va = proj[:, COL_A + 2 * W_A:COL_A + 3 * W_A]
    ka_st[...] = ka
    va_st[...] = va
    ka_b[...] = ka.astype(BF16)
    va_b[...] = va.astype(BF16)

    qc_b[...] = (proj[:, COL_C:COL_C + W_C] * FOX_SCALE).astype(BF16)
    kc = proj[:, COL_C + W_C:COL_C + 2 * W_C]
    vc = proj[:, COL_C + 2 * W_C:COL_C + 3 * W_C]
    kc_st[...] = kc
    vc_st[...] = vc
    kc_b[...] = kc.astype(BF16)
    vc_b[...] = vc.astype(BF16)

    lf = _log_sigmoid(proj[:, COL_F:COL_F + LANES] + bf_ref[...])
    lf_st[...] = lf[:, :H_C]

    ckv = _rms(proj[:, COL_CKV:COL_CKV + KV_LORA], gbkv_ref[...])
    ckv_st[...] = ckv
    kr = proj[:, COL_KRA:COL_KRA + LANES] * cos + proj[:, COL_KRB:COL_KRB + LANES] * sin
    kr_st[...] = kr[:, :MLA_ROPE]
    kl_b[:, :KV_LORA] = ckv.astype(BF16)
    kl_b[:, KV_LORA:] = kr.astype(BF16)

    cqn = _rms(proj[:, COL_CQ:COL_CQ + Q_LORA], gbq_ref[...]).astype(BF16)
    qlat = _dot(cqn, wcomb_ref[...])
    qr = _dot(cqn, wqr_ref[...])
    half = H_B * LANES
    for hh in range(H_B):
        sl = slice(hh * LANES, (hh + 1) * LANES)
        rope = qr[:, sl] * cos + qr[:, half + hh * LANES:half + (hh + 1) * LANES] * sin
        qm_b[hh, :, :KV_LORA] = (qlat[:, sl] * MLA_SCALE).astype(BF16)
        qm_b[hh, :, KV_LORA:] = (rope * MLA_SCALE).astype(BF16)


def _inproj(h, gpre, win, bf, gbq, gbkv, wcomb, wqr, cos_t, sin_t):
    n, d = h.shape
    tm = _row_tile(n)
    row = lambda w: pl.BlockSpec((tm, w), lambda i: (i, 0))
    tab = row(LANES)
    outs = [
        (row(W_A), (n, W_A), F32), (row(W_A), (n, W_A), F32), (row(KV_LORA), (n, KV_LORA), F32),
        (row(MLA_ROPE), (n, MLA_ROPE), F32), (row(W_C), (n, W_C), F32), (row(W_C), (n, W_C), F32),
        (row(H_C), (n, H_C), F32),
        (row(W_A), (n, W_A), BF16), (row(W_A), (n, W_A), BF16), (row(W_A), (n, W_A), BF16),
        (pl.BlockSpec((H_B, tm, 2 * LANES), lambda i: (0, i, 0)), (H_B, n, 2 * LANES), BF16),
        (row(2 * LANES), (n, 2 * LANES), BF16),
        (row(W_C), (n, W_C), BF16), (row(W_C), (n, W_C), BF16), (row(W_C), (n, W_C), BF16),
    ]
    return pl.pallas_call(
        _inproj_kernel,
        grid=(n // tm,),
        in_specs=[row(d), _const_spec(gpre.shape), _const_spec(win.shape), _const_spec(bf.shape),
                  _const_spec(gbq.shape), _const_spec(gbkv.shape), _const_spec(wcomb.shape), _const_spec(wqr.shape),
                  tab, tab],
        out_specs=[o[0] for o in outs],
        out_shape=[jax.ShapeDtypeStruct(o[1], o[2]) for o in outs],
        compiler_params=_params(1),
        name="inproj",
    )(h, gpre, win, bf, gbq, gbkv, wcomb, wqr, cos_t, sin_t)


def _cumsum_kernel(x_ref, o_ref, *, n_blocks):
    r = lax.broadcasted_iota(jnp.int32, (KEY_BLOCK, KEY_BLOCK), 0)
    c = lax.broadcasted_iota(jnp.int32, (KEY_BLOCK, KEY_BLOCK), 1)
    upper = jnp.where(r <= c, 1.0, 0.0).astype(BF16)
    carry = jnp.zeros((x_ref.shape[0], 1), F32)
    for j in range(n_blocks):
        sl = slice(j * KEY_BLOCK, (j + 1) * KEY_BLOCK)
        x = x_ref[:, sl]
        hi = x.astype(BF16)
        mid, lo = _split2(x - hi.astype(F32))
        f = _dot(hi, upper) + _dot(mid, upper) + _dot(lo, upper) + carry
        o_ref[:, sl] = f
        carry = f[:, KEY_BLOCK - 1:KEY_BLOCK]


def _cumsum_rows(x):
    b, r, tp = x.shape
    spec = pl.BlockSpec((None, r, tp), lambda i: (i, 0, 0))
    return pl.pallas_call(
        functools.partial(_cumsum_kernel, n_blocks=tp // KEY_BLOCK),
        grid=(b,),
        in_specs=[spec],
        out_specs=spec,
        out_shape=jax.ShapeDtypeStruct(x.shape, F32),
        compiler_params=_params(1),
        name="cumsum_logf",
    )(x)


def _block_range(q_start, tq, n_valid):
    n_full = q_start // KEY_BLOCK
    last = jnp.minimum(((q_start + tq + CHUNK - 1) // CHUNK) * CHUNK, n_valid)
    n_total = (last + KEY_BLOCK - 1) // KEY_BLOCK
    return n_full, n_total


def _positions(q_start, tq, j, cols):
    k_pos = j * KEY_BLOCK + lax.broadcasted_iota(jnp.int32, (KEY_BLOCK, cols), 0)
    lane = lax.broadcasted_iota(jnp.int32, (KEY_BLOCK, cols), 1)
    q_pos = q_start + (lane if cols == tq else lane % tq)
    return q_pos, k_pos


def _transpose_blocks(src_ref, dst_ref, width):
    for j in range(dst_ref.shape[0]):
        blk = src_ref[j * KEY_BLOCK:(j + 1) * KEY_BLOCK, :width].astype(F32)
        dst_ref[j] = blk.T.astype(BF16)


def _finite_or_zero(m):
    return jnp.where(m == -jnp.inf, 0.0, m)


def _head_masks(n_heads):
    lane = lax.broadcasted_iota(jnp.int32, (1, n_heads * HEAD_DIM), 1)
    return [(lane >= hh * HEAD_DIM) & (lane < (hh + 1) * HEAD_DIM) for hh in range(n_heads)]


def _sb_kernel(q_ref, k_ref, v_ref, o_ref, qm_ref, vt_ref, acc_ref, c_ref, *, tq, q_pos0, n_valid):
    @pl.when(pl.program_id(1) == 0)
    def _():
        _transpose_blocks(v_ref, vt_ref, v_ref.shape[1])

    q_start = q_pos0 + pl.program_id(1) * tq
    n_full, n_total = _block_range(q_start, tq, n_valid)
    hmask = _head_masks(H_A)
    q = q_ref[...]
    for hh in range(H_A):
        qm_ref[hh] = jnp.where(hmask[hh], q, jnp.zeros_like(q))
    acc_ref[...] = jnp.zeros(acc_ref.shape, F32)
    c_ref[...] = jnp.zeros(c_ref.shape, F32)
    r = lax.broadcasted_iota(jnp.int32, (KEY_BLOCK, KEY_BLOCK), 0)
    c = lax.broadcasted_iota(jnp.int32, (KEY_BLOCK, KEY_BLOCK), 1)
    upper = jnp.where(c >= r, 1.0, 0.0).astype(BF16)

    def block(j, masked):
        off = pl.multiple_of(j * KEY_BLOCK, KEY_BLOCK)
        kb = k_ref[pl.ds(off, KEY_BLOCK), :]
        if masked:
            q_pos, k_pos = _positions(q_start, tq, j, tq)
            mask = k_pos < q_pos
        for hh in range(H_A):
            rows = slice(hh * HEAD_DIM, (hh + 1) * HEAD_DIM)
            z = _dot_nt(kb, qm_ref[hh])
            ls = _log_sigmoid(z)
            l1m = ls - z
            if masked:
                l1m = jnp.where(mask, l1m, 0.0)
            hi, lo = _split2(l1m)
            cum = _dot(upper, hi) + _dot(upper, lo)
            w = jnp.exp(ls + (cum - l1m) + c_ref[hh:hh + 1, :])
            if masked:
                w = jnp.where(mask, w, 0.0)
            acc_ref[rows, :] += _dot(vt_ref[j, rows, :], w.astype(BF16))
            c_ref[hh:hh + 1, :] += cum[0:1, :]

    def masked_body(i, carry):
        block(n_total - 1 - i, True)
        return carry

    def full_body(i, carry):
        block(n_full - 1 - i, False)
        return carry

    lax.fori_loop(0, n_total - n_full, masked_body, 0)
    lax.fori_loop(0, n_full, full_body, 0)
    o_ref[...] = acc_ref[...].T


def _sb_attention(q, k, v, q_pos0, n_valid):
    b, tq_all, w = q.shape
    tk = k.shape[1]
    tq = min(tq_all, KEY_BLOCK)
    qspec = pl.BlockSpec((None, tq, w), lambda bi, qi: (bi, qi, 0))
    kspec = pl.BlockSpec((None, tk, w), lambda bi, qi: (bi, 0, 0))
    return pl.pallas_call(
        functools.partial(_sb_kernel, tq=tq, q_pos0=q_pos0, n_valid=n_valid),
        grid=(b, tq_all // tq),
        in_specs=[qspec, kspec, kspec],
        out_specs=qspec,
        out_shape=jax.ShapeDtypeStruct((b, tq_all, w), F32),
        scratch_shapes=[pltpu.VMEM((H_A, tq, w), BF16), pltpu.VMEM((tk // KEY_BLOCK, w, KEY_BLOCK), BF16),
                        pltpu.VMEM((w, tq), F32), pltpu.VMEM((8, tq), F32)],
        compiler_params=_params(2),
        name="sb_attention",
    )(q, k, v)


def _fox_kernel(q_ref, k_ref, v_ref, fq_ref, fk_ref, o_ref, qm_ref, vt_ref, acc_ref, m_ref, l_ref, *, tq, q_pos0, n_valid):
    @pl.when(pl.program_id(1) == 0)
    def _():
        _transpose_blocks(v_ref, vt_ref, v_ref.shape[1])

    q_start = q_pos0 + pl.program_id(1) * tq
    n_full, n_total = _block_range(q_start, tq, n_valid)
    hmask = _head_masks(H_C)
    q = q_ref[...]
    for hh in range(H_C):
        qm_ref[hh] = jnp.where(hmask[hh], q, jnp.zeros_like(q))
    acc_ref[...] = jnp.zeros(acc_ref.shape, F32)
    m_ref[...] = jnp.full(m_ref.shape, -jnp.inf, F32)
    l_ref[...] = jnp.zeros(l_ref.shape, F32)

    def block(j, masked):
        off = pl.multiple_of(j * KEY_BLOCK, KEY_BLOCK)
        kb = k_ref[pl.ds(off, KEY_BLOCK), :]
        fk = fk_ref[pl.ds(off, KEY_BLOCK), :]
        if masked:
            q_pos, k_pos = _positions(q_start, tq, j, tq)
            mask = k_pos <= q_pos
        for hh in range(H_C):
            rows = slice(hh * HEAD_DIM, (hh + 1) * HEAD_DIM)
            z = _dot_nt(kb, qm_ref[hh]) + (fq_ref[hh:hh + 1, :] - fk[:, hh:hh + 1])
            if masked:
                z = jnp.where(mask, z, -jnp.inf)
            m_prev = m_ref[hh:hh + 1, :]
            m_new = jnp.maximum(m_prev, jnp.max(z, axis=0, keepdims=True))
            m_use = _finite_or_zero(m_new)
            alpha = jnp.exp(m_prev - m_use)
            p = jnp.exp(z - m_use)
            l_ref[hh:hh + 1, :] = alpha * l_ref[hh:hh + 1, :] + jnp.sum(p, axis=0, keepdims=True)
            m_ref[hh:hh + 1, :] = m_new
            acc_ref[rows, :] = acc_ref[rows, :] * alpha + _dot(vt_ref[j, rows, :], p.astype(BF16))

    def masked_body(i, carry):
        block(n_total - 1 - i, True)
        return carry

    def full_body(i, carry):
        block(n_full - 1 - i, False)
        return carry

    lax.fori_loop(0, n_total - n_full, masked_body, 0)
    lax.fori_loop(0, n_full, full_body, 0)
    for hh in range(H_C):
        rows = slice(hh * HEAD_DIM, (hh + 1) * HEAD_DIM)
        acc_ref[rows, :] = acc_ref[rows, :] * (1.0 / l_ref[hh:hh + 1, :])
    o_ref[...] = acc_ref[...].T


def _fox_attention(q, k, v, fq, fk, q_pos0, n_valid):
    b, tq_all, w = q.shape
    tk = k.shape[1]
    tq = min(tq_all, KEY_BLOCK)
    qspec = pl.BlockSpec((None, tq, w), lambda bi, qi: (bi, qi, 0))
    kspec = pl.BlockSpec((None, tk, w), lambda bi, qi: (bi, 0, 0))
    fqspec = pl.BlockSpec((None, F_ROWS, tq), lambda bi, qi: (bi, 0, qi))
    fkspec = pl.BlockSpec((None, tk, F_ROWS), lambda bi, qi: (bi, 0, 0))
    return pl.pallas_call(
        functools.partial(_fox_kernel, tq=tq, q_pos0=q_pos0, n_valid=n_valid),
        grid=(b, tq_all // tq),
        in_specs=[qspec, kspec, kspec, fqspec, fkspec],
        out_specs=qspec,
        out_shape=jax.ShapeDtypeStruct((b, tq_all, w), F32),
        scratch_shapes=[pltpu.VMEM((H_C, tq, w), BF16), pltpu.VMEM((tk // KEY_BLOCK, w, KEY_BLOCK), BF16),
                        pltpu.VMEM((w, tq), F32), pltpu.VMEM((8, tq), F32), pltpu.VMEM((8, tq), F32)],
        compiler_params=_params(2),
        name="fox_attention",
    )(q, k, v, fq, fk)


def _mla_kernel(q_ref, kl_ref, wuvt_ref, o_ref, ct_ref, acc_ref, m_ref, l_ref, *, tq, q_pos0, n_valid):
    @pl.when(pl.program_id(1) == 0)
    def _():
        _transpose_blocks(kl_ref, ct_ref, KV_LORA)

    q_start = q_pos0 + pl.program_id(1) * tq
    n_full, n_total = _block_range(q_start, tq, n_valid)
    cols = H_B * tq
    qs = q_ref[...].reshape(cols, 2 * LANES)
    acc_ref[...] = jnp.zeros(acc_ref.shape, F32)
    m_ref[...] = jnp.full(m_ref.shape, -jnp.inf, F32)
    l_ref[...] = jnp.zeros(l_ref.shape, F32)

    def block(j, masked):
        off = pl.multiple_of(j * KEY_BLOCK, KEY_BLOCK)
        z = _dot_nt(kl_ref[pl.ds(off, KEY_BLOCK), :], qs)
        if masked:
            q_pos, k_pos = _positions(q_start, tq, j, cols)
            mask = (k_pos // CHUNK <= q_pos // CHUNK) & (k_pos < n_valid)
            z = jnp.where(mask, z, -jnp.inf)
        m_prev = m_ref[...]
        m_new = jnp.maximum(m_prev, jnp.max(z, axis=0, keepdims=True))
        m_use = _finite_or_zero(m_new)
        alpha = jnp.exp(m_prev - m_use)
        p = jnp.exp(z - m_use)
        l_ref[...] = alpha * l_ref[...] + jnp.sum(p, axis=0, keepdims=True)
        m_ref[...] = m_new
        acc_ref[...] = acc_ref[...] * alpha + _dot(ct_ref[j], p.astype(BF16))

    def masked_body(i, carry):
        block(n_total - 1 - i, True)
        return carry

    def full_body(i, carry):
        block(n_full - 1 - i, False)
        return carry

    lax.fori_loop(0, n_total - n_full, masked_body, 0)
    lax.fori_loop(0, n_full, full_body, 0)
    lat = (acc_ref[...] * (1.0 / l_ref[...])).astype(BF16)
    heads = [_dot(wuvt_ref[hh], lat[:, hh * tq:(hh + 1) * tq]) for hh in range(H_B)]
    o_ref[...] = jnp.concatenate(heads, axis=0).T


def _mla_attention(qm, kl, wuvt, tq_all, q_pos0, n_valid):
    b, tk, w = kl.shape
    tq = min(tq_all, LANES)
    cols = H_B * tq
    return pl.pallas_call(
        functools.partial(_mla_kernel, tq=tq, q_pos0=q_pos0, n_valid=n_valid),
        grid=(b, tq_all // tq),
        in_specs=[pl.BlockSpec((H_B, None, tq, w), lambda bi, qi: (0, bi, qi, 0)),
                  pl.BlockSpec((None, tk, w), lambda bi, qi: (bi, 0, 0)),
                  _const_spec(wuvt.shape)],
        out_specs=pl.BlockSpec((None, tq, W_B), lambda bi, qi: (bi, qi, 0)),
        out_shape=jax.ShapeDtypeStruct((b, tq_all, W_B), F32),
        scratch_shapes=[pltpu.VMEM((tk // KEY_BLOCK, KV_LORA, KEY_BLOCK), BF16), pltpu.VMEM((KV_LORA, cols), F32),
                        pltpu.VMEM((1, cols), F32), pltpu.VMEM((1, cols), F32)],
        compiler_params=_params(2),
        name="mla_attention",
    )(qm, kl, wuvt)


def _mixout_kernel(h_ref, oa_ref, ob_ref, oc_ref, ga_ref, gb_ref, gc_ref, wa_ref, wb_ref, wc_ref, gpost_ref, o_ref):
    m = _dot(_rms(oa_ref[...], ga_ref[...]).astype(BF16), wa_ref[...])
    m = m + _dot(_rms(ob_ref[...], gb_ref[...]).astype(BF16), wb_ref[...])
    m = m + _dot(_rms(oc_ref[...], gc_ref[...]).astype(BF16), wc_ref[...])
    o_ref[...] = h_ref[...] + _rms(m, gpost_ref[...])


def _mixout(h, oa, ob, oc, ga, gb, gc, wa, wb, wc, gpost):
    n, d = h.shape
    tm = _row_tile(n)
    row = lambda w: pl.BlockSpec((tm, w), lambda i: (i, 0))
    consts = [ga, gb, gc, wa, wb, wc, gpost]
    return pl.pallas_call(
        _mixout_kernel,
        grid=(n // tm,),
        in_specs=[row(d), row(W_A), row(W_B), row(W_C)] + [_const_spec(c.shape) for c in consts],
        out_specs=row(d),
        out_shape=jax.ShapeDtypeStruct((n, d), F32),
        compiler_params=_params(1),
        name="mixout",
    )(h, oa, ob, oc, *consts)


def _ple_kernel(h_ref, p_ref, gpre_ref, wg_ref, wp_ref, gpost_ref, o_ref):
    h = h_ref[...]
    gate = jax.nn.sigmoid(_dot(_rms(h, gpre_ref[...]).astype(BF16), wg_ref[...]))
    e = _dot(p_ref[...].astype(BF16), wp_ref[...]) * gate
    o_ref[...] = h + _rms(e, gpost_ref[...])


def _ple(h, p, gpre, wg, wp, gpost):
    n, d = h.shape
    tm = _row_tile(n)
    row = lambda w: pl.BlockSpec((tm, w), lambda i: (i, 0))
    consts = [gpre, wg, wp, gpost]
    return pl.pallas_call(
        _ple_kernel,
        grid=(n // tm,),
        in_specs=[row(d), row(p.shape[1])] + [_const_spec(c.shape) for c in consts],
        out_specs=row(d),
        out_shape=jax.ShapeDtypeStruct((n, d), F32),
        compiler_params=_params(1),
        name="ple",
    )(h, p, *consts)


def _rope_tables(pos):
    half = MLA_ROPE // 2
    inv = ROPE_THETA ** (-jnp.arange(half, dtype=F32) / half)
    ang = pos.astype(F32)[:, None] * inv[None, :]
    cos, sin = jnp.cos(ang), jnp.sin(ang)
    pad = jnp.zeros((pos.shape[0], LANES - MLA_ROPE), F32)
    return jnp.concatenate([cos, cos, pad], axis=1), jnp.concatenate([-sin, sin, pad], axis=1)


def _prep_layer(lw):
    (g_ff1_pre, g_ff1_post, w_ff1_gu, w_ff1_down, g_mix_pre, g_mix_post, w_in, b_f, g_bq, g_bkv, w_uq, w_ukv,
     g_grp, w_out, g_ff2_pre, g_ff2_post, w_ff2_gu, w_ff2_down, g_ple_pre, w_ple_gate, w_ple_proj, g_ple_post) = lw
    d = w_in.shape[0]
    half = MLA_ROPE // 2
    row = lambda g: g.reshape(1, -1).astype(F32)
    c_kr = 3 * W_A + Q_LORA + KV_LORA
    c_c = c_kr + MLA_ROPE
    c_f = c_c + 3 * W_C
    kr = w_in[:, c_kr:c_kr + MLA_ROPE]
    kr_sw = jnp.concatenate([kr[:, half:], kr[:, :half]], axis=1)
    zpad = jnp.zeros((d, LANES - MLA_ROPE), F32)
    win_p = jnp.concatenate([w_in[:, :c_kr], kr, zpad, kr_sw, zpad, w_in[:, c_c:c_f], w_in[:, c_f:],
                             jnp.zeros((d, LANES - H_C), F32)], axis=1).astype(BF16)
    assert win_p.shape[1] == IN_COLS_P
    bf_p = jnp.concatenate([b_f.astype(F32), jnp.zeros((LANES - H_C,), F32)]).reshape(1, LANES)

    wq3 = w_uq.reshape(Q_LORA, H_B, MLA_NOPE + MLA_ROPE)
    wkv3 = w_ukv.reshape(KV_LORA, H_B, MLA_NOPE + MLA_V)
    wq_n = jnp.transpose(wq3[:, :, :MLA_NOPE], (1, 0, 2)).astype(BF16)
    wk_n = jnp.transpose(wkv3[:, :, :MLA_NOPE], (1, 0, 2)).astype(BF16)
    wcomb = _wcomb(wq_n, wk_n)
    x1 = wq3[:, :, MLA_NOPE:MLA_NOPE + half]
    x2 = wq3[:, :, MLA_NOPE + half:]
    zq = jnp.zeros((Q_LORA, H_B, LANES - MLA_ROPE), F32)
    wqr = jnp.concatenate([jnp.concatenate([x1, x2, zq], axis=2).reshape(Q_LORA, H_B * LANES),
                           jnp.concatenate([x2, x1, zq], axis=2).reshape(Q_LORA, H_B * LANES)], axis=1).astype(BF16)
    wuvt = jnp.transpose(wkv3[:, :, MLA_NOPE:], (1, 2, 0)).astype(BF16)

    return dict(
        ff1=(row(g_ff1_pre), row(g_ff1_post), w_ff1_gu.astype(BF16), w_ff1_down.astype(BF16)),
        ff2=(row(g_ff2_pre), row(g_ff2_post), w_ff2_gu.astype(BF16), w_ff2_down.astype(BF16)),
        inproj=(row(g_mix_pre), win_p, bf_p, row(g_bq), row(g_bkv), wcomb, wqr),
        wuvt=wuvt,
        mixout=(row(g_grp[:W_A]), row(g_grp[W_A:W_A + W_B]), row(g_grp[W_A + W_B:]),
                w_out[:W_A].astype(BF16), w_out[W_A:W_A + W_B].astype(BF16), w_out[W_A + W_B:].astype(BF16),
                row(g_mix_post)),
        ple=(row(g_ple_pre), w_ple_gate.astype(BF16), w_ple_proj.astype(BF16), row(g_ple_post)),
    )


def _pad_keys(a, tk_pad):
    return jnp.pad(a, ((0, 0), (0, tk_pad - a.shape[1])) + ((0, 0),) * (a.ndim - 2))


def _forget_sums(logf_all, tk_pad):
    b, tk, _ = logf_all.shape
    x = jnp.transpose(logf_all, (0, 2, 1))
    x = jnp.pad(x, ((0, 0), (0, F_ROWS - H_C), (0, tk_pad - tk)))
    return _cumsum_rows(x)


def _layer(h, p, seq_len, q_pos0, past, lp, tables):
    n, d = h.shape
    b = n // seq_len
    h = _ffn(h, *lp["ff1"])
    (ka_st, va_st, ckv_st, kr_st, kc_st, vc_st, lf_st,
     qa_b, ka_b, va_b, qm_b, kl_b, qc_b, kc_b, vc_b) = _inproj(h, *lp["inproj"], *tables)
    state = (ka_st.reshape(b, seq_len, H_A, HEAD_DIM), va_st.reshape(b, seq_len, H_A, HEAD_DIM),
             ckv_st.reshape(b, seq_len, KV_LORA), kr_st.reshape(b, seq_len, MLA_ROPE),
             kc_st.reshape(b, seq_len, H_C, HEAD_DIM), vc_st.reshape(b, seq_len, H_C, HEAD_DIM),
             lf_st.reshape(b, seq_len, H_C))
    seq3 = lambda a: a.reshape(b, seq_len, a.shape[-1])
    tq_pad = -(-seq_len // LANES) * LANES
    if past is None:
        n_valid = seq_len
        tk_pad = -(-tq_pad // KEY_BLOCK) * KEY_BLOCK
        ka_all, va_all, kl_all, kc_all, vc_all = (_pad_keys(seq3(a), tk_pad) for a in (ka_b, va_b, kl_b, kc_b, vc_b))
        lf_all = state[6]
    else:
        pa_k, pa_v, pb_ckv, pb_kr, pc_k, pc_v, pc_lf = past
        past_len = pa_k.shape[1]
        n_valid = past_len + seq_len
        tk_pad = -(-(past_len + tq_pad) // KEY_BLOCK) * KEY_BLOCK
        join = lambda c, new: _pad_keys(jnp.concatenate([c.reshape(b, past_len, -1).astype(BF16), seq3(new)], axis=1), tk_pad)
        ka_all, va_all, kc_all, vc_all = join(pa_k, ka_b), join(pa_v, va_b), join(pc_k, kc_b), join(pc_v, vc_b)
        kl_past = jnp.concatenate([pb_ckv, pb_kr, jnp.zeros((b, past_len, LANES - MLA_ROPE), F32)], axis=-1)
        kl_all = join(kl_past, kl_b)
        lf_all = jnp.concatenate([pc_lf, state[6]], axis=1)
    f_rows = _forget_sums(lf_all, tk_pad)
    f_q = f_rows[:, :, q_pos0:q_pos0 + tq_pad]
    f_k = jnp.transpose(f_rows, (0, 2, 1))
    pad_q = lambda a: jnp.pad(a, ((0, 0),) * (a.ndim - 2) + ((0, tq_pad - seq_len), (0, 0)))

    oa = _sb_attention(pad_q(seq3(qa_b)), ka_all, va_all, q_pos0, n_valid)[:, :seq_len]
    ob = _mla_attention(pad_q(qm_b.reshape(H_B, b, seq_len, 2 * LANES)), kl_all, lp["wuvt"], tq_pad, q_pos0,
                        n_valid)[:, :seq_len]
    oc = _fox_attention(pad_q(seq3(qc_b)), kc_all, vc_all, f_q, f_k, q_pos0, n_valid)[:, :seq_len]
    flat = lambda a: a.reshape(n, a.shape[-1])
    h = _mixout(h, flat(oa), flat(ob), flat(oc), *lp["mixout"])
    h = _ffn(h, *lp["ff2"])
    h = _ple(h, p, *lp["ple"])
    return h, state


def _trunk(x, p, q_pos0, caches, layers):
    b, t, d = x.shape
    depth = len(layers)
    tables = tuple(jnp.tile(tab, (b, 1)) for tab in _rope_tables(q_pos0 + jnp.arange(t, dtype=jnp.int32)))
    h = x.reshape(b * t, d)
    states = []
    for i in range(depth):
        past = None if caches is None else [c[i] for c in caches]
        h, st = _layer(h, p[i].reshape(b * t, -1), t, q_pos0, past, layers[i], tables)
        states.append(st)
    stacked = [jnp.stack([st[j] for st in states]) for j in range(len(states[0]))]
    return h.reshape(b, t, d), stacked


def kernel(x_prompt, x_sample, p_prompt, p_sample, cache_a_k, cache_a_v, cache_b_ckv, cache_b_krope, cache_c_k, cache_c_v, cache_c_logf, g_ff1_pre, g_ff1_post, w_ff1_gu, w_ff1_down, g_mix_pre, g_mix_post, w_in, b_f, g_bq, g_bkv, w_uq, w_ukv, g_grp, w_out, g_ff2_pre, g_ff2_post, w_ff2_gu, w_ff2_down, g_ple_pre, w_ple_gate, w_ple_proj, g_ple_post):
    weights = (g_ff1_pre, g_ff1_post, w_ff1_gu, w_ff1_down, g_mix_pre, g_mix_post, w_in, b_f,
               g_bq, g_bkv, w_uq, w_ukv, g_grp, w_out, g_ff2_pre, g_ff2_post, w_ff2_gu, w_ff2_down,
               g_ple_pre, w_ple_gate, w_ple_proj, g_ple_post)
    depth = w_in.shape[0]
    layers = [_prep_layer([w[i] for w in weights]) for i in range(depth)]
    y_prompt, sp = _trunk(x_prompt, p_prompt, 0, None, layers)
    caches = (cache_a_k, cache_a_v, cache_b_ckv, cache_b_krope, cache_c_k, cache_c_v, cache_c_logf)
    y_sample, ss = _trunk(x_sample, p_sample, cache_a_k.shape[2], caches, layers)
    return (y_prompt, y_sample, *sp, *ss)
```

```python
import functools
import math

import jax
import jax.numpy as jnp
from jax import lax
from jax.experimental import pallas as pl
from jax.experimental.pallas import tpu as pltpu

CHUNK = 64
HEAD_DIM = 64
H_A = 4
H_B = 8
H_C = 4
W_A = H_A * HEAD_DIM
MLA_NOPE = 64
MLA_ROPE = 32
MLA_V = 64
W_B = H_B * MLA_V
W_C = H_C * HEAD_DIM
Q_LORA = 256
KV_LORA = 128
ROPE_THETA = 10000.0
EPS = 1e-6
FFN_RES = 0.5
SB_SCALE = HEAD_DIM ** -0.5
MLA_SCALE = (MLA_NOPE + MLA_ROPE) ** -0.5
FOX_SCALE = HEAD_DIM ** -0.5
LOG2E = math.log2(math.e)

LANES = 128
KEY_BLOCK = 256
F_ROWS = 16
VMEM_LIMIT = 56 * 1024 * 1024

COL_A = 0
COL_CQ = COL_A + 3 * W_A
COL_CKV = COL_CQ + Q_LORA
COL_KRA = COL_CKV + KV_LORA
COL_KRB = COL_KRA + LANES
COL_C = COL_KRB + LANES
COL_F = COL_C + 3 * W_C
IN_COLS_P = COL_F + LANES

BF16 = jnp.bfloat16
F32 = jnp.float32


def _dot(a, b):
    return jnp.dot(a, b, preferred_element_type=F32)


def _dot_nt(a, b):
    return lax.dot_general(a, b, (((1,), (1,)), ((), ())), preferred_element_type=F32)


def _rms(x, g):
    ms = jnp.mean(x * x, axis=-1, keepdims=True)
    return x * lax.rsqrt(ms + EPS) * g


def _log_sigmoid(x):
    return jnp.minimum(x, 0.0) - jnp.log(1.0 + jnp.exp(-jnp.abs(x)))


def _softplus(x):
    return jnp.maximum(x, 0.0) + jnp.log(1.0 + jnp.exp2(jnp.abs(x) * (-LOG2E)))


def _split2(x):
    hi = x.astype(BF16)
    lo = (x - hi.astype(F32)).astype(BF16)
    return hi, lo


def _layer_spec(a, layer):
    idx = (layer,) + (0,) * (a.ndim - 1)
    return pl.BlockSpec((None,) + a.shape[1:], lambda *_: idx, pipeline_mode=pl.Buffered(1))


def _params(n_axes):
    return pltpu.CompilerParams(dimension_semantics=("arbitrary",) * n_axes, vmem_limit_bytes=VMEM_LIMIT)


def _row_tile(n):
    for tm in (512, 256):
        if n % tm == 0:
            return tm
    return n


def _ffn_rows(h, gpre_ref, gpost_ref, wgu_ref, wdown_ref, d_ff, chunks):
    xn = _rms(h, gpre_ref[...]).astype(BF16)
    acc = jnp.zeros(h.shape, F32)
    for c0, c1 in chunks:
        g = _dot(xn, wgu_ref[:, c0:c1])
        u = _dot(xn, wgu_ref[:, d_ff + c0:d_ff + c1])
        a = (g * jax.nn.sigmoid(g) * u).astype(BF16)
        acc = acc + _dot(a, wdown_ref[c0:c1, :])
    return h + FFN_RES * _rms(acc, gpost_ref[...])


def _ffn_chunks(d_ff):
    step = 4 * KEY_BLOCK
    return tuple((c, min(c + step, d_ff)) for c in range(0, d_ff, step))


def _ffn_kernel(h_ref, gpre_ref, gpost_ref, wgu_ref, wdown_ref, o_ref, *, d_ff, chunks):
    o_ref[...] = _ffn_rows(h_ref[...], gpre_ref, gpost_ref, wgu_ref, wdown_ref, d_ff, chunks)


def _ffn(h, layer, gpre, gpost, wgu, wdown):
    n, d = h.shape
    d_ff = wdown.shape[1]
    consts = [gpre, gpost, wgu, wdown]
    tm = _row_tile(n)
    chunks = _ffn_chunks(d_ff)
    row = pl.BlockSpec((tm, d), lambda i: (i, 0))
    return pl.pallas_call(
        functools.partial(_ffn_kernel, d_ff=d_ff, chunks=chunks),
        grid=(n // tm,),
        in_specs=[row] + [_layer_spec(c, layer) for c in consts],
        out_specs=row,
        out_shape=jax.ShapeDtypeStruct((n, d), F32),
        compiler_params=_params(1),
        name="ffn",
    )(h, gpre, gpost, wgu, wdown)


def _wcomb_kernel(wq_ref, wk_ref, o_ref):
    o_ref[...] = _dot_nt(wq_ref[...], wk_ref[...]).astype(BF16)


def _wcomb(wq_n, wk_n):
    depth, nh, ql, dn = wq_n.shape
    kl = wk_n.shape[2]
    return pl.pallas_call(
        _wcomb_kernel,
        grid=(depth, nh),
        in_specs=[pl.BlockSpec((None, None, ql, dn), lambda i, h: (i, h, 0, 0)),
                  pl.BlockSpec((None, None, kl, dn), lambda i, h: (i, h, 0, 0))],
        out_specs=pl.BlockSpec((None, ql, kl), lambda i, h: (i, 0, h)),
        out_shape=jax.ShapeDtypeStruct((depth, ql, nh * kl), BF16),
        compiler_params=_params(2),
        name="wcomb",
    )(wq_n, wk_n)


def _inproj_kernel(h_ref, gpre_ref, win_ref, bf_ref, gbq_ref, gbkv_ref, wcomb_ref, wqr_ref, cos_ref, sin_ref,
                   ka_st, va_st, ckv_st, kr_st, kc_st, vc_st, lf_st,
                   qa_b, ka_b, va_b, qm_b, kl_b, qc_b, kc_b, vc_b):
    xn = _rms(h_ref[...], gpre_ref[...]).astype(BF16)
    proj = _dot(xn, win_ref[...])
    cos = cos_ref[...]
    sin = sin_ref[...]

    qa_b[...] = (proj[:, COL_A:COL_A + W_A] * SB_SCALE).astype(BF16)
    ka = proj[:, COL_A + W_A:COL_A + 2 * W_A]
    va = proj[:, COL_A + 2 * W_A:COL_A + 3 * W_A]
    ka_st[...] = ka
    va_st[...] = va
    ka_b[...] = ka.astype(BF16)
    va_b[...] = va.astype(BF16)

    qc_b[...] = (proj[:, COL_C:COL_C + W_C] * (FOX_SCALE * LOG2E)).astype(BF16)
    kc = proj[:, COL_C + W_C:COL_C + 2 * W_C]
    vc = proj[:, COL_C + 2 * W_C:COL_C + 3 * W_C]
    kc_st[...] = kc
    vc_st[...] = vc
    kc_b[...] = kc.astype(BF16)
    vc_b[...] = vc.astype(BF16)

    lf = _log_sigmoid(proj[:, COL_F:COL_F + LANES] + bf_ref[...])
    lf_st[...] = lf[:, :H_C]

    ckv = _rms(proj[:, COL_CKV:COL_CKV + KV_LORA], gbkv_ref[...])
    ckv_st[...] = ckv
    kr = proj[:, COL_KRA:COL_KRA + LANES] * cos + proj[:, COL_KRB:COL_KRB + LANES] * sin
    kr_st[...] = kr[:, :MLA_ROPE]
    kl_b[:, :KV_LORA] = ckv.astype(BF16)
    kl_b[:, KV_LORA:] = kr.astype(BF16)

    cqn = _rms(proj[:, COL_CQ:COL_CQ + Q_LORA], gbq_ref[...]).astype(BF16)
    qlat = _dot(cqn, wcomb_ref[...])
    qr = _dot(cqn, wqr_ref[...])
    half = H_B * LANES
    for hh in range(H_B):
        sl = slice(hh * LANES, (hh + 1) * LANES)
        rope = qr[:, sl] * cos + qr[:, half + hh * LANES:half + (hh + 1) * LANES] * sin
        qm_b[hh, :, :KV_LORA] = (qlat[:, sl] * (MLA_SCALE * LOG2E)).astype(BF16)
        qm_b[hh, :, KV_LORA:] = (rope * (MLA_SCALE * LOG2E)).astype(BF16)


def _inproj(h, seq_len, layer, gpre, win, bf, gbq, gbkv, wcomb, wqr, cos_t, sin_t):
    n, d = h.shape
    tm = _row_tile(n)
    row = lambda w: pl.BlockSpec((tm, w), lambda i: (i, 0))
    consts = [gpre, win, bf, gbq, gbkv, wcomb, wqr]
    if seq_len % tm == 0:
        per_seq = seq_len // tm
        tab = pl.BlockSpec((tm, LANES), lambda i: (i % per_seq, 0))
    else:
        cos_t, sin_t = (jnp.tile(t, (n // seq_len, 1)) for t in (cos_t, sin_t))
        tab = row(LANES)
    outs = [
        (row(W_A), (n, W_A), F32), (row(W_A), (n, W_A), F32), (row(KV_LORA), (n, KV_LORA), F32),
        (row(MLA_ROPE), (n, MLA_ROPE), F32), (row(W_C), (n, W_C), F32), (row(W_C), (n, W_C), F32),
        (row(H_C), (n, H_C), F32),
        (row(W_A), (n, W_A), BF16), (row(W_A), (n, W_A), BF16), (row(W_A), (n, W_A), BF16),
        (pl.BlockSpec((H_B, tm, 2 * LANES), lambda i: (0, i, 0)), (H_B, n, 2 * LANES), BF16),
        (row(2 * LANES), (n, 2 * LANES), BF16),
        (row(W_C), (n, W_C), BF16), (row(W_C), (n, W_C), BF16), (row(W_C), (n, W_C), BF16),
    ]
    return pl.pallas_call(
        _inproj_kernel,
        grid=(n // tm,),
        in_specs=[row(d)] + [_layer_spec(c, layer) for c in consts] + [tab, tab],
        out_specs=[o[0] for o in outs],
        out_shape=[jax.ShapeDtypeStruct(o[1], o[2]) for o in outs],
        compiler_params=_params(1),
        name="inproj",
    )(h, gpre, win, bf, gbq, gbkv, wcomb, wqr, cos_t, sin_t)


def _cumsum_kernel(x_ref, o_ref, *, n_blocks):
    r = lax.broadcasted_iota(jnp.int32, (KEY_BLOCK, KEY_BLOCK), 0)
    c = lax.broadcasted_iota(jnp.int32, (KEY_BLOCK, KEY_BLOCK), 1)
    upper = jnp.where(r <= c, 1.0, 0.0).astype(BF16)
    carry = jnp.zeros((x_ref.shape[0], 1), F32)
    for j in range(n_blocks):
        sl = slice(j * KEY_BLOCK, (j + 1) * KEY_BLOCK)
        x = x_ref[:, sl]
        hi = x.astype(BF16)
        mid, lo = _split2(x - hi.astype(F32))
        f = _dot(hi, upper) + _dot(mid, upper) + _dot(lo, upper) + carry
        o_ref[:, sl] = f * LOG2E
        carry = f[:, KEY_BLOCK - 1:KEY_BLOCK]


def _cumsum_rows(x):
    b, r, tp = x.shape
    spec = pl.BlockSpec((None, r, tp), lambda i: (i, 0, 0))
    return pl.pallas_call(
        functools.partial(_cumsum_kernel, n_blocks=tp // KEY_BLOCK),
        grid=(b,),
        in_specs=[spec],
        out_specs=spec,
        out_shape=jax.ShapeDtypeStruct(x.shape, F32),
        compiler_params=_params(1),
        name="cumsum_logf",
    )(x)


def _block_range(q_start, tq, n_valid):
    n_full = q_start // KEY_BLOCK
    last = jnp.minimum(((q_start + tq + CHUNK - 1) // CHUNK) * CHUNK, n_valid)
    n_total = (last + KEY_BLOCK - 1) // KEY_BLOCK
    return n_full, n_total


def _positions(q_start, tq, j, cols):
    k_pos = j * KEY_BLOCK + lax.broadcasted_iota(jnp.int32, (KEY_BLOCK, cols), 0)
    lane = lax.broadcasted_iota(jnp.int32, (KEY_BLOCK, cols), 1)
    q_pos = q_start + (lane if cols == tq else lane % tq)
    return q_pos, k_pos


def _transpose_blocks(src_ref, dst_ref, width):
    for j in range(dst_ref.shape[0]):
        blk = src_ref[j * KEY_BLOCK:(j + 1) * KEY_BLOCK, :width].astype(F32)
        dst_ref[j] = blk.T.astype(BF16)


def _finite_or_zero(m):
    return jnp.where(m == -jnp.inf, 0.0, m)


def _head_masks(n_heads):
    lane = lax.broadcasted_iota(jnp.int32, (1, n_heads * HEAD_DIM), 1)
    return [(lane >= hh * HEAD_DIM) & (lane < (hh + 1) * HEAD_DIM) for hh in range(n_heads)]


def _sb_kernel(q_ref, k_ref, v_ref, o_ref, qm_ref, vt_ref, acc_ref, c_ref, *, tq, q_pos0, n_valid):
    @pl.when(pl.program_id(1) == 0)
    def _():
        _transpose_blocks(v_ref, vt_ref, v_ref.shape[1])

    q_start = q_pos0 + pl.program_id(1) * tq
    n_full, n_total = _block_range(q_start, tq, n_valid)
    hmask = _head_masks(H_A)
    q = q_ref[...]
    for hh in range(H_A):
        qm_ref[hh] = jnp.where(hmask[hh], q, jnp.zeros_like(q))
    acc_ref[...] = jnp.zeros(acc_ref.shape, F32)
    c_ref[...] = jnp.zeros(c_ref.shape, F32)
    half = KEY_BLOCK // 2
    r = lax.broadcasted_iota(jnp.int32, (half, KEY_BLOCK), 0)
    c = lax.broadcasted_iota(jnp.int32, (half, KEY_BLOCK), 1)
    neg_upper2 = jnp.where(c % half >= r, -1.0, 0.0).astype(BF16)
    last_row = lax.broadcasted_iota(jnp.int32, (8, tq), 0) == 7

    def neg_suffix_sums(x):
        hi, lo = _split2(x)
        return _dot(neg_upper2, jnp.concatenate([hi, lo], axis=0))

    def block(j, masked):
        off = pl.multiple_of(j * KEY_BLOCK, KEY_BLOCK)
        kb = k_ref[pl.ds(off, KEY_BLOCK), :]
        if masked:
            q_pos, k_pos = _positions(q_start, tq, j, tq)
            mask = k_pos < q_pos
        heads = range(H_A)
        rows = [slice(hh * HEAD_DIM, (hh + 1) * HEAD_DIM) for hh in heads]
        vts = [vt_ref[j, rows[hh], :] for hh in heads]
        carry = c_ref[...]
        zs = [_dot_nt(kb, qm_ref[hh]) for hh in heads]
        sps = [_softplus(z) for z in zs]
        if masked:
            sps = [jnp.where(mask, sp, 0.0) for sp in sps]
        lates = [jnp.concatenate([sp[half:KEY_BLOCK - 8],
                                  jnp.where(last_row, sp[KEY_BLOCK - 8:] - carry[hh:hh + 1, :], sp[KEY_BLOCK - 8:])], axis=0)
                 for hh, sp in enumerate(sps)]
        cum_late = [neg_suffix_sums(x) for x in lates]
        cum_early = [neg_suffix_sums(sp[:half]) for sp in sps]
        cum_early = [cum_early[hh] + cum_late[hh][0:1, :] for hh in heads]
        ws = [jnp.exp(zs[hh] + jnp.concatenate([cum_early[hh], cum_late[hh]], axis=0)) for hh in heads]
        if masked:
            ws = [jnp.where(mask, w, 0.0) for w in ws]
        pvs = [_dot(vts[hh], ws[hh].astype(BF16)) for hh in heads]
        for hh in heads:
            acc_ref[rows[hh], :] += pvs[hh]
            c_ref[hh:hh + 1, :] = cum_early[hh][0:1, :]

    def masked_body(i, carry):
        block(n_total - 1 - i, True)
        return carry

    def full_body(i, carry):
        block(n_full - 1 - i, False)
        return carry

    lax.fori_loop(0, n_total - n_full, masked_body, 0)
    lax.fori_loop(0, n_full, full_body, 0)
    o_ref[...] = acc_ref[...].T


def _sb_attention(q, k, v, q_pos0, n_valid):
    b, tq_all, w = q.shape
    tk = k.shape[1]
    tq = min(tq_all, KEY_BLOCK)
    qspec = pl.BlockSpec((None, tq, w), lambda bi, qi: (bi, qi, 0))
    kspec = pl.BlockSpec((None, tk, w), lambda bi, qi: (bi, 0, 0))
    return pl.pallas_call(
        functools.partial(_sb_kernel, tq=tq, q_pos0=q_pos0, n_valid=n_valid),
        grid=(b, tq_all // tq),
        in_specs=[qspec, kspec, kspec],
        out_specs=qspec,
        out_shape=jax.ShapeDtypeStruct((b, tq_all, w), F32),
        scratch_shapes=[pltpu.VMEM((H_A, tq, w), BF16), pltpu.VMEM((tk // KEY_BLOCK, w, KEY_BLOCK), BF16),
                        pltpu.VMEM((w, tq), F32), pltpu.VMEM((8, tq), F32)],
        compiler_params=_params(2),
        name="sb_attention",
    )(q, k, v)


def _fox_kernel(q_ref, k_ref, v_ref, fq_ref, fk_ref, o_ref, qm_ref, vt_ref, acc_ref, m_ref, l_ref, *, tq, q_pos0, n_valid):
    @pl.when(pl.program_id(1) == 0)
    def _():
        _transpose_blocks(v_ref, vt_ref, v_ref.shape[1])

    q_start = q_pos0 + pl.program_id(1) * tq
    n_full, n_total = _block_range(q_start, tq, n_valid)
    hmask = _head_masks(H_C)
    q = q_ref[...]
    for hh in range(H_C):
        qm_ref[hh] = jnp.where(hmask[hh], q, jnp.zeros_like(q))
    acc_ref[...] = jnp.zeros(acc_ref.shape, F32)
    m_ref[...] = jnp.full(m_ref.shape, -jnp.inf, F32)
    l_ref[...] = jnp.zeros(l_ref.shape, F32)

    def block(j, masked):
        off = pl.multiple_of(j * KEY_BLOCK, KEY_BLOCK)
        kb = k_ref[pl.ds(off, KEY_BLOCK), :]
        fk = fk_ref[pl.ds(off, KEY_BLOCK), :]
        if masked:
            q_pos, k_pos = _positions(q_start, tq, j, tq)
            mask = k_pos <= q_pos
        heads = range(H_C)
        rows = [slice(hh * HEAD_DIM, (hh + 1) * HEAD_DIM) for hh in heads]
        vts = [vt_ref[j, rows[hh], :] for hh in heads]
        fq, m_all, l_all = fq_ref[...], m_ref[...], l_ref[...]
        accs = [acc_ref[rows[hh], :] for hh in heads]
        zs = []
        for hh in heads:
            z = _dot_nt(kb, qm_ref[hh]) + (fq[hh:hh + 1, :] - fk[:, hh:hh + 1])
            zs.append(jnp.where(mask, z, -jnp.inf) if masked else z)
        m_new = [jnp.maximum(m_all[hh:hh + 1, :], jnp.max(zs[hh], axis=0, keepdims=True)) for hh in heads]
        m_use = [_finite_or_zero(m) for m in m_new]
        ps = [jnp.exp2(zs[hh] - m_use[hh]) for hh in heads]
        alpha = [jnp.exp2(m_all[hh:hh + 1, :] - m_use[hh]) for hh in heads]
        l_new = [alpha[hh] * l_all[hh:hh + 1, :] + jnp.sum(ps[hh], axis=0, keepdims=True) for hh in heads]
        pvs = [_dot(vts[hh], ps[hh].astype(BF16)) for hh in heads]
        for hh in heads:
            m_ref[hh:hh + 1, :] = m_new[hh]
            l_ref[hh:hh + 1, :] = l_new[hh]
            acc_ref[rows[hh], :] = accs[hh] * alpha[hh] + pvs[hh]

    def masked_body(i, carry):
        block(n_total - 1 - i, True)
        return carry

    def full_body(i, carry):
        block(n_full - 1 - i, False)
        return carry

    lax.fori_loop(0, n_total - n_full, masked_body, 0)
    lax.fori_loop(0, n_full, full_body, 0)
    for hh in range(H_C):
        rows = slice(hh * HEAD_DIM, (hh + 1) * HEAD_DIM)
        acc_ref[rows, :] = acc_ref[rows, :] * (1.0 / l_ref[hh:hh + 1, :])
    o_ref[...] = acc_ref[...].T


def _fox_attention(q, k, v, fq, fk, q_pos0, n_valid):
    b, tq_all, w = q.shape
    tk = k.shape[1]
    tq = min(tq_all, KEY_BLOCK)
    qspec = pl.BlockSpec((None, tq, w), lambda bi, qi: (bi, qi, 0))
    kspec = pl.BlockSpec((None, tk, w), lambda bi, qi: (bi, 0, 0))
    fqspec = pl.BlockSpec((None, F_ROWS, tq), lambda bi, qi: (bi, 0, qi))
    fkspec = pl.BlockSpec((None, tk, F_ROWS), lambda bi, qi: (bi, 0, 0))
    return pl.pallas_call(
        functools.partial(_fox_kernel, tq=tq, q_pos0=q_pos0, n_valid=n_valid),
        grid=(b, tq_all // tq),
        in_specs=[qspec, kspec, kspec, fqspec, fkspec],
        out_specs=qspec,
        out_shape=jax.ShapeDtypeStruct((b, tq_all, w), F32),
        scratch_shapes=[pltpu.VMEM((H_C, tq, w), BF16), pltpu.VMEM((tk // KEY_BLOCK, w, KEY_BLOCK), BF16),
                        pltpu.VMEM((w, tq), F32), pltpu.VMEM((8, tq), F32), pltpu.VMEM((8, tq), F32)],
        compiler_params=_params(2),
        name="fox_attention",
    )(q, k, v, fq, fk)


def _mla_kernel(q_ref, kl_ref, wuvt_ref, o_ref, ct_ref, acc_ref, m_ref, l_ref, *, tq, q_pos0, n_valid):
    @pl.when(pl.program_id(1) == 0)
    def _():
        _transpose_blocks(kl_ref, ct_ref, KV_LORA)

    q_start = q_pos0 + pl.program_id(1) * tq
    n_full, n_total = _block_range(q_start, tq, n_valid)
    cols = H_B * tq
    qs = q_ref[...].reshape(cols, 2 * LANES)
    acc_ref[...] = jnp.zeros(acc_ref.shape, F32)
    m_ref[...] = jnp.full(m_ref.shape, -jnp.inf, F32)
    l_ref[...] = jnp.zeros(l_ref.shape, F32)

    group = 2 * tq
    groups = [slice(g * group, (g + 1) * group) for g in range(cols // group)]

    def block(j, masked):
        off = pl.multiple_of(j * KEY_BLOCK, KEY_BLOCK)
        kb = kl_ref[pl.ds(off, KEY_BLOCK), :]
        ct = ct_ref[j]
        if masked:
            q_pos, k_pos = _positions(q_start, tq, j, group)
            mask = (k_pos // CHUNK <= q_pos // CHUNK) & (k_pos < n_valid)
        m_all, l_all = m_ref[...], l_ref[...]
        accs = [acc_ref[:, g] for g in groups]
        zs = [_dot_nt(kb, qs[g]) for g in groups]
        if masked:
            zs = [jnp.where(mask, z, -jnp.inf) for z in zs]
        gs = range(len(groups))
        m_new = [jnp.maximum(m_all[:, groups[gi]], jnp.max(zs[gi], axis=0, keepdims=True)) for gi in gs]
        m_use = [_finite_or_zero(m) for m in m_new]
        ps = [jnp.exp2(zs[gi] - m_use[gi]) for gi in gs]
        alpha = [jnp.exp2(m_all[:, groups[gi]] - m_use[gi]) for gi in gs]
        l_new = [alpha[gi] * l_all[:, groups[gi]] + jnp.sum(ps[gi], axis=0, keepdims=True) for gi in gs]
        pvs = [_dot(ct, ps[gi].astype(BF16)) for gi in gs]
        for gi, g in enumerate(groups):
            m_ref[:, g] = m_new[gi]
            l_ref[:, g] = l_new[gi]
            acc_ref[:, g] = accs[gi] * alpha[gi] + pvs[gi]

    def masked_body(i, carry):
        block(n_total - 1 - i, True)
        return carry

    def full_body(i, carry):
        block(n_full - 1 - i, False)
        return carry

    lax.fori_loop(0, n_total - n_full, masked_body, 0)
    lax.fori_loop(0, n_full, full_body, 0)
    lat = (acc_ref[...] * (1.0 / l_ref[...])).astype(BF16)
    heads = [_dot(wuvt_ref[hh], lat[:, hh * tq:(hh + 1) * tq]) for hh in range(H_B)]
    o_ref[...] = jnp.concatenate(heads, axis=0).T


def _mla_attention(qm, kl, wuvt, layer, tq_all, q_pos0, n_valid):
    b, tk, w = kl.shape
    tq = min(tq_all, LANES)
    cols = H_B * tq
    return pl.pallas_call(
        functools.partial(_mla_kernel, tq=tq, q_pos0=q_pos0, n_valid=n_valid),
        grid=(b, tq_all // tq),
        in_specs=[pl.BlockSpec((H_B, None, tq, w), lambda bi, qi: (0, bi, qi, 0)),
                  pl.BlockSpec((None, tk, w), lambda bi, qi: (bi, 0, 0)),
                  _layer_spec(wuvt, layer)],
        out_specs=pl.BlockSpec((None, tq, W_B), lambda bi, qi: (bi, qi, 0)),
        out_shape=jax.ShapeDtypeStruct((b, tq_all, W_B), F32),
        scratch_shapes=[pltpu.VMEM((tk // KEY_BLOCK, KV_LORA, KEY_BLOCK), BF16), pltpu.VMEM((KV_LORA, cols), F32),
                        pltpu.VMEM((1, cols), F32), pltpu.VMEM((1, cols), F32)],
        compiler_params=_params(2),
        name="mla_attention",
    )(qm, kl, wuvt)


def _post_kernel(h_ref, oa_ref, ob_ref, oc_ref, p_ref, ggrp_ref, wout_ref, gmix_ref,
                 gpre_ref, gpost_ref, wgu_ref, wdown_ref, gple_pre_ref, wgate_ref, wproj_ref, gple_post_ref, o_ref,
                 *, d_ff, chunks):
    m = jnp.zeros(h_ref.shape, F32)
    c0 = 0
    for o_grp in (oa_ref, ob_ref, oc_ref):
        c1 = c0 + o_grp.shape[1]
        m = m + _dot(_rms(o_grp[...], ggrp_ref[:, c0:c1]).astype(BF16), wout_ref[c0:c1, :])
        c0 = c1
    h = h_ref[...] + _rms(m, gmix_ref[...])
    h = _ffn_rows(h, gpre_ref, gpost_ref, wgu_ref, wdown_ref, d_ff, chunks)
    gate = jax.nn.sigmoid(_dot(_rms(h, gple_pre_ref[...]).astype(BF16), wgate_ref[...]))
    e = _dot(p_ref[...].astype(BF16), wproj_ref[...]) * gate
    o_ref[...] = h + _rms(e, gple_post_ref[...])


def _post(h, oa, ob, oc, p, layer, consts):
    n, d = h.shape
    tm = _row_tile(n)
    row = lambda w: pl.BlockSpec((tm, w), lambda i: (i, 0))
    d_ff = consts[6].shape[1]
    return pl.pallas_call(
        functools.partial(_post_kernel, d_ff=d_ff, chunks=_ffn_chunks(d_ff)),
        grid=(n // tm,),
        in_specs=[row(d), row(W_A), row(W_B), row(W_C), row(p.shape[1])] + [_layer_spec(c, layer) for c in consts],
        out_specs=row(d),
        out_shape=jax.ShapeDtypeStruct((n, d), F32),
        compiler_params=_params(1),
        name="post",
    )(h, oa, ob, oc, p, *consts)


def _rope_tables(pos):
    half = MLA_ROPE // 2
    inv = ROPE_THETA ** (-jnp.arange(half, dtype=F32) / half)
    ang = pos.astype(F32)[:, None] * inv[None, :]
    cos, sin = jnp.cos(ang), jnp.sin(ang)
    pad = jnp.zeros((pos.shape[0], LANES - MLA_ROPE), F32)
    return jnp.concatenate([cos, cos, pad], axis=1), jnp.concatenate([-sin, sin, pad], axis=1)


def _prep_weights(weights):
    (g_ff1_pre, g_ff1_post, w_ff1_gu, w_ff1_down, g_mix_pre, g_mix_post, w_in, b_f, g_bq, g_bkv, w_uq, w_ukv,
     g_grp, w_out, g_ff2_pre, g_ff2_post, w_ff2_gu, w_ff2_down, g_ple_pre, w_ple_gate, w_ple_proj, g_ple_post) = weights
    depth, d = w_in.shape[:2]
    half = MLA_ROPE // 2
    row = lambda g: g.reshape(depth, 1, -1).astype(F32)
    bf = lambda w: w.astype(BF16)
    c_kr = 3 * W_A + Q_LORA + KV_LORA
    c_c = c_kr + MLA_ROPE
    c_f = c_c + 3 * W_C
    w_in = bf(w_in)
    kr = w_in[:, :, c_kr:c_kr + MLA_ROPE]
    kr_sw = jnp.concatenate([kr[:, :, half:], kr[:, :, :half]], axis=2)
    zpad = jnp.zeros((depth, d, LANES - MLA_ROPE), BF16)
    win_p = jnp.concatenate([w_in[:, :, :c_kr], kr, zpad, kr_sw, zpad, w_in[:, :, c_c:c_f], w_in[:, :, c_f:],
                             jnp.zeros((depth, d, LANES - H_C), BF16)], axis=2)
    assert win_p.shape[2] == IN_COLS_P
    bf_p = jnp.pad(b_f.astype(F32), ((0, 0), (0, LANES - H_C))).reshape(depth, 1, LANES)

    wq4 = bf(w_uq).reshape(depth, Q_LORA, H_B, MLA_NOPE + MLA_ROPE)
    wkv4 = bf(w_ukv).reshape(depth, KV_LORA, H_B, MLA_NOPE + MLA_V)
    wq_n = jnp.transpose(wq4[..., :MLA_NOPE], (0, 2, 1, 3))
    wk_n = jnp.transpose(wkv4[..., :MLA_NOPE], (0, 2, 1, 3))
    wcomb = _wcomb(wq_n, wk_n)
    x1 = wq4[..., MLA_NOPE:MLA_NOPE + half]
    x2 = wq4[..., MLA_NOPE + half:]
    zq = jnp.zeros((depth, Q_LORA, H_B, LANES - MLA_ROPE), BF16)
    wqr = jnp.concatenate([jnp.concatenate([x1, x2, zq], axis=3).reshape(depth, Q_LORA, H_B * LANES),
                           jnp.concatenate([x2, x1, zq], axis=3).reshape(depth, Q_LORA, H_B * LANES)], axis=2)
    wuvt = jnp.transpose(wkv4[..., MLA_NOPE:], (0, 2, 3, 1))

    return dict(
        ff1=(row(g_ff1_pre), row(g_ff1_post), bf(w_ff1_gu), bf(w_ff1_down)),
        inproj=(row(g_mix_pre), win_p, bf_p, row(g_bq), row(g_bkv), wcomb, wqr),
        wuvt=wuvt,
        post=(row(g_grp), bf(w_out), row(g_mix_post),
              row(g_ff2_pre), row(g_ff2_post), bf(w_ff2_gu), bf(w_ff2_down),
              row(g_ple_pre), bf(w_ple_gate), bf(w_ple_proj), row(g_ple_post)),
    )


def _pad_keys(a, tk_pad):
    return jnp.pad(a, ((0, 0), (0, tk_pad - a.shape[1])) + ((0, 0),) * (a.ndim - 2))


def _forget_sums(logf_all, tk_pad):
    b, tk, _ = logf_all.shape
    x = jnp.transpose(logf_all, (0, 2, 1))
    x = jnp.pad(x, ((0, 0), (0, F_ROWS - H_C), (0, tk_pad - tk)))
    return _cumsum_rows(x)


def _layer(h, p, seq_len, q_pos0, past, layer, lp, tables):
    n, d = h.shape
    b = n // seq_len
    h = _ffn(h, layer, *lp["ff1"])
    (ka_st, va_st, ckv_st, kr_st, kc_st, vc_st, lf_st,
     qa_b, ka_b, va_b, qm_b, kl_b, qc_b, kc_b, vc_b) = _inproj(h, seq_len, layer, *lp["inproj"], *tables)
    state = (ka_st.reshape(b, seq_len, H_A, HEAD_DIM), va_st.reshape(b, seq_len, H_A, HEAD_DIM),
             ckv_st.reshape(b, seq_len, KV_LORA), kr_st.reshape(b, seq_len, MLA_ROPE),
             kc_st.reshape(b, seq_len, H_C, HEAD_DIM), vc_st.reshape(b, seq_len, H_C, HEAD_DIM),
             lf_st.reshape(b, seq_len, H_C))
    seq3 = lambda a: a.reshape(b, seq_len, a.shape[-1])
    tq_pad = -(-seq_len // LANES) * LANES
    if past is None:
        n_valid = seq_len
        tk_pad = -(-tq_pad // KEY_BLOCK) * KEY_BLOCK
        ka_all, va_all, kl_all, kc_all, vc_all = (_pad_keys(seq3(a), tk_pad) for a in (ka_b, va_b, kl_b, kc_b, vc_b))
        lf_all = state[6]
    else:
        pa_k, pa_v, pb_ckv, pb_kr, pc_k, pc_v, pc_lf = past
        past_len = pa_k.shape[1]
        n_valid = past_len + seq_len
        tk_pad = -(-(past_len + tq_pad) // KEY_BLOCK) * KEY_BLOCK
        join = lambda c, new: _pad_keys(jnp.concatenate([c.reshape(b, past_len, -1).astype(BF16), seq3(new)], axis=1), tk_pad)
        ka_all, va_all, kc_all, vc_all = join(pa_k, ka_b), join(pa_v, va_b), join(pc_k, kc_b), join(pc_v, vc_b)
        kl_past = jnp.concatenate([pb_ckv, pb_kr, jnp.zeros((b, past_len, LANES - MLA_ROPE), F32)], axis=-1)
        kl_all = join(kl_past, kl_b)
        lf_all = jnp.concatenate([pc_lf, state[6]], axis=1)
    f_rows = _forget_sums(lf_all, tk_pad)
    f_q = f_rows[:, :, q_pos0:q_pos0 + tq_pad]
    f_k = jnp.transpose(f_rows, (0, 2, 1))
    pad_q = lambda a: jnp.pad(a, ((0, 0),) * (a.ndim - 2) + ((0, tq_pad - seq_len), (0, 0)))

    oa = _sb_attention(pad_q(seq3(qa_b)), ka_all, va_all, q_pos0, n_valid)[:, :seq_len]
    ob = _mla_attention(pad_q(qm_b.reshape(H_B, b, seq_len, 2 * LANES)), kl_all, lp["wuvt"], layer, tq_pad, q_pos0,
                        n_valid)[:, :seq_len]
    oc = _fox_attention(pad_q(seq3(qc_b)), kc_all, vc_all, f_q, f_k, q_pos0, n_valid)[:, :seq_len]
    flat = lambda a: a.reshape(n, a.shape[-1])
    h = _post(h, flat(oa), flat(ob), flat(oc), p, layer, lp["post"])
    return h, state


def _trunk(x, p, q_pos0, caches, lp, depth):
    b, t, d = x.shape
    tables = _rope_tables(q_pos0 + jnp.arange(t, dtype=jnp.int32))
    h = x.reshape(b * t, d)
    states = []
    for i in range(depth):
        past = None if caches is None else [c[i] for c in caches]
        h, st = _layer(h, p[i].reshape(b * t, -1), t, q_pos0, past, i, lp, tables)
        states.append(st)
    stacked = [jnp.stack([st[j] for st in states]) for j in range(len(states[0]))]
    return h.reshape(b, t, d), stacked


def kernel(x_prompt, x_sample, p_prompt, p_sample, cache_a_k, cache_a_v, cache_b_ckv, cache_b_krope, cache_c_k, cache_c_v, cache_c_logf, g_ff1_pre, g_ff1_post, w_ff1_gu, w_ff1_down, g_mix_pre, g_mix_post, w_in, b_f, g_bq, g_bkv, w_uq, w_ukv, g_grp, w_out, g_ff2_pre, g_ff2_post, w_ff2_gu, w_ff2_down, g_ple_pre, w_ple_gate, w_ple_proj, g_ple_post):
    weights = (g_ff1_pre, g_ff1_post, w_ff1_gu, w_ff1_down, g_mix_pre, g_mix_post, w_in, b_f,
               g_bq, g_bkv, w_uq, w_ukv, g_grp, w_out, g_ff2_pre, g_ff2_post, w_ff2_gu, w_ff2_down,
               g_ple_pre, w_ple_gate, w_ple_proj, g_ple_post)
    depth = w_in.shape[0]
    lp = _prep_weights(weights)
    y_prompt, sp = _trunk(x_prompt, p_prompt, 0, None, lp, depth)
    caches = (cache_a_k, cache_a_v, cache_b_ckv, cache_b_krope, cache_c_k, cache_c_v, cache_c_logf)
    y_sample, ss = _trunk(x_sample, p_sample, cache_a_k.shape[2], caches, lp, depth)
    return (y_prompt, y_sample, *sp, *ss)
```

```python
import functools
import math

import jax
import jax.numpy as jnp
from jax import lax
from jax.experimental import pallas as pl
from jax.experimental.pallas import tpu as pltpu

CHUNK = 64
HEAD_DIM = 64
H_A = 4
H_B = 8
H_C = 4
W_A = H_A * HEAD_DIM
MLA_NOPE = 64
MLA_ROPE = 32
MLA_V = 64
W_B = H_B * MLA_V
W_C = H_C * HEAD_DIM
Q_LORA = 256
KV_LORA = 128
ROPE_THETA = 10000.0
EPS = 1e-6
FFN_RES = 0.5
SB_SCALE = HEAD_DIM ** -0.5
MLA_SCALE = (MLA_NOPE + MLA_ROPE) ** -0.5
FOX_SCALE = HEAD_DIM ** -0.5
LOG2E = math.log2(math.e)

LANES = 128
KEY_BLOCK = 256
F_ROWS = 16
VMEM_LIMIT = 56 * 1024 * 1024

COL_A = 0
COL_CQ = COL_A + 3 * W_A
COL_CKV = COL_CQ + Q_LORA
COL_KRA = COL_CKV + KV_LORA
COL_KRB = COL_KRA + LANES
COL_C = COL_KRB + LANES
COL_F = COL_C + 3 * W_C
IN_COLS_P = COL_F + LANES

BF16 = jnp.bfloat16
F32 = jnp.float32


def _dot(a, b):
    return jnp.dot(a, b, preferred_element_type=F32)


def _dot_nt(a, b):
    return lax.dot_general(a, b, (((1,), (1,)), ((), ())), preferred_element_type=F32)


def _rms(x, g):
    ms = jnp.mean(x * x, axis=-1, keepdims=True)
    return x * lax.rsqrt(ms + EPS) * g


def _log_sigmoid(x):
    return jnp.minimum(x, 0.0) - jnp.log(1.0 + jnp.exp(-jnp.abs(x)))


def _softplus(x):
    return jnp.maximum(x, 0.0) + jnp.log(1.0 + jnp.exp2(jnp.abs(x) * (-LOG2E)))


def _split2(x):
    hi = x.astype(BF16)
    lo = (x - hi.astype(F32)).astype(BF16)
    return hi, lo


def _layer_spec(a, layer):
    idx = (layer,) + (0,) * (a.ndim - 1)
    return pl.BlockSpec((None,) + a.shape[1:], lambda *_: idx, pipeline_mode=pl.Buffered(1))


def _params(n_axes):
    return pltpu.CompilerParams(dimension_semantics=("arbitrary",) * n_axes, vmem_limit_bytes=VMEM_LIMIT)


def _row_tile(n):
    for tm in (512, 256):
        if n % tm == 0:
            return tm
    return n


def _ffn_rows(h, gpre_ref, gpost_ref, wgu_ref, wdown_ref, d_ff, chunks):
    xn = _rms(h, gpre_ref[...]).astype(BF16)
    acc = jnp.zeros(h.shape, F32)
    for c0, c1 in chunks:
        g = _dot(xn, wgu_ref[:, c0:c1])
        u = _dot(xn, wgu_ref[:, d_ff + c0:d_ff + c1])
        a = (g * jax.nn.sigmoid(g) * u).astype(BF16)
        acc = acc + _dot(a, wdown_ref[c0:c1, :])
    return h + FFN_RES * _rms(acc, gpost_ref[...])


def _ffn_chunks(d_ff):
    step = 4 * KEY_BLOCK
    return tuple((c, min(c + step, d_ff)) for c in range(0, d_ff, step))


def _ffn_kernel(h_ref, gpre_ref, gpost_ref, wgu_ref, wdown_ref, o_ref, *, d_ff, chunks):
    o_ref[...] = _ffn_rows(h_ref[...], gpre_ref, gpost_ref, wgu_ref, wdown_ref, d_ff, chunks)


def _ffn(h, layer, gpre, gpost, wgu, wdown):
    n, d = h.shape
    d_ff = wdown.shape[1]
    consts = [gpre, gpost, wgu, wdown]
    tm = _row_tile(n)
    chunks = _ffn_chunks(d_ff)
    row = pl.BlockSpec((tm, d), lambda i: (i, 0))
    return pl.pallas_call(
        functools.partial(_ffn_kernel, d_ff=d_ff, chunks=chunks),
        grid=(n // tm,),
        in_specs=[row] + [_layer_spec(c, layer) for c in consts],
        out_specs=row,
        out_shape=jax.ShapeDtypeStruct((n, d), F32),
        compiler_params=_params(1),
        name="ffn",
    )(h, gpre, gpost, wgu, wdown)


def _wcomb_kernel(wq_ref, wk_ref, o_ref):
    o_ref[...] = _dot_nt(wq_ref[...], wk_ref[...]).astype(BF16)


def _wcomb(wq_n, wk_n):
    depth, nh, ql, dn = wq_n.shape
    kl = wk_n.shape[2]
    return pl.pallas_call(
        _wcomb_kernel,
        grid=(depth, nh),
        in_specs=[pl.BlockSpec((None, None, ql, dn), lambda i, h: (i, h, 0, 0)),
                  pl.BlockSpec((None, None, kl, dn), lambda i, h: (i, h, 0, 0))],
        out_specs=pl.BlockSpec((None, ql, kl), lambda i, h: (i, 0, h)),
        out_shape=jax.ShapeDtypeStruct((depth, ql, nh * kl), BF16),
        compiler_params=_params(2),
        name="wcomb",
    )(wq_n, wk_n)


def _inproj_kernel(h_ref, gpre_ref, win_ref, bf_ref, gbq_ref, gbkv_ref, wcomb_ref, wqr_ref, wkvt_ref, cos_ref, sin_ref,
                   ka_st, va_st, ckv_st, kr_st, kc_st, vc_st, lf_st,
                   qa_b, ka_b, va_b, qm_b, kl_b, qc_b, kc_b, vc_b, *, feature_major):
    xn = _rms(h_ref[...], gpre_ref[...]).astype(BF16)
    proj = _dot(xn, win_ref[...])
    cos = cos_ref[...]
    sin = sin_ref[...]

    qa_b[...] = (proj[:, COL_A:COL_A + W_A] * SB_SCALE).astype(BF16)
    qc_b[...] = (proj[:, COL_C:COL_C + W_C] * (FOX_SCALE * LOG2E)).astype(BF16)
    ka = proj[:, COL_A + W_A:COL_A + 2 * W_A]
    kc = proj[:, COL_C + W_C:COL_C + 2 * W_C]
    ka_b[...] = ka.astype(BF16)
    kc_b[...] = kc.astype(BF16)
    lf = _log_sigmoid(proj[:, COL_F:COL_F + LANES] + bf_ref[...])
    ckv = _rms(proj[:, COL_CKV:COL_CKV + KV_LORA], gbkv_ref[...])
    ckv_st[...] = ckv
    kr = proj[:, COL_KRA:COL_KRA + LANES] * cos + proj[:, COL_KRB:COL_KRB + LANES] * sin
    kl_b[:, :KV_LORA] = ckv.astype(BF16)
    kl_b[:, KV_LORA:] = kr.astype(BF16)

    if feature_major:
        kvt = _dot_nt(wkvt_ref[...], xn)
        for idx, st in enumerate((ka_st, va_st, kc_st, vc_st)):
            st[...] = kvt[idx * W_A:(idx + 1) * W_A]
        for r in range(va_b.shape[0]):
            cols = slice(r * KEY_BLOCK, (r + 1) * KEY_BLOCK)
            va_b[r] = kvt[W_A:2 * W_A, cols].astype(BF16)
            vc_b[r] = kvt[3 * W_A:4 * W_A, cols].astype(BF16)
        kr_st[...] = kr.T[:MLA_ROPE]
        lf_st[...] = lf.T[:H_C]
    else:
        va = proj[:, COL_A + 2 * W_A:COL_A + 3 * W_A]
        vc = proj[:, COL_C + 2 * W_C:COL_C + 3 * W_C]
        ka_st[...] = ka
        va_st[...] = va
        kc_st[...] = kc
        vc_st[...] = vc
        va_b[...] = va.astype(BF16)
        vc_b[...] = vc.astype(BF16)
        kr_st[...] = kr[:, :MLA_ROPE]
        lf_st[...] = lf[:, :H_C]

    cqn = _rms(proj[:, COL_CQ:COL_CQ + Q_LORA], gbq_ref[...]).astype(BF16)
    qlat = _dot(cqn, wcomb_ref[...])
    qr = _dot(cqn, wqr_ref[...])
    half = H_B * LANES
    for hh in range(H_B):
        sl = slice(hh * LANES, (hh + 1) * LANES)
        rope = qr[:, sl] * cos + qr[:, half + hh * LANES:half + (hh + 1) * LANES] * sin
        qm_b[hh, :, :KV_LORA] = (qlat[:, sl] * (MLA_SCALE * LOG2E)).astype(BF16)
        qm_b[hh, :, KV_LORA:] = (rope * (MLA_SCALE * LOG2E)).astype(BF16)


def _inproj(h, seq_len, layer, gpre, win, bf, gbq, gbkv, wcomb, wqr, wkvt, cos_t, sin_t):
    n, d = h.shape
    b = n // seq_len
    tm = _row_tile(n)
    row = lambda w: pl.BlockSpec((tm, w), lambda i: (i, 0))
    consts = [gpre, win, bf, gbq, gbkv, wcomb, wqr, wkvt]
    feature_major = seq_len % tm == 0 and tm % KEY_BLOCK == 0
    if feature_major:
        per_seq = seq_len // tm
        tab = pl.BlockSpec((tm, LANES), lambda i: (i % per_seq, 0))
        narrow = lambda w: (pl.BlockSpec((None, w, tm), lambda i: (i // per_seq, 0, i % per_seq)), (b, w, seq_len), F32)
        vals = lambda w: (pl.BlockSpec((tm // KEY_BLOCK, w, KEY_BLOCK), lambda i: (i, 0, 0)),
                          (n // KEY_BLOCK, w, KEY_BLOCK), BF16)
    else:
        cos_t, sin_t = (jnp.tile(t, (b, 1)) for t in (cos_t, sin_t))
        tab = row(LANES)
        narrow = lambda w: (row(w), (n, w), F32)
        vals = lambda w: (row(w), (n, w), BF16)
    outs = [
        narrow(W_A), narrow(W_A), (row(KV_LORA), (n, KV_LORA), F32), narrow(MLA_ROPE), narrow(W_C), narrow(W_C),
        narrow(H_C),
        (row(W_A), (n, W_A), BF16), (row(W_A), (n, W_A), BF16), vals(W_A),
        (pl.BlockSpec((H_B, tm, 2 * LANES), lambda i: (0, i, 0)), (H_B, n, 2 * LANES), BF16),
        (row(2 * LANES), (n, 2 * LANES), BF16),
        (row(W_C), (n, W_C), BF16), (row(W_C), (n, W_C), BF16), vals(W_C),
    ]
    res = pl.pallas_call(
        functools.partial(_inproj_kernel, feature_major=feature_major),
        grid=(n // tm,),
        in_specs=[row(d)] + [_layer_spec(c, layer) for c in consts] + [tab, tab],
        out_specs=[o[0] for o in outs],
        out_shape=[jax.ShapeDtypeStruct(o[1], o[2]) for o in outs],
        compiler_params=_params(1),
        name="inproj",
    )(h, *consts, cos_t, sin_t)
    return res, feature_major


def _cumsum_kernel(x_ref, o_ref, *, n_blocks):
    r = lax.broadcasted_iota(jnp.int32, (KEY_BLOCK, KEY_BLOCK), 0)
    c = lax.broadcasted_iota(jnp.int32, (KEY_BLOCK, KEY_BLOCK), 1)
    upper = jnp.where(r <= c, 1.0, 0.0).astype(BF16)
    carry = jnp.zeros((x_ref.shape[0], 1), F32)
    for j in range(n_blocks):
        sl = slice(j * KEY_BLOCK, (j + 1) * KEY_BLOCK)
        x = x_ref[:, sl]
        hi = x.astype(BF16)
        mid, lo = _split2(x - hi.astype(F32))
        f = _dot(hi, upper) + _dot(mid, upper) + _dot(lo, upper) + carry
        o_ref[:, sl] = f * LOG2E
        carry = f[:, KEY_BLOCK - 1:KEY_BLOCK]


def _cumsum_rows(x):
    b, r, tp = x.shape
    spec = pl.BlockSpec((None, r, tp), lambda i: (i, 0, 0))
    return pl.pallas_call(
        functools.partial(_cumsum_kernel, n_blocks=tp // KEY_BLOCK),
        grid=(b,),
        in_specs=[spec],
        out_specs=spec,
        out_shape=jax.ShapeDtypeStruct(x.shape, F32),
        compiler_params=_params(1),
        name="cumsum_logf",
    )(x)


def _block_range(q_start, tq, n_valid):
    n_full = q_start // KEY_BLOCK
    last = jnp.minimum(((q_start + tq + CHUNK - 1) // CHUNK) * CHUNK, n_valid)
    n_total = (last + KEY_BLOCK - 1) // KEY_BLOCK
    return n_full, n_total


def _positions(q_start, tq, j, cols):
    k_pos = j * KEY_BLOCK + lax.broadcasted_iota(jnp.int32, (KEY_BLOCK, cols), 0)
    lane = lax.broadcasted_iota(jnp.int32, (KEY_BLOCK, cols), 1)
    q_pos = q_start + (lane if cols == tq else lane % tq)
    return q_pos, k_pos


def _transpose_blocks(src_ref, dst_ref, width):
    for j in range(dst_ref.shape[0]):
        blk = src_ref[j * KEY_BLOCK:(j + 1) * KEY_BLOCK, :width].astype(F32)
        dst_ref[j] = blk.T.astype(BF16)


def _finite_or_zero(m):
    return jnp.where(m == -jnp.inf, 0.0, m)


def _head_masks(n_heads):
    lane = lax.broadcasted_iota(jnp.int32, (1, n_heads * HEAD_DIM), 1)
    return [(lane >= hh * HEAD_DIM) & (lane < (hh + 1) * HEAD_DIM) for hh in range(n_heads)]


def _sb_kernel(q_ref, k_ref, vt_ref, o_ref, qm_ref, acc_ref, c_ref, *, tq, q_pos0, n_valid):
    q_start = q_pos0 + pl.program_id(1) * tq
    n_full, n_total = _block_range(q_start, tq, n_valid)
    hmask = _head_masks(H_A)
    q = q_ref[...]
    for hh in range(H_A):
        qm_ref[hh] = jnp.where(hmask[hh], q, jnp.zeros_like(q))
    acc_ref[...] = jnp.zeros(acc_ref.shape, F32)
    c_ref[...] = jnp.zeros(c_ref.shape, F32)
    half = KEY_BLOCK // 2
    r = lax.broadcasted_iota(jnp.int32, (half, KEY_BLOCK), 0)
    c = lax.broadcasted_iota(jnp.int32, (half, KEY_BLOCK), 1)
    neg_upper2 = jnp.where(c % half >= r, -1.0, 0.0).astype(BF16)
    last_row = lax.broadcasted_iota(jnp.int32, (8, tq), 0) == 7

    def neg_suffix_sums(x):
        hi, lo = _split2(x)
        return _dot(neg_upper2, jnp.concatenate([hi, lo], axis=0))

    def block(j, masked):
        off = pl.multiple_of(j * KEY_BLOCK, KEY_BLOCK)
        kb = k_ref[pl.ds(off, KEY_BLOCK), :]
        if masked:
            q_pos, k_pos = _positions(q_start, tq, j, tq)
            mask = k_pos < q_pos
        heads = range(H_A)
        rows = [slice(hh * HEAD_DIM, (hh + 1) * HEAD_DIM) for hh in heads]
        vts = [vt_ref[j, rows[hh], :] for hh in heads]
        carry = c_ref[...]
        zs = [_dot_nt(kb, qm_ref[hh]) for hh in heads]
        sps = [_softplus(z) for z in zs]
        if masked:
            sps = [jnp.where(mask, sp, 0.0) for sp in sps]
        lates = [jnp.concatenate([sp[half:KEY_BLOCK - 8],
                                  jnp.where(last_row, sp[KEY_BLOCK - 8:] - carry[hh:hh + 1, :], sp[KEY_BLOCK - 8:])], axis=0)
                 for hh, sp in enumerate(sps)]
        cum_late = [neg_suffix_sums(x) for x in lates]
        cum_early = [neg_suffix_sums(sp[:half]) for sp in sps]
        cum_early = [cum_early[hh] + cum_late[hh][0:1, :] for hh in heads]
        ws = [jnp.exp(zs[hh] + jnp.concatenate([cum_early[hh], cum_late[hh]], axis=0)) for hh in heads]
        if masked:
            ws = [jnp.where(mask, w, 0.0) for w in ws]
        pvs = [_dot(vts[hh], ws[hh].astype(BF16)) for hh in heads]
        for hh in heads:
            acc_ref[rows[hh], :] += pvs[hh]
            c_ref[hh:hh + 1, :] = cum_early[hh][0:1, :]

    def masked_body(i, carry):
        block(n_total - 1 - i, True)
        return carry

    def full_body(i, carry):
        block(n_full - 1 - i, False)
        return carry

    lax.fori_loop(0, n_total - n_full, masked_body, 0)
    lax.fori_loop(0, n_full, full_body, 0)
    o_ref[...] = acc_ref[...].T


def _sb_attention(q, k, vt, q_pos0, n_valid):
    b, tq_all, w = q.shape
    tk = k.shape[1]
    tq = min(tq_all, KEY_BLOCK)
    qspec = pl.BlockSpec((None, tq, w), lambda bi, qi: (bi, qi, 0))
    kspec = pl.BlockSpec((None, tk, w), lambda bi, qi: (bi, 0, 0))
    vspec = pl.BlockSpec((None,) + vt.shape[1:], lambda bi, qi: (bi, 0, 0, 0))
    return pl.pallas_call(
        functools.partial(_sb_kernel, tq=tq, q_pos0=q_pos0, n_valid=n_valid),
        grid=(b, tq_all // tq),
        in_specs=[qspec, kspec, vspec],
        out_specs=qspec,
        out_shape=jax.ShapeDtypeStruct((b, tq_all, w), F32),
        scratch_shapes=[pltpu.VMEM((H_A, tq, w), BF16), pltpu.VMEM((w, tq), F32), pltpu.VMEM((8, tq), F32)],
        compiler_params=_params(2),
        name="sb_attention",
    )(q, k, vt)


def _fox_kernel(q_ref, k_ref, vt_ref, fq_ref, fk_ref, o_ref, qm_ref, acc_ref, m_ref, l_ref, *, tq, q_pos0, n_valid):
    q_start = q_pos0 + pl.program_id(1) * tq
    n_full, n_total = _block_range(q_start, tq, n_valid)
    hmask = _head_masks(H_C)
    q = q_ref[...]
    for hh in range(H_C):
        qm_ref[hh] = jnp.where(hmask[hh], q, jnp.zeros_like(q))
    acc_ref[...] = jnp.zeros(acc_ref.shape, F32)
    m_ref[...] = jnp.full(m_ref.shape, -jnp.inf, F32)
    l_ref[...] = jnp.zeros(l_ref.shape, F32)

    def block(j, masked):
        off = pl.multiple_of(j * KEY_BLOCK, KEY_BLOCK)
        kb = k_ref[pl.ds(off, KEY_BLOCK), :]
        fk = fk_ref[pl.ds(off, KEY_BLOCK), :]
        if masked:
            q_pos, k_pos = _positions(q_start, tq, j, tq)
            mask = k_pos <= q_pos
        heads = range(H_C)
        rows = [slice(hh * HEAD_DIM, (hh + 1) * HEAD_DIM) for hh in heads]
        vts = [vt_ref[j, rows[hh], :] for hh in heads]
        fq, m_all, l_all = fq_ref[...], m_ref[...], l_ref[...]
        accs = [acc_ref[rows[hh], :] for hh in heads]
        zs = []
        for hh in heads:
            z = _dot_nt(kb, qm_ref[hh]) + (fq[hh:hh + 1, :] - fk[:, hh:hh + 1])
            zs.append(jnp.where(mask, z, -jnp.inf) if masked else z)
        m_new = [jnp.maximum(m_all[hh:hh + 1, :], jnp.max(zs[hh], axis=0, keepdims=True)) for hh in heads]
        m_use = [_finite_or_zero(m) for m in m_new]
        ps = [jnp.exp2(zs[hh] - m_use[hh]) for hh in heads]
        alpha = [jnp.exp2(m_all[hh:hh + 1, :] - m_use[hh]) for hh in heads]
        l_new = [alpha[hh] * l_all[hh:hh + 1, :] + jnp.sum(ps[hh], axis=0, keepdims=True) for hh in heads]
        pvs = [_dot(vts[hh], ps[hh].astype(BF16)) for hh in heads]
        for hh in heads:
            m_ref[hh:hh + 1, :] = m_new[hh]
            l_ref[hh:hh + 1, :] = l_new[hh]
            acc_ref[rows[hh], :] = accs[hh] * alpha[hh] + pvs[hh]

    def masked_body(i, carry):
        block(n_total - 1 - i, True)
        return carry

    def full_body(i, carry):
        block(n_full - 1 - i, False)
        return carry

    lax.fori_loop(0, n_total - n_full, masked_body, 0)
    lax.fori_loop(0, n_full, full_body, 0)
    for hh in range(H_C):
        rows = slice(hh * HEAD_DIM, (hh + 1) * HEAD_DIM)
        acc_ref[rows, :] = acc_ref[rows, :] * (1.0 / l_ref[hh:hh + 1, :])
    o_ref[...] = acc_ref[...].T


def _fox_attention(q, k, vt, fq, fk, q_pos0, n_valid):
    b, tq_all, w = q.shape
    tk = k.shape[1]
    tq = min(tq_all, KEY_BLOCK)
    qspec = pl.BlockSpec((None, tq, w), lambda bi, qi: (bi, qi, 0))
    kspec = pl.BlockSpec((None, tk, w), lambda bi, qi: (bi, 0, 0))
    vspec = pl.BlockSpec((None,) + vt.shape[1:], lambda bi, qi: (bi, 0, 0, 0))
    fqspec = pl.BlockSpec((None, F_ROWS, tq), lambda bi, qi: (bi, 0, qi))
    fkspec = pl.BlockSpec((None, tk, F_ROWS), lambda bi, qi: (bi, 0, 0))
    return pl.pallas_call(
        functools.partial(_fox_kernel, tq=tq, q_pos0=q_pos0, n_valid=n_valid),
        grid=(b, tq_all // tq),
        in_specs=[qspec, kspec, vspec, fqspec, fkspec],
        out_specs=qspec,
        out_shape=jax.ShapeDtypeStruct((b, tq_all, w), F32),
        scratch_shapes=[pltpu.VMEM((H_C, tq, w), BF16), pltpu.VMEM((w, tq), F32),
                        pltpu.VMEM((8, tq), F32), pltpu.VMEM((8, tq), F32)],
        compiler_params=_params(2),
        name="fox_attention",
    )(q, k, vt, fq, fk)


def _mla_kernel(q_ref, kl_ref, wuvt_ref, o_ref, ct_ref, acc_ref, m_ref, l_ref, *, tq, q_pos0, n_valid):
    @pl.when(pl.program_id(1) == 0)
    def _():
        _transpose_blocks(kl_ref, ct_ref, KV_LORA)

    q_start = q_pos0 + pl.program_id(1) * tq
    n_full, n_total = _block_range(q_start, tq, n_valid)
    cols = H_B * tq
    qs = q_ref[...].reshape(cols, 2 * LANES)
    acc_ref[...] = jnp.zeros(acc_ref.shape, F32)
    m_ref[...] = jnp.full(m_ref.shape, -jnp.inf, F32)
    l_ref[...] = jnp.zeros(l_ref.shape, F32)

    group = 2 * tq
    groups = [slice(g * group, (g + 1) * group) for g in range(cols // group)]

    def block(j, masked):
        off = pl.multiple_of(j * KEY_BLOCK, KEY_BLOCK)
        kb = kl_ref[pl.ds(off, KEY_BLOCK), :]
        ct = ct_ref[j]
        if masked:
            q_pos, k_pos = _positions(q_start, tq, j, group)
            mask = (k_pos // CHUNK <= q_pos // CHUNK) & (k_pos < n_valid)
        m_all, l_all = m_ref[...], l_ref[...]
        accs = [acc_ref[:, g] for g in groups]
        zs = [_dot_nt(kb, qs[g]) for g in groups]
        if masked:
            zs = [jnp.where(mask, z, -jnp.inf) for z in zs]
        gs = range(len(groups))
        m_new = [jnp.maximum(m_all[:, groups[gi]], jnp.max(zs[gi], axis=0, keepdims=True)) for gi in gs]
        m_use = [_finite_or_zero(m) for m in m_new]
        ps = [jnp.exp2(zs[gi] - m_use[gi]) for gi in gs]
        alpha = [jnp.exp2(m_all[:, groups[gi]] - m_use[gi]) for gi in gs]
        l_new = [alpha[gi] * l_all[:, groups[gi]] + jnp.sum(ps[gi], axis=0, keepdims=True) for gi in gs]
        pvs = [_dot(ct, ps[gi].astype(BF16)) for gi in gs]
        for gi, g in enumerate(groups):
            m_ref[:, g] = m_new[gi]
            l_ref[:, g] = l_new[gi]
            acc_ref[:, g] = accs[gi] * alpha[gi] + pvs[gi]

    def masked_body(i, carry):
        block(n_total - 1 - i, True)
        return carry

    def full_body(i, carry):
        block(n_full - 1 - i, False)
        return carry

    lax.fori_loop(0, n_total - n_full, masked_body, 0)
    lax.fori_loop(0, n_full, full_body, 0)
    lat = (acc_ref[...] * (1.0 / l_ref[...])).astype(BF16)
    heads = [_dot(wuvt_ref[hh], lat[:, hh * tq:(hh + 1) * tq]) for hh in range(H_B)]
    o_ref[...] = jnp.concatenate(heads, axis=0).T


def _mla_attention(qm, kl, wuvt, layer, tq_all, q_pos0, n_valid):
    b, tk, w = kl.shape
    tq = min(tq_all, LANES)
    cols = H_B * tq
    return pl.pallas_call(
        functools.partial(_mla_kernel, tq=tq, q_pos0=q_pos0, n_valid=n_valid),
        grid=(b, tq_all // tq),
        in_specs=[pl.BlockSpec((H_B, None, tq, w), lambda bi, qi: (0, bi, qi, 0)),
                  pl.BlockSpec((None, tk, w), lambda bi, qi: (bi, 0, 0)),
                  _layer_spec(wuvt, layer)],
        out_specs=pl.BlockSpec((None, tq, W_B), lambda bi, qi: (bi, qi, 0)),
        out_shape=jax.ShapeDtypeStruct((b, tq_all, W_B), F32),
        scratch_shapes=[pltpu.VMEM((tk // KEY_BLOCK, KV_LORA, KEY_BLOCK), BF16), pltpu.VMEM((KV_LORA, cols), F32),
                        pltpu.VMEM((1, cols), F32), pltpu.VMEM((1, cols), F32)],
        compiler_params=_params(2),
        name="mla_attention",
    )(qm, kl, wuvt)


def _post_kernel(h_ref, oa_ref, ob_ref, oc_ref, p_ref, ggrp_ref, wout_ref, gmix_ref,
                 gpre_ref, gpost_ref, wgu_ref, wdown_ref, gple_pre_ref, wgate_ref, wproj_ref, gple_post_ref, o_ref,
                 *, d_ff, chunks):
    m = jnp.zeros(h_ref.shape, F32)
    c0 = 0
    for o_grp in (oa_ref, ob_ref, oc_ref):
        c1 = c0 + o_grp.shape[1]
        m = m + _dot(_rms(o_grp[...], ggrp_ref[:, c0:c1]).astype(BF16), wout_ref[c0:c1, :])
        c0 = c1
    h = h_ref[...] + _rms(m, gmix_ref[...])
    h = _ffn_rows(h, gpre_ref, gpost_ref, wgu_ref, wdown_ref, d_ff, chunks)
    gate = jax.nn.sigmoid(_dot(_rms(h, gple_pre_ref[...]).astype(BF16), wgate_ref[...]))
    e = _dot(p_ref[...].astype(BF16), wproj_ref[...]) * gate
    o_ref[...] = h + _rms(e, gple_post_ref[...])


def _post(h, oa, ob, oc, p, layer, consts):
    n, d = h.shape
    tm = _row_tile(n)
    row = lambda w: pl.BlockSpec((tm, w), lambda i: (i, 0))
    d_ff = consts[6].shape[1]
    return pl.pallas_call(
        functools.partial(_post_kernel, d_ff=d_ff, chunks=_ffn_chunks(d_ff)),
        grid=(n // tm,),
        in_specs=[row(d), row(W_A), row(W_B), row(W_C), row(p.shape[1])] + [_layer_spec(c, layer) for c in consts],
        out_specs=row(d),
        out_shape=jax.ShapeDtypeStruct((n, d), F32),
        compiler_params=_params(1),
        name="post",
    )(h, oa, ob, oc, p, *consts)


def _rope_tables(pos):
    half = MLA_ROPE // 2
    inv = ROPE_THETA ** (-jnp.arange(half, dtype=F32) / half)
    ang = pos.astype(F32)[:, None] * inv[None, :]
    cos, sin = jnp.cos(ang), jnp.sin(ang)
    pad = jnp.zeros((pos.shape[0], LANES - MLA_ROPE), F32)
    return jnp.concatenate([cos, cos, pad], axis=1), jnp.concatenate([-sin, sin, pad], axis=1)


def _prep_weights(weights):
    (g_ff1_pre, g_ff1_post, w_ff1_gu, w_ff1_down, g_mix_pre, g_mix_post, w_in, b_f, g_bq, g_bkv, w_uq, w_ukv,
     g_grp, w_out, g_ff2_pre, g_ff2_post, w_ff2_gu, w_ff2_down, g_ple_pre, w_ple_gate, w_ple_proj, g_ple_post) = weights
    depth, d = w_in.shape[:2]
    half = MLA_ROPE // 2
    row = lambda g: g.reshape(depth, 1, -1).astype(F32)
    bf = lambda w: w.astype(BF16)
    c_kr = 3 * W_A + Q_LORA + KV_LORA
    c_c = c_kr + MLA_ROPE
    c_f = c_c + 3 * W_C
    w_in = bf(w_in)
    kr = w_in[:, :, c_kr:c_kr + MLA_ROPE]
    kr_sw = jnp.concatenate([kr[:, :, half:], kr[:, :, :half]], axis=2)
    zpad = jnp.zeros((depth, d, LANES - MLA_ROPE), BF16)
    win_p = jnp.concatenate([w_in[:, :, :c_kr], kr, zpad, kr_sw, zpad, w_in[:, :, c_c:c_f], w_in[:, :, c_f:],
                             jnp.zeros((depth, d, LANES - H_C), BF16)], axis=2)
    assert win_p.shape[2] == IN_COLS_P
    bf_p = jnp.pad(b_f.astype(F32), ((0, 0), (0, LANES - H_C))).reshape(depth, 1, LANES)
    wkvt = jnp.transpose(jnp.concatenate([w_in[:, :, W_A:3 * W_A], w_in[:, :, c_c + W_C:c_f]], axis=2), (0, 2, 1))

    wq4 = bf(w_uq).reshape(depth, Q_LORA, H_B, MLA_NOPE + MLA_ROPE)
    wkv4 = bf(w_ukv).reshape(depth, KV_LORA, H_B, MLA_NOPE + MLA_V)
    wq_n = jnp.transpose(wq4[..., :MLA_NOPE], (0, 2, 1, 3))
    wk_n = jnp.transpose(wkv4[..., :MLA_NOPE], (0, 2, 1, 3))
    wcomb = _wcomb(wq_n, wk_n)
    x1 = wq4[..., MLA_NOPE:MLA_NOPE + half]
    x2 = wq4[..., MLA_NOPE + half:]
    zq = jnp.zeros((depth, Q_LORA, H_B, LANES - MLA_ROPE), BF16)
    wqr = jnp.concatenate([jnp.concatenate([x1, x2, zq], axis=3).reshape(depth, Q_LORA, H_B * LANES),
                           jnp.concatenate([x2, x1, zq], axis=3).reshape(depth, Q_LORA, H_B * LANES)], axis=2)
    wuvt = jnp.transpose(wkv4[..., MLA_NOPE:], (0, 2, 3, 1))

    return dict(
        ff1=(row(g_ff1_pre), row(g_ff1_post), bf(w_ff1_gu), bf(w_ff1_down)),
        inproj=(row(g_mix_pre), win_p, bf_p, row(g_bq), row(g_bkv), wcomb, wqr, wkvt),
        wuvt=wuvt,
        post=(row(g_grp), bf(w_out), row(g_mix_post),
              row(g_ff2_pre), row(g_ff2_post), bf(w_ff2_gu), bf(w_ff2_down),
              row(g_ple_pre), bf(w_ple_gate), bf(w_ple_proj), row(g_ple_post)),
    )


def _pad_keys(a, tk_pad):
    return jnp.pad(a, ((0, 0), (0, tk_pad - a.shape[1])) + ((0, 0),) * (a.ndim - 2))


def _forget_sums(logf_rows, tk_pad):
    _, heads, tk = logf_rows.shape
    return _cumsum_rows(jnp.pad(logf_rows, ((0, 0), (0, F_ROWS - heads), (0, tk_pad - tk))))


def _layer(h, p, seq_len, q_pos0, past, layer, lp, tables):
    n, d = h.shape
    b = n // seq_len
    h = _ffn(h, layer, *lp["ff1"])
    (ka_st, va_st, ckv_st, kr_st, kc_st, vc_st, lf_st,
     qa_b, ka_b, va_b, qm_b, kl_b, qc_b, kc_b, vc_b), feature_major = _inproj(h, seq_len, layer, *lp["inproj"], *tables)
    if feature_major:
        heads4 = lambda a, nh: jnp.transpose(a.reshape(b, nh, HEAD_DIM, seq_len), (0, 3, 1, 2))
        swap = lambda a: jnp.transpose(a, (0, 2, 1))
        state = (heads4(ka_st, H_A), heads4(va_st, H_A), ckv_st.reshape(b, seq_len, KV_LORA), swap(kr_st),
                 heads4(kc_st, H_C), heads4(vc_st, H_C), swap(lf_st))
        lf_rows_new = lf_st
    else:
        state = (ka_st.reshape(b, seq_len, H_A, HEAD_DIM), va_st.reshape(b, seq_len, H_A, HEAD_DIM),
                 ckv_st.reshape(b, seq_len, KV_LORA), kr_st.reshape(b, seq_len, MLA_ROPE),
                 kc_st.reshape(b, seq_len, H_C, HEAD_DIM), vc_st.reshape(b, seq_len, H_C, HEAD_DIM),
                 lf_st.reshape(b, seq_len, H_C))
        lf_rows_new = jnp.transpose(state[6], (0, 2, 1))
    seq3 = lambda a: a.reshape(b, seq_len, a.shape[-1])
    tq_pad = -(-seq_len // LANES) * LANES
    to_blocks = lambda v: jnp.transpose(v.reshape(b, v.shape[1] // KEY_BLOCK, KEY_BLOCK, v.shape[2]), (0, 1, 3, 2))
    if past is None:
        n_valid = seq_len
        tk_pad = -(-tq_pad // KEY_BLOCK) * KEY_BLOCK
        ka_all, kl_all, kc_all = (_pad_keys(seq3(a), tk_pad) for a in (ka_b, kl_b, kc_b))
        if feature_major:
            va_all, vc_all = (v.reshape(b, seq_len // KEY_BLOCK, v.shape[1], KEY_BLOCK) for v in (va_b, vc_b))
        else:
            va_all, vc_all = (to_blocks(_pad_keys(seq3(v), tk_pad)) for v in (va_b, vc_b))
        lf_rows = lf_rows_new
    else:
        assert not feature_major
        pa_k, pa_v, pb_ckv, pb_kr, pc_k, pc_v, pc_lf = past
        past_len = pa_k.shape[1]
        n_valid = past_len + seq_len
        tk_pad = -(-(past_len + tq_pad) // KEY_BLOCK) * KEY_BLOCK
        join = lambda c, new: _pad_keys(jnp.concatenate([c.reshape(b, past_len, -1).astype(BF16), seq3(new)], axis=1), tk_pad)
        ka_all, kc_all = join(pa_k, ka_b), join(pc_k, kc_b)
        va_all, vc_all = to_blocks(join(pa_v, va_b)), to_blocks(join(pc_v, vc_b))
        kl_past = jnp.concatenate([pb_ckv, pb_kr, jnp.zeros((b, past_len, LANES - MLA_ROPE), F32)], axis=-1)
        kl_all = join(kl_past, kl_b)
        lf_rows = jnp.concatenate([jnp.transpose(pc_lf, (0, 2, 1)), lf_rows_new], axis=2)
    f_rows = _forget_sums(lf_rows, tk_pad)
    f_q = f_rows[:, :, q_pos0:q_pos0 + tq_pad]
    f_k = jnp.transpose(f_rows, (0, 2, 1))
    pad_q = lambda a: jnp.pad(a, ((0, 0),) * (a.ndim - 2) + ((0, tq_pad - seq_len), (0, 0)))

    oa = _sb_attention(pad_q(seq3(qa_b)), ka_all, va_all, q_pos0, n_valid)[:, :seq_len]
    ob = _mla_attention(pad_q(qm_b.reshape(H_B, b, seq_len, 2 * LANES)), kl_all, lp["wuvt"], layer, tq_pad, q_pos0,
                        n_valid)[:, :seq_len]
    oc = _fox_attention(pad_q(seq3(qc_b)), kc_all, vc_all, f_q, f_k, q_pos0, n_valid)[:, :seq_len]
    flat = lambda a: a.reshape(n, a.shape[-1])
    h = _post(h, flat(oa), flat(ob), flat(oc), p, layer, lp["post"])
    return h, state


def _trunk(x, p, q_pos0, caches, lp, depth):
    b, t, d = x.shape
    tables = _rope_tables(q_pos0 + jnp.arange(t, dtype=jnp.int32))
    h = x.reshape(b * t, d)
    states = []
    for i in range(depth):
        past = None if caches is None else [c[i] for c in caches]
        h, st = _layer(h, p[i].reshape(b * t, -1), t, q_pos0, past, i, lp, tables)
        states.append(st)
    stacked = [jnp.stack([st[j] for st in states]) for j in range(len(states[0]))]
    return h.reshape(b, t, d), stacked


def kernel(x_prompt, x_sample, p_prompt, p_sample, cache_a_k, cache_a_v, cache_b_ckv, cache_b_krope, cache_c_k, cache_c_v, cache_c_logf, g_ff1_pre, g_ff1_post, w_ff1_gu, w_ff1_down, g_mix_pre, g_mix_post, w_in, b_f, g_bq, g_bkv, w_uq, w_ukv, g_grp, w_out, g_ff2_pre, g_ff2_post, w_ff2_gu, w_ff2_down, g_ple_pre, w_ple_gate, w_ple_proj, g_ple_post):
    weights = (g_ff1_pre, g_ff1_post, w_ff1_gu, w_ff1_down, g_mix_pre, g_mix_post, w_in, b_f,
               g_bq, g_bkv, w_uq, w_ukv, g_grp, w_out, g_ff2_pre, g_ff2_post, w_ff2_gu, w_ff2_down,
               g_ple_pre, w_ple_gate, w_ple_proj, g_ple_post)
    depth = w_in.shape[0]
    lp = _prep_weights(weights)
    y_prompt, sp = _trunk(x_prompt, p_prompt, 0, None, lp, depth)
    caches = (cache_a_k, cache_a_v, cache_b_ckv, cache_b_krope, cache_c_k, cache_c_v, cache_c_logf)
    y_sample, ss = _trunk(x_sample, p_sample, cache_a_k.shape[2], caches, lp, depth)
    return (y_prompt, y_sample, *sp, *ss)
```

```python
import functools
import math

import jax
import jax.numpy as jnp
from jax import lax
from jax.experimental import pallas as pl
from jax.experimental.pallas import tpu as pltpu

CHUNK = 64
HEAD_DIM = 64
H_A = 4
H_B = 8
H_C = 4
W_A = H_A * HEAD_DIM
MLA_NOPE = 64
MLA_ROPE = 32
MLA_V = 64
W_B = H_B * MLA_V
W_C = H_C * HEAD_DIM
Q_LORA = 256
KV_LORA = 128
ROPE_THETA = 10000.0
EPS = 1e-6
FFN_RES = 0.5
SB_SCALE = HEAD_DIM ** -0.5
MLA_SCALE = (MLA_NOPE + MLA_ROPE) ** -0.5
FOX_SCALE = HEAD_DIM ** -0.5
LOG2E = math.log2(math.e)

LANES = 128
KEY_BLOCK = 256
F_ROWS = 16
VMEM_LIMIT = 56 * 1024 * 1024

COL_A = 0
COL_CQ = COL_A + 3 * W_A
COL_CKV = COL_CQ + Q_LORA
COL_KRA = COL_CKV + KV_LORA
COL_KRB = COL_KRA + LANES
COL_C = COL_KRB + LANES
COL_F = COL_C + 3 * W_C
IN_COLS_P = COL_F + LANES

BF16 = jnp.bfloat16
F32 = jnp.float32


def _dot(a, b):
    return jnp.dot(a, b, preferred_element_type=F32)


def _dot_nt(a, b):
    return lax.dot_general(a, b, (((1,), (1,)), ((), ())), preferred_element_type=F32)


def _rms(x, g):
    ms = jnp.mean(x * x, axis=-1, keepdims=True)
    return x * lax.rsqrt(ms + EPS) * g


def _log_sigmoid(x):
    return jnp.minimum(x, 0.0) - jnp.log(1.0 + jnp.exp(-jnp.abs(x)))


def _softplus(x):
    return jnp.maximum(x, 0.0) + jnp.log(1.0 + jnp.exp2(jnp.abs(x) * (-LOG2E)))


def _split2(x):
    hi = x.astype(BF16)
    lo = (x - hi.astype(F32)).astype(BF16)
    return hi, lo


def _layer_spec(a, layer):
    idx = (layer,) + (0,) * (a.ndim - 1)
    return pl.BlockSpec((None,) + a.shape[1:], lambda *_: idx, pipeline_mode=pl.Buffered(1))


def _params(n_axes):
    return pltpu.CompilerParams(dimension_semantics=("arbitrary",) * n_axes, vmem_limit_bytes=VMEM_LIMIT)


def _row_tile(n):
    for tm in (512, 256):
        if n % tm == 0:
            return tm
    return n


def _ffn_rows(h, gpre_ref, gpost_ref, wgu_ref, wdown_ref, d_ff, chunks):
    xn = _rms(h, gpre_ref[...]).astype(BF16)
    acc = jnp.zeros(h.shape, F32)
    for c0, c1 in chunks:
        g = _dot(xn, wgu_ref[:, c0:c1])
        u = _dot(xn, wgu_ref[:, d_ff + c0:d_ff + c1])
        a = (g * jax.nn.sigmoid(g) * u).astype(BF16)
        acc = acc + _dot(a, wdown_ref[c0:c1, :])
    return h + FFN_RES * _rms(acc, gpost_ref[...])


def _ffn_chunks(d_ff):
    step = 4 * KEY_BLOCK
    return tuple((c, min(c + step, d_ff)) for c in range(0, d_ff, step))


def _ffn_kernel(h_ref, gpre_ref, gpost_ref, wgu_ref, wdown_ref, o_ref, *, d_ff, chunks):
    o_ref[...] = _ffn_rows(h_ref[...], gpre_ref, gpost_ref, wgu_ref, wdown_ref, d_ff, chunks)


def _ffn(h, layer, gpre, gpost, wgu, wdown):
    n, d = h.shape
    d_ff = wdown.shape[1]
    consts = [gpre, gpost, wgu, wdown]
    tm = _row_tile(n)
    chunks = _ffn_chunks(d_ff)
    row = pl.BlockSpec((tm, d), lambda i: (i, 0))
    return pl.pallas_call(
        functools.partial(_ffn_kernel, d_ff=d_ff, chunks=chunks),
        grid=(n // tm,),
        in_specs=[row] + [_layer_spec(c, layer) for c in consts],
        out_specs=row,
        out_shape=jax.ShapeDtypeStruct((n, d), F32),
        compiler_params=_params(1),
        name="ffn",
    )(h, gpre, gpost, wgu, wdown)


def _wcomb_kernel(wq_ref, wk_ref, o_ref):
    o_ref[...] = _dot_nt(wq_ref[...], wk_ref[...]).astype(BF16)


def _wcomb(wq_n, wk_n):
    depth, nh, ql, dn = wq_n.shape
    kl = wk_n.shape[2]
    return pl.pallas_call(
        _wcomb_kernel,
        grid=(depth, nh),
        in_specs=[pl.BlockSpec((None, None, ql, dn), lambda i, h: (i, h, 0, 0)),
                  pl.BlockSpec((None, None, kl, dn), lambda i, h: (i, h, 0, 0))],
        out_specs=pl.BlockSpec((None, ql, kl), lambda i, h: (i, 0, h)),
        out_shape=jax.ShapeDtypeStruct((depth, ql, nh * kl), BF16),
        compiler_params=_params(2),
        name="wcomb",
    )(wq_n, wk_n)


def _inproj_kernel(h_ref, gpre_ref, win_ref, bf_ref, gbq_ref, gbkv_ref, wcomb_ref, wqr_ref, wkvt_ref, cos_ref, sin_ref,
                   ka_st, va_st, ckv_st, kr_st, kc_st, vc_st, lf_st,
                   qa_b, ka_b, va_b, qm_b, kl_b, qc_b, kc_b, vc_b, *, feature_major):
    xn = _rms(h_ref[...], gpre_ref[...]).astype(BF16)
    proj = _dot(xn, win_ref[...])
    cos = cos_ref[...]
    sin = sin_ref[...]

    qa_b[...] = (proj[:, COL_A:COL_A + W_A] * SB_SCALE).astype(BF16)
    qc_b[...] = (proj[:, COL_C:COL_C + W_C] * (FOX_SCALE * LOG2E)).astype(BF16)
    ka = proj[:, COL_A + W_A:COL_A + 2 * W_A]
    kc = proj[:, COL_C + W_C:COL_C + 2 * W_C]
    ka_b[...] = ka.astype(BF16)
    kc_b[...] = kc.astype(BF16)
    lf = _log_sigmoid(proj[:, COL_F:COL_F + LANES] + bf_ref[...])
    ckv = _rms(proj[:, COL_CKV:COL_CKV + KV_LORA], gbkv_ref[...])
    ckv_st[...] = ckv
    kr = proj[:, COL_KRA:COL_KRA + LANES] * cos + proj[:, COL_KRB:COL_KRB + LANES] * sin
    kl_b[:, :KV_LORA] = ckv.astype(BF16)
    kl_b[:, KV_LORA:] = kr.astype(BF16)

    if feature_major:
        kvt = _dot_nt(wkvt_ref[...], xn)
        for idx, st in enumerate((ka_st, va_st, kc_st, vc_st)):
            st[...] = kvt[idx * W_A:(idx + 1) * W_A]
        for r in range(va_b.shape[0]):
            cols = slice(r * KEY_BLOCK, (r + 1) * KEY_BLOCK)
            va_b[r] = kvt[W_A:2 * W_A, cols].astype(BF16)
            vc_b[r] = kvt[3 * W_A:4 * W_A, cols].astype(BF16)
        kr_st[...] = kr.T[:MLA_ROPE]
        lf_st[...] = lf.T[:H_C]
    else:
        va = proj[:, COL_A + 2 * W_A:COL_A + 3 * W_A]
        vc = proj[:, COL_C + 2 * W_C:COL_C + 3 * W_C]
        ka_st[...] = ka
        va_st[...] = va
        kc_st[...] = kc
        vc_st[...] = vc
        va_b[...] = va.astype(BF16)
        vc_b[...] = vc.astype(BF16)
        kr_st[...] = kr[:, :MLA_ROPE]
        lf_st[...] = lf[:, :H_C]

    cqn = _rms(proj[:, COL_CQ:COL_CQ + Q_LORA], gbq_ref[...]).astype(BF16)
    qlat = _dot(cqn, wcomb_ref[...])
    qr = _dot(cqn, wqr_ref[...])
    half = H_B * LANES
    for hh in range(H_B):
        sl = slice(hh * LANES, (hh + 1) * LANES)
        rope = qr[:, sl] * cos + qr[:, half + hh * LANES:half + (hh + 1) * LANES] * sin
        qm_b[hh, :, :KV_LORA] = (qlat[:, sl] * (MLA_SCALE * LOG2E)).astype(BF16)
        qm_b[hh, :, KV_LORA:] = (rope * (MLA_SCALE * LOG2E)).astype(BF16)


def _inproj(h, seq_len, layer, gpre, win, bf, gbq, gbkv, wcomb, wqr, wkvt, cos_t, sin_t):
    n, d = h.shape
    b = n // seq_len
    tm = _row_tile(n)
    row = lambda w: pl.BlockSpec((tm, w), lambda i: (i, 0))
    consts = [gpre, win, bf, gbq, gbkv, wcomb, wqr, wkvt]
    feature_major = seq_len % tm == 0 and tm % KEY_BLOCK == 0
    if feature_major:
        per_seq = seq_len // tm
        tab = pl.BlockSpec((tm, LANES), lambda i: (i % per_seq, 0))
        narrow = lambda w: (pl.BlockSpec((None, w, tm), lambda i: (i // per_seq, 0, i % per_seq)), (b, w, seq_len), F32)
        vals = lambda w: (pl.BlockSpec((tm // KEY_BLOCK, w, KEY_BLOCK), lambda i: (i, 0, 0)),
                          (n // KEY_BLOCK, w, KEY_BLOCK), BF16)
    else:
        cos_t, sin_t = (jnp.tile(t, (b, 1)) for t in (cos_t, sin_t))
        tab = row(LANES)
        narrow = lambda w: (row(w), (n, w), F32)
        vals = lambda w: (row(w), (n, w), BF16)
    outs = [
        narrow(W_A), narrow(W_A), (row(KV_LORA), (n, KV_LORA), F32), narrow(MLA_ROPE), narrow(W_C), narrow(W_C),
        narrow(H_C),
        (row(W_A), (n, W_A), BF16), (row(W_A), (n, W_A), BF16), vals(W_A),
        (pl.BlockSpec((H_B, tm, 2 * LANES), lambda i: (0, i, 0)), (H_B, n, 2 * LANES), BF16),
        (row(2 * LANES), (n, 2 * LANES), BF16),
        (row(W_C), (n, W_C), BF16), (row(W_C), (n, W_C), BF16), vals(W_C),
    ]
    res = pl.pallas_call(
        functools.partial(_inproj_kernel, feature_major=feature_major),
        grid=(n // tm,),
        in_specs=[row(d)] + [_layer_spec(c, layer) for c in consts] + [tab, tab],
        out_specs=[o[0] for o in outs],
        out_shape=[jax.ShapeDtypeStruct(o[1], o[2]) for o in outs],
        compiler_params=_params(1),
        name="inproj",
    )(h, *consts, cos_t, sin_t)
    return res, feature_major


def _cumsum_kernel(x_ref, o_ref, *, n_blocks):
    r = lax.broadcasted_iota(jnp.int32, (KEY_BLOCK, KEY_BLOCK), 0)
    c = lax.broadcasted_iota(jnp.int32, (KEY_BLOCK, KEY_BLOCK), 1)
    upper = jnp.where(r <= c, 1.0, 0.0).astype(BF16)
    carry = jnp.zeros((x_ref.shape[0], 1), F32)
    for j in range(n_blocks):
        sl = slice(j * KEY_BLOCK, (j + 1) * KEY_BLOCK)
        x = x_ref[:, sl]
        hi = x.astype(BF16)
        mid, lo = _split2(x - hi.astype(F32))
        f = _dot(hi, upper) + _dot(mid, upper) + _dot(lo, upper) + carry
        o_ref[:, sl] = f * LOG2E
        carry = f[:, KEY_BLOCK - 1:KEY_BLOCK]


def _cumsum_rows(x):
    b, r, tp = x.shape
    spec = pl.BlockSpec((None, r, tp), lambda i: (i, 0, 0))
    return pl.pallas_call(
        functools.partial(_cumsum_kernel, n_blocks=tp // KEY_BLOCK),
        grid=(b,),
        in_specs=[spec],
        out_specs=spec,
        out_shape=jax.ShapeDtypeStruct(x.shape, F32),
        compiler_params=_params(1),
        name="cumsum_logf",
    )(x)


def _block_range(q_start, tq, n_valid):
    n_full = q_start // KEY_BLOCK
    last = jnp.minimum(((q_start + tq + CHUNK - 1) // CHUNK) * CHUNK, n_valid)
    n_total = (last + KEY_BLOCK - 1) // KEY_BLOCK
    return n_full, n_total


def _positions(q_start, tq, j, cols):
    k_pos = j * KEY_BLOCK + lax.broadcasted_iota(jnp.int32, (KEY_BLOCK, cols), 0)
    lane = lax.broadcasted_iota(jnp.int32, (KEY_BLOCK, cols), 1)
    q_pos = q_start + (lane if cols == tq else lane % tq)
    return q_pos, k_pos


def _transpose_blocks(src_ref, dst_ref, width):
    for j in range(dst_ref.shape[0]):
        blk = src_ref[j * KEY_BLOCK:(j + 1) * KEY_BLOCK, :width].astype(F32)
        dst_ref[j] = blk.T.astype(BF16)


def _pipelined_blocks(n_full, produce, consume, store, load):
    f = lambda i: n_full - 1 - i
    n_pairs = jnp.maximum(n_full - 1, 0) // 2
    rest = n_full - 2 * n_pairs

    def both(i, slot):
        consume(f(i), load(slot), overlap=(lambda: produce(f(i + 1)), lambda vals: store(1 - slot, vals)))

    @pl.when(n_full > 0)
    def _():
        store(0, produce(f(0)))

    def pair_body(t, carry):
        both(2 * t, 0)
        both(2 * t + 1, 1)
        return carry

    lax.fori_loop(0, n_pairs, pair_body, 0)

    @pl.when((n_full > 0) & (rest == 1))
    def _():
        consume(f(2 * n_pairs), load(0))

    @pl.when((n_full > 0) & (rest == 2))
    def _():
        both(2 * n_pairs, 0)
        consume(f(2 * n_pairs + 1), load(1))


def _finite_or_zero(m):
    return jnp.where(m == -jnp.inf, 0.0, m)


def _head_masks(n_heads):
    lane = lax.broadcasted_iota(jnp.int32, (1, n_heads * HEAD_DIM), 1)
    return [(lane >= hh * HEAD_DIM) & (lane < (hh + 1) * HEAD_DIM) for hh in range(n_heads)]


def _sb_kernel(q_ref, k_ref, vt_ref, o_ref, qm_ref, z_ref, x_ref, acc_ref, c_ref, *, tq, q_pos0, n_valid):
    q_start = q_pos0 + pl.program_id(1) * tq
    n_full, n_total = _block_range(q_start, tq, n_valid)
    hmask = _head_masks(H_A)
    q = q_ref[...]
    for hh in range(H_A):
        qm_ref[hh] = jnp.where(hmask[hh], q, jnp.zeros_like(q))
    acc_ref[...] = jnp.zeros(acc_ref.shape, F32)
    c_ref[...] = jnp.zeros(c_ref.shape, F32)
    half = KEY_BLOCK // 2
    r = lax.broadcasted_iota(jnp.int32, (half, KEY_BLOCK), 0)
    c = lax.broadcasted_iota(jnp.int32, (half, KEY_BLOCK), 1)
    neg_upper2 = jnp.where(c % half >= r, -1.0, 0.0).astype(BF16)
    heads = range(H_A)
    rows = [slice(hh * HEAD_DIM, (hh + 1) * HEAD_DIM) for hh in heads]

    def parts(x):
        hi, lo = _split2(x)
        return jnp.concatenate([hi, lo], axis=0)

    def first_stage(j, mask=None):
        off = pl.multiple_of(j * KEY_BLOCK, KEY_BLOCK)
        kb = k_ref[pl.ds(off, KEY_BLOCK), :]
        zs = [_dot_nt(kb, qm_ref[hh]) for hh in heads]
        sps = [_softplus(z) for z in zs]
        if mask is not None:
            sps = [jnp.where(mask, sp, 0.0) for sp in sps]
        return zs, [parts(sp[:half]) for sp in sps], [parts(sp[half:]) for sp in sps]

    def second_stage(j, vals, overlap=None, mask=None):
        zs, early, late = vals
        vts = [vt_ref[j, rows[hh], :] for hh in heads]
        carry = c_ref[...]
        upcoming = overlap[0]() if overlap else None
        cum_late = [_dot(neg_upper2, late[hh]) + carry[hh:hh + 1, :] for hh in heads]
        cum_early = [_dot(neg_upper2, early[hh]) + cum_late[hh][0:1, :] for hh in heads]
        ws = [jnp.exp(zs[hh] + jnp.concatenate([cum_early[hh], cum_late[hh]], axis=0)) for hh in heads]
        if mask is not None:
            ws = [jnp.where(mask, w, 0.0) for w in ws]
        pvs = [_dot(vts[hh], ws[hh].astype(BF16)) for hh in heads]
        for hh in heads:
            acc_ref[rows[hh], :] += pvs[hh]
            c_ref[hh:hh + 1, :] = cum_early[hh][0:1, :]
        if overlap:
            overlap[1](upcoming)

    def store(slot, vals):
        zs, early, late = vals
        for hh in heads:
            z_ref[slot, hh] = zs[hh]
            x_ref[slot, hh, 0] = early[hh]
            x_ref[slot, hh, 1] = late[hh]

    def load(slot):
        return ([z_ref[slot, hh] for hh in heads], [x_ref[slot, hh, 0] for hh in heads],
                [x_ref[slot, hh, 1] for hh in heads])

    def masked_body(i, carry):
        j = n_total - 1 - i
        q_pos, k_pos = _positions(q_start, tq, j, tq)
        mask = k_pos < q_pos
        second_stage(j, first_stage(j, mask), mask=mask)
        return carry

    lax.fori_loop(0, n_total - n_full, masked_body, 0)
    _pipelined_blocks(n_full, first_stage, second_stage, store, load)
    o_ref[...] = acc_ref[...].T


def _sb_attention(q, k, vt, q_pos0, n_valid):
    b, tq_all, w = q.shape
    tk = k.shape[1]
    tq = min(tq_all, KEY_BLOCK)
    qspec = pl.BlockSpec((None, tq, w), lambda bi, qi: (bi, qi, 0))
    kspec = pl.BlockSpec((None, tk, w), lambda bi, qi: (bi, 0, 0))
    vspec = pl.BlockSpec((None,) + vt.shape[1:], lambda bi, qi: (bi, 0, 0, 0))
    return pl.pallas_call(
        functools.partial(_sb_kernel, tq=tq, q_pos0=q_pos0, n_valid=n_valid),
        grid=(b, tq_all // tq),
        in_specs=[qspec, kspec, vspec],
        out_specs=qspec,
        out_shape=jax.ShapeDtypeStruct((b, tq_all, w), F32),
        scratch_shapes=[pltpu.VMEM((H_A, tq, w), BF16), pltpu.VMEM((2, H_A, KEY_BLOCK, tq), F32),
                        pltpu.VMEM((2, H_A, 2, KEY_BLOCK, tq), BF16), pltpu.VMEM((w, tq), F32), pltpu.VMEM((8, tq), F32)],
        compiler_params=_params(2),
        name="sb_attention",
    )(q, k, vt)


def _fox_kernel(q_ref, k_ref, vt_ref, fq_ref, fk_ref, o_ref, qm_ref, z_ref, acc_ref, m_ref, l_ref, *, tq, q_pos0, n_valid):
    q_start = q_pos0 + pl.program_id(1) * tq
    n_full, n_total = _block_range(q_start, tq, n_valid)
    hmask = _head_masks(H_C)
    q = q_ref[...]
    for hh in range(H_C):
        qm_ref[hh] = jnp.where(hmask[hh], q, jnp.zeros_like(q))
    acc_ref[...] = jnp.zeros(acc_ref.shape, F32)
    m_ref[...] = jnp.full(m_ref.shape, -jnp.inf, F32)
    l_ref[...] = jnp.zeros(l_ref.shape, F32)

    heads = range(H_C)
    rows = [slice(hh * HEAD_DIM, (hh + 1) * HEAD_DIM) for hh in heads]

    def scores(j):
        off = pl.multiple_of(j * KEY_BLOCK, KEY_BLOCK)
        kb = k_ref[pl.ds(off, KEY_BLOCK), :]
        return [_dot_nt(kb, qm_ref[hh]) for hh in heads]

    def step(j, qk, overlap=None, mask=None):
        off = pl.multiple_of(j * KEY_BLOCK, KEY_BLOCK)
        fk = fk_ref[pl.ds(off, KEY_BLOCK), :]
        vts = [vt_ref[j, rows[hh], :] for hh in heads]
        fq, m_all, l_all = fq_ref[...], m_ref[...], l_ref[...]
        accs = [acc_ref[rows[hh], :] for hh in heads]
        upcoming = overlap[0]() if overlap else None
        zs = [qk[hh] + (fq[hh:hh + 1, :] - fk[:, hh:hh + 1]) for hh in heads]
        if mask is not None:
            zs = [jnp.where(mask, z, -jnp.inf) for z in zs]
        m_new = [jnp.maximum(m_all[hh:hh + 1, :], jnp.max(zs[hh], axis=0, keepdims=True)) for hh in heads]
        m_use = [_finite_or_zero(m) for m in m_new]
        ps = [jnp.exp2(zs[hh] - m_use[hh]) for hh in heads]
        alpha = [jnp.exp2(m_all[hh:hh + 1, :] - m_use[hh]) for hh in heads]
        l_new = [alpha[hh] * l_all[hh:hh + 1, :] + jnp.sum(ps[hh], axis=0, keepdims=True) for hh in heads]
        pvs = [_dot(vts[hh], ps[hh].astype(BF16)) for hh in heads]
        for hh in heads:
            m_ref[hh:hh + 1, :] = m_new[hh]
            l_ref[hh:hh + 1, :] = l_new[hh]
            acc_ref[rows[hh], :] = accs[hh] * alpha[hh] + pvs[hh]
        if overlap:
            overlap[1](upcoming)

    def store(slot, qk):
        for hh in heads:
            z_ref[slot, hh] = qk[hh]

    def load(slot):
        return [z_ref[slot, hh] for hh in heads]

    def masked_body(i, carry):
        j = n_total - 1 - i
        q_pos, k_pos = _positions(q_start, tq, j, tq)
        step(j, scores(j), mask=k_pos <= q_pos)
        return carry

    lax.fori_loop(0, n_total - n_full, masked_body, 0)
    _pipelined_blocks(n_full, scores, step, store, load)
    for hh in range(H_C):
        rows = slice(hh * HEAD_DIM, (hh + 1) * HEAD_DIM)
        acc_ref[rows, :] = acc_ref[rows, :] * (1.0 / l_ref[hh:hh + 1, :])
    o_ref[...] = acc_ref[...].T


def _fox_attention(q, k, vt, fq, fk, q_pos0, n_valid):
    b, tq_all, w = q.shape
    tk = k.shape[1]
    tq = min(tq_all, KEY_BLOCK)
    qspec = pl.BlockSpec((None, tq, w), lambda bi, qi: (bi, qi, 0))
    kspec = pl.BlockSpec((None, tk, w), lambda bi, qi: (bi, 0, 0))
    vspec = pl.BlockSpec((None,) + vt.shape[1:], lambda bi, qi: (bi, 0, 0, 0))
    fqspec = pl.BlockSpec((None, F_ROWS, tq), lambda bi, qi: (bi, 0, qi))
    fkspec = pl.BlockSpec((None, tk, F_ROWS), lambda bi, qi: (bi, 0, 0))
    return pl.pallas_call(
        functools.partial(_fox_kernel, tq=tq, q_pos0=q_pos0, n_valid=n_valid),
        grid=(b, tq_all // tq),
        in_specs=[qspec, kspec, vspec, fqspec, fkspec],
        out_specs=qspec,
        out_shape=jax.ShapeDtypeStruct((b, tq_all, w), F32),
        scratch_shapes=[pltpu.VMEM((H_C, tq, w), BF16), pltpu.VMEM((2, H_C, KEY_BLOCK, tq), F32),
                        pltpu.VMEM((w, tq), F32), pltpu.VMEM((8, tq), F32), pltpu.VMEM((8, tq), F32)],
        compiler_params=_params(2),
        name="fox_attention",
    )(q, k, vt, fq, fk)


def _mla_kernel(q_ref, kl_ref, wuvt_ref, o_ref, ct_ref, z_ref, acc_ref, m_ref, l_ref, *, tq, q_pos0, n_valid):
    @pl.when(pl.program_id(1) == 0)
    def _():
        _transpose_blocks(kl_ref, ct_ref, KV_LORA)

    q_start = q_pos0 + pl.program_id(1) * tq
    n_full, n_total = _block_range(q_start, tq, n_valid)
    cols = H_B * tq
    qs = q_ref[...].reshape(cols, 2 * LANES)
    acc_ref[...] = jnp.zeros(acc_ref.shape, F32)
    m_ref[...] = jnp.full(m_ref.shape, -jnp.inf, F32)
    l_ref[...] = jnp.zeros(l_ref.shape, F32)

    group = 2 * tq
    groups = [slice(g * group, (g + 1) * group) for g in range(cols // group)]

    gs = range(len(groups))

    def scores(j):
        off = pl.multiple_of(j * KEY_BLOCK, KEY_BLOCK)
        kb = kl_ref[pl.ds(off, KEY_BLOCK), :]
        return [_dot_nt(kb, qs[g]) for g in groups]

    def step(j, zs, overlap=None):
        ct = ct_ref[j]
        m_all, l_all = m_ref[...], l_ref[...]
        accs = [acc_ref[:, g] for g in groups]
        upcoming = overlap[0]() if overlap else None
        m_new = [jnp.maximum(m_all[:, groups[gi]], jnp.max(zs[gi], axis=0, keepdims=True)) for gi in gs]
        m_use = [_finite_or_zero(m) for m in m_new]
        ps = [jnp.exp2(zs[gi] - m_use[gi]) for gi in gs]
        alpha = [jnp.exp2(m_all[:, groups[gi]] - m_use[gi]) for gi in gs]
        l_new = [alpha[gi] * l_all[:, groups[gi]] + jnp.sum(ps[gi], axis=0, keepdims=True) for gi in gs]
        pvs = [_dot(ct, ps[gi].astype(BF16)) for gi in gs]
        for gi, g in enumerate(groups):
            m_ref[:, g] = m_new[gi]
            l_ref[:, g] = l_new[gi]
            acc_ref[:, g] = accs[gi] * alpha[gi] + pvs[gi]
        if overlap:
            overlap[1](upcoming)

    def store(slot, zs):
        for gi, g in enumerate(groups):
            z_ref[slot, :, g] = zs[gi]

    def load(slot):
        return [z_ref[slot, :, g] for g in groups]

    def masked_body(i, carry):
        j = n_total - 1 - i
        q_pos, k_pos = _positions(q_start, tq, j, group)
        mask = (k_pos // CHUNK <= q_pos // CHUNK) & (k_pos < n_valid)
        step(j, [jnp.where(mask, z, -jnp.inf) for z in scores(j)])
        return carry

    lax.fori_loop(0, n_total - n_full, masked_body, 0)

    _pipelined_blocks(n_full, scores, step, store, load)

    lat = (acc_ref[...] * (1.0 / l_ref[...])).astype(BF16)
    heads = [_dot(wuvt_ref[hh], lat[:, hh * tq:(hh + 1) * tq]) for hh in range(H_B)]
    o_ref[...] = jnp.concatenate(heads, axis=0).T


def _mla_attention(qm, kl, wuvt, layer, tq_all, q_pos0, n_valid):
    b, tk, w = kl.shape
    tq = min(tq_all, LANES)
    cols = H_B * tq
    return pl.pallas_call(
        functools.partial(_mla_kernel, tq=tq, q_pos0=q_pos0, n_valid=n_valid),
        grid=(b, tq_all // tq),
        in_specs=[pl.BlockSpec((H_B, None, tq, w), lambda bi, qi: (0, bi, qi, 0)),
                  pl.BlockSpec((None, tk, w), lambda bi, qi: (bi, 0, 0)),
                  _layer_spec(wuvt, layer)],
        out_specs=pl.BlockSpec((None, tq, W_B), lambda bi, qi: (bi, qi, 0)),
        out_shape=jax.ShapeDtypeStruct((b, tq_all, W_B), F32),
        scratch_shapes=[pltpu.VMEM((tk // KEY_BLOCK, KV_LORA, KEY_BLOCK), BF16), pltpu.VMEM((2, KEY_BLOCK, cols), F32),
                        pltpu.VMEM((KV_LORA, cols), F32), pltpu.VMEM((1, cols), F32), pltpu.VMEM((1, cols), F32)],
        compiler_params=_params(2),
        name="mla_attention",
    )(qm, kl, wuvt)


def _post_kernel(h_ref, oa_ref, ob_ref, oc_ref, p_ref, ggrp_ref, wout_ref, gmix_ref,
                 gpre_ref, gpost_ref, wgu_ref, wdown_ref, gple_pre_ref, wgate_ref, wproj_ref, gple_post_ref, o_ref,
                 *, d_ff, chunks):
    m = jnp.zeros(h_ref.shape, F32)
    c0 = 0
    for o_grp in (oa_ref, ob_ref, oc_ref):
        c1 = c0 + o_grp.shape[1]
        m = m + _dot(_rms(o_grp[...], ggrp_ref[:, c0:c1]).astype(BF16), wout_ref[c0:c1, :])
        c0 = c1
    h = h_ref[...] + _rms(m, gmix_ref[...])
    h = _ffn_rows(h, gpre_ref, gpost_ref, wgu_ref, wdown_ref, d_ff, chunks)
    gate = jax.nn.sigmoid(_dot(_rms(h, gple_pre_ref[...]).astype(BF16), wgate_ref[...]))
    e = _dot(p_ref[...].astype(BF16), wproj_ref[...]) * gate
    o_ref[...] = h + _rms(e, gple_post_ref[...])


def _post(h, oa, ob, oc, p, layer, consts):
    n, d = h.shape
    tm = _row_tile(n)
    row = lambda w: pl.BlockSpec((tm, w), lambda i: (i, 0))
    d_ff = consts[6].shape[1]
    return pl.pallas_call(
        functools.partial(_post_kernel, d_ff=d_ff, chunks=_ffn_chunks(d_ff)),
        grid=(n // tm,),
        in_specs=[row(d), row(W_A), row(W_B), row(W_C), row(p.shape[1])] + [_layer_spec(c, layer) for c in consts],
        out_specs=row(d),
        out_shape=jax.ShapeDtypeStruct((n, d), F32),
        compiler_params=_params(1),
        name="post",
    )(h, oa, ob, oc, p, *consts)


def _rope_tables(pos):
    half = MLA_ROPE // 2
    inv = ROPE_THETA ** (-jnp.arange(half, dtype=F32) / half)
    ang = pos.astype(F32)[:, None] * inv[None, :]
    cos, sin = jnp.cos(ang), jnp.sin(ang)
    pad = jnp.zeros((pos.shape[0], LANES - MLA_ROPE), F32)
    return jnp.concatenate([cos, cos, pad], axis=1), jnp.concatenate([-sin, sin, pad], axis=1)


def _prep_weights(weights):
    (g_ff1_pre, g_ff1_post, w_ff1_gu, w_ff1_down, g_mix_pre, g_mix_post, w_in, b_f, g_bq, g_bkv, w_uq, w_ukv,
     g_grp, w_out, g_ff2_pre, g_ff2_post, w_ff2_gu, w_ff2_down, g_ple_pre, w_ple_gate, w_ple_proj, g_ple_post) = weights
    depth, d = w_in.shape[:2]
    half = MLA_ROPE // 2
    row = lambda g: g.reshape(depth, 1, -1).astype(F32)
    bf = lambda w: w.astype(BF16)
    c_kr = 3 * W_A + Q_LORA + KV_LORA
    c_c = c_kr + MLA_ROPE
    c_f = c_c + 3 * W_C
    w_in = bf(w_in)
    kr = w_in[:, :, c_kr:c_kr + MLA_ROPE]
    kr_sw = jnp.concatenate([kr[:, :, half:], kr[:, :, :half]], axis=2)
    zpad = jnp.zeros((depth, d, LANES - MLA_ROPE), BF16)
    win_p = jnp.concatenate([w_in[:, :, :c_kr], kr, zpad, kr_sw, zpad, w_in[:, :, c_c:c_f], w_in[:, :, c_f:],
                             jnp.zeros((depth, d, LANES - H_C), BF16)], axis=2)
    assert win_p.shape[2] == IN_COLS_P
    bf_p = jnp.pad(b_f.astype(F32), ((0, 0), (0, LANES - H_C))).reshape(depth, 1, LANES)
    wkvt = jnp.transpose(jnp.concatenate([w_in[:, :, W_A:3 * W_A], w_in[:, :, c_c + W_C:c_f]], axis=2), (0, 2, 1))

    wq4 = bf(w_uq).reshape(depth, Q_LORA, H_B, MLA_NOPE + MLA_ROPE)
    wkv4 = bf(w_ukv).reshape(depth, KV_LORA, H_B, MLA_NOPE + MLA_V)
    wq_n = jnp.transpose(wq4[..., :MLA_NOPE], (0, 2, 1, 3))
    wk_n = jnp.transpose(wkv4[..., :MLA_NOPE], (0, 2, 1, 3))
    wcomb = _wcomb(wq_n, wk_n)
    x1 = wq4[..., MLA_NOPE:MLA_NOPE + half]
    x2 = wq4[..., MLA_NOPE + half:]
    zq = jnp.zeros((depth, Q_LORA, H_B, LANES - MLA_ROPE), BF16)
    wqr = jnp.concatenate([jnp.concatenate([x1, x2, zq], axis=3).reshape(depth, Q_LORA, H_B * LANES),
                           jnp.concatenate([x2, x1, zq], axis=3).reshape(depth, Q_LORA, H_B * LANES)], axis=2)
    wuvt = jnp.transpose(wkv4[..., MLA_NOPE:], (0, 2, 3, 1))

    return dict(
        ff1=(row(g_ff1_pre), row(g_ff1_post), bf(w_ff1_gu), bf(w_ff1_down)),
        inproj=(row(g_mix_pre), win_p, bf_p, row(g_bq), row(g_bkv), wcomb, wqr, wkvt),
        wuvt=wuvt,
        post=(row(g_grp), bf(w_out), row(g_mix_post),
              row(g_ff2_pre), row(g_ff2_post), bf(w_ff2_gu), bf(w_ff2_down),
              row(g_ple_pre), bf(w_ple_gate), bf(w_ple_proj), row(g_ple_post)),
    )


def _pad_keys(a, tk_pad):
    return jnp.pad(a, ((0, 0), (0, tk_pad - a.shape[1])) + ((0, 0),) * (a.ndim - 2))


def _forget_sums(logf_rows, tk_pad):
    _, heads, tk = logf_rows.shape
    return _cumsum_rows(jnp.pad(logf_rows, ((0, 0), (0, F_ROWS - heads), (0, tk_pad - tk))))


def _layer(h, p, seq_len, q_pos0, past, layer, lp, tables):
    n, d = h.shape
    b = n // seq_len
    h = _ffn(h, layer, *lp["ff1"])
    (ka_st, va_st, ckv_st, kr_st, kc_st, vc_st, lf_st,
     qa_b, ka_b, va_b, qm_b, kl_b, qc_b, kc_b, vc_b), feature_major = _inproj(h, seq_len, layer, *lp["inproj"], *tables)
    if feature_major:
        heads4 = lambda a, nh: jnp.transpose(a.reshape(b, nh, HEAD_DIM, seq_len), (0, 3, 1, 2))
        swap = lambda a: jnp.transpose(a, (0, 2, 1))
        state = (heads4(ka_st, H_A), heads4(va_st, H_A), ckv_st.reshape(b, seq_len, KV_LORA), swap(kr_st),
                 heads4(kc_st, H_C), heads4(vc_st, H_C), swap(lf_st))
        lf_rows_new = lf_st
    else:
        state = (ka_st.reshape(b, seq_len, H_A, HEAD_DIM), va_st.reshape(b, seq_len, H_A, HEAD_DIM),
                 ckv_st.reshape(b, seq_len, KV_LORA), kr_st.reshape(b, seq_len, MLA_ROPE),
                 kc_st.reshape(b, seq_len, H_C, HEAD_DIM), vc_st.reshape(b, seq_len, H_C, HEAD_DIM),
                 lf_st.reshape(b, seq_len, H_C))
        lf_rows_new = jnp.transpose(state[6], (0, 2, 1))
    seq3 = lambda a: a.reshape(b, seq_len, a.shape[-1])
    tq_pad = -(-seq_len // LANES) * LANES
    to_blocks = lambda v: jnp.transpose(v.reshape(b, v.shape[1] // KEY_BLOCK, KEY_BLOCK, v.shape[2]), (0, 1, 3, 2))
    if past is None:
        n_valid = seq_len
        tk_pad = -(-tq_pad // KEY_BLOCK) * KEY_BLOCK
        ka_all, kl_all, kc_all = (_pad_keys(seq3(a), tk_pad) for a in (ka_b, kl_b, kc_b))
        if feature_major:
            va_all, vc_all = (v.reshape(b, seq_len // KEY_BLOCK, v.shape[1], KEY_BLOCK) for v in (va_b, vc_b))
        else:
            va_all, vc_all = (to_blocks(_pad_keys(seq3(v), tk_pad)) for v in (va_b, vc_b))
        lf_rows = lf_rows_new
    else:
        assert not feature_major
        pa_k, pa_v, pb_ckv, pb_kr, pc_k, pc_v, pc_lf = past
        past_len = pa_k.shape[1]
        n_valid = past_len + seq_len
        tk_pad = -(-(past_len + tq_pad) // KEY_BLOCK) * KEY_BLOCK
        join = lambda c, new: _pad_keys(jnp.concatenate([c.reshape(b, past_len, -1).astype(BF16), seq3(new)], axis=1), tk_pad)
        ka_all, kc_all = join(pa_k, ka_b), join(pc_k, kc_b)
        va_all, vc_all = to_blocks(join(pa_v, va_b)), to_blocks(join(pc_v, vc_b))
        kl_past = jnp.concatenate([pb_ckv, pb_kr, jnp.zeros((b, past_len, LANES - MLA_ROPE), F32)], axis=-1)
        kl_all = join(kl_past, kl_b)
        lf_rows = jnp.concatenate([jnp.transpose(pc_lf, (0, 2, 1)), lf_rows_new], axis=2)
    f_rows = _forget_sums(lf_rows, tk_pad)
    f_q = f_rows[:, :, q_pos0:q_pos0 + tq_pad]
    f_k = jnp.transpose(f_rows, (0, 2, 1))
    pad_q = lambda a: jnp.pad(a, ((0, 0),) * (a.ndim - 2) + ((0, tq_pad - seq_len), (0, 0)))

    oa = _sb_attention(pad_q(seq3(qa_b)), ka_all, va_all, q_pos0, n_valid)[:, :seq_len]
    ob = _mla_attention(pad_q(qm_b.reshape(H_B, b, seq_len, 2 * LANES)), kl_all, lp["wuvt"], layer, tq_pad, q_pos0,
                        n_valid)[:, :seq_len]
    oc = _fox_attention(pad_q(seq3(qc_b)), kc_all, vc_all, f_q, f_k, q_pos0, n_valid)[:, :seq_len]
    flat = lambda a: a.reshape(n, a.shape[-1])
    h = _post(h, flat(oa), flat(ob), flat(oc), p, layer, lp["post"])
    return h, state


def _trunk(x, p, q_pos0, caches, lp, depth):
    b, t, d = x.shape
    tables = _rope_tables(q_pos0 + jnp.arange(t, dtype=jnp.int32))
    h = x.reshape(b * t, d)
    states = []
    for i in range(depth):
        past = None if caches is None else [c[i] for c in caches]
        h, st = _layer(h, p[i].reshape(b * t, -1), t, q_pos0, past, i, lp, tables)
        states.append(st)
    stacked = [jnp.stack([st[j] for st in states]) for j in range(len(states[0]))]
    return h.reshape(b, t, d), stacked


def kernel(x_prompt, x_sample, p_prompt, p_sample, cache_a_k, cache_a_v, cache_b_ckv, cache_b_krope, cache_c_k, cache_c_v, cache_c_logf, g_ff1_pre, g_ff1_post, w_ff1_gu, w_ff1_down, g_mix_pre, g_mix_post, w_in, b_f, g_bq, g_bkv, w_uq, w_ukv, g_grp, w_out, g_ff2_pre, g_ff2_post, w_ff2_gu, w_ff2_down, g_ple_pre, w_ple_gate, w_ple_proj, g_ple_post):
    weights = (g_ff1_pre, g_ff1_post, w_ff1_gu, w_ff1_down, g_mix_pre, g_mix_post, w_in, b_f,
               g_bq, g_bkv, w_uq, w_ukv, g_grp, w_out, g_ff2_pre, g_ff2_post, w_ff2_gu, w_ff2_down,
               g_ple_pre, w_ple_gate, w_ple_proj, g_ple_post)
    depth = w_in.shape[0]
    lp = _prep_weights(weights)
    y_prompt, sp = _trunk(x_prompt, p_prompt, 0, None, lp, depth)
    caches = (cache_a_k, cache_a_v, cache_b_ckv, cache_b_krope, cache_c_k, cache_c_v, cache_c_logf)
    y_sample, ss = _trunk(x_sample, p_sample, cache_a_k.shape[2], caches, lp, depth)
    return (y_prompt, y_sample, *sp, *ss)
```

```python
import functools
import math

import jax
import jax.numpy as jnp
from jax import lax
from jax.experimental import pallas as pl
from jax.experimental.pallas import tpu as pltpu

CHUNK = 64
HEAD_DIM = 64
H_A = 4
H_B = 8
H_C = 4
W_A = H_A * HEAD_DIM
MLA_NOPE = 64
MLA_ROPE = 32
MLA_V = 64
W_B = H_B * MLA_V
W_C = H_C * HEAD_DIM
Q_LORA = 256
KV_LORA = 128
ROPE_THETA = 10000.0
EPS = 1e-6
FFN_RES = 0.5
SB_SCALE = HEAD_DIM ** -0.5
MLA_SCALE = (MLA_NOPE + MLA_ROPE) ** -0.5
FOX_SCALE = HEAD_DIM ** -0.5
LOG2E = math.log2(math.e)

LANES = 128
KEY_BLOCK = 256
F_ROWS = 16
VMEM_LIMIT = 56 * 1024 * 1024

COL_A = 0
COL_CQ = COL_A + 3 * W_A
COL_CKV = COL_CQ + Q_LORA
COL_KRA = COL_CKV + KV_LORA
COL_KRB = COL_KRA + LANES
COL_C = COL_KRB + LANES
COL_F = COL_C + 3 * W_C
IN_COLS_P = COL_F + LANES

BF16 = jnp.bfloat16
F32 = jnp.float32


def _dot(a, b):
    return jnp.dot(a, b, preferred_element_type=F32)


def _dot_nt(a, b):
    return lax.dot_general(a, b, (((1,), (1,)), ((), ())), preferred_element_type=F32)


def _rms(x, g):
    ms = jnp.mean(x * x, axis=-1, keepdims=True)
    return x * lax.rsqrt(ms + EPS) * g


def _log_sigmoid(x):
    return jnp.minimum(x, 0.0) - jnp.log(1.0 + jnp.exp(-jnp.abs(x)))


def _softplus(x):
    return jnp.maximum(x, 0.0) + jnp.log(1.0 + jnp.exp2(jnp.abs(x) * (-LOG2E)))


def _split2(x):
    hi = x.astype(BF16)
    lo = (x - hi.astype(F32)).astype(BF16)
    return hi, lo


def _layer_spec(a, layer):
    idx = (layer,) + (0,) * (a.ndim - 1)
    return pl.BlockSpec((None,) + a.shape[1:], lambda *_: idx, pipeline_mode=pl.Buffered(1))


def _params(n_axes):
    return pltpu.CompilerParams(dimension_semantics=("arbitrary",) * n_axes, vmem_limit_bytes=VMEM_LIMIT)


def _row_tile(n):
    for tm in (512, 256):
        if n % tm == 0:
            return tm
    return n


def _ffn_rows(h, gpre_ref, gpost_ref, wgu_ref, wdown_ref, d_ff, chunks):
    xn = _rms(h, gpre_ref[...]).astype(BF16)
    acc = jnp.zeros(h.shape, F32)
    for c0, c1 in chunks:
        g = _dot(xn, wgu_ref[:, c0:c1])
        u = _dot(xn, wgu_ref[:, d_ff + c0:d_ff + c1])
        a = (g * jax.nn.sigmoid(g) * u).astype(BF16)
        acc = acc + _dot(a, wdown_ref[c0:c1, :])
    return h + FFN_RES * _rms(acc, gpost_ref[...])


def _ffn_chunks(d_ff):
    step = 4 * KEY_BLOCK
    return tuple((c, min(c + step, d_ff)) for c in range(0, d_ff, step))


def _ffn_kernel(h_ref, gpre_ref, gpost_ref, wgu_ref, wdown_ref, o_ref, *, d_ff, chunks):
    o_ref[...] = _ffn_rows(h_ref[...], gpre_ref, gpost_ref, wgu_ref, wdown_ref, d_ff, chunks)


def _ffn(h, layer, gpre, gpost, wgu, wdown):
    n, d = h.shape
    d_ff = wdown.shape[1]
    consts = [gpre, gpost, wgu, wdown]
    tm = _row_tile(n)
    chunks = _ffn_chunks(d_ff)
    row = pl.BlockSpec((tm, d), lambda i: (i, 0))
    return pl.pallas_call(
        functools.partial(_ffn_kernel, d_ff=d_ff, chunks=chunks),
        grid=(n // tm,),
        in_specs=[row] + [_layer_spec(c, layer) for c in consts],
        out_specs=row,
        out_shape=jax.ShapeDtypeStruct((n, d), F32),
        compiler_params=_params(1),
        name="ffn",
    )(h, gpre, gpost, wgu, wdown)


def _wcomb_kernel(wq_ref, wk_ref, o_ref):
    o_ref[...] = _dot_nt(wq_ref[...], wk_ref[...]).astype(BF16)


def _wcomb(wq_n, wk_n):
    depth, nh, ql, dn = wq_n.shape
    kl = wk_n.shape[2]
    return pl.pallas_call(
        _wcomb_kernel,
        grid=(depth, nh),
        in_specs=[pl.BlockSpec((None, None, ql, dn), lambda i, h: (i, h, 0, 0)),
                  pl.BlockSpec((None, None, kl, dn), lambda i, h: (i, h, 0, 0))],
        out_specs=pl.BlockSpec((None, ql, kl), lambda i, h: (i, 0, h)),
        out_shape=jax.ShapeDtypeStruct((depth, ql, nh * kl), BF16),
        compiler_params=_params(2),
        name="wcomb",
    )(wq_n, wk_n)


def _inproj_kernel(*refs, feature_major, n_alias):
    (h_ref, gpre_ref, win_ref, bf_ref, gbq_ref, gbkv_ref, wcomb_ref, wqr_ref, wkvt_ref, cos_ref, sin_ref) = refs[:11]
    (ka_st, va_st, ckv_st, kr_st, kc_st, vc_st, lf_st,
     qa_b, ka_b, va_b, qm_b, kl_b, qc_b, kc_b, vc_b) = refs[11 + n_alias:]
    xn = _rms(h_ref[...], gpre_ref[...]).astype(BF16)
    proj = _dot(xn, win_ref[...])
    cos = cos_ref[...]
    sin = sin_ref[...]

    qa_b[...] = (proj[:, COL_A:COL_A + W_A] * SB_SCALE).astype(BF16)
    qc_b[...] = (proj[:, COL_C:COL_C + W_C] * (FOX_SCALE * LOG2E)).astype(BF16)
    ka = proj[:, COL_A + W_A:COL_A + 2 * W_A]
    kc = proj[:, COL_C + W_C:COL_C + 2 * W_C]
    ka_b[...] = ka.astype(BF16)
    kc_b[...] = kc.astype(BF16)
    lf = _log_sigmoid(proj[:, COL_F:COL_F + LANES] + bf_ref[...])
    ckv = _rms(proj[:, COL_CKV:COL_CKV + KV_LORA], gbkv_ref[...])
    ckv_st[...] = ckv
    kr = proj[:, COL_KRA:COL_KRA + LANES] * cos + proj[:, COL_KRB:COL_KRB + LANES] * sin
    kl_b[:, :KV_LORA] = ckv.astype(BF16)
    kl_b[:, KV_LORA:] = kr.astype(BF16)

    if feature_major:
        kvt = _dot_nt(wkvt_ref[...], xn)
        for idx, st in enumerate((ka_st, va_st, kc_st, vc_st)):
            st[...] = kvt[idx * W_A:(idx + 1) * W_A]
        for r in range(va_b.shape[0]):
            cols = slice(r * KEY_BLOCK, (r + 1) * KEY_BLOCK)
            va_b[r] = kvt[W_A:2 * W_A, cols].astype(BF16)
            vc_b[r] = kvt[3 * W_A:4 * W_A, cols].astype(BF16)
        kr_st[...] = kr.T[:MLA_ROPE]
        lf_st[...] = lf.T[:H_C]
    else:
        va = proj[:, COL_A + 2 * W_A:COL_A + 3 * W_A]
        vc = proj[:, COL_C + 2 * W_C:COL_C + 3 * W_C]
        ka_st[...] = ka
        va_st[...] = va
        kc_st[...] = kc
        vc_st[...] = vc
        va_b[...] = va.astype(BF16)
        vc_b[...] = vc.astype(BF16)
        kr_st[...] = kr[:, :MLA_ROPE]
        lf_st[...] = lf[:, :H_C]

    cqn = _rms(proj[:, COL_CQ:COL_CQ + Q_LORA], gbq_ref[...]).astype(BF16)
    qlat = _dot(cqn, wcomb_ref[...])
    qr = _dot(cqn, wqr_ref[...])
    half = H_B * LANES
    for hh in range(H_B):
        sl = slice(hh * LANES, (hh + 1) * LANES)
        rope = qr[:, sl] * cos + qr[:, half + hh * LANES:half + (hh + 1) * LANES] * sin
        qm_b[hh, :, :KV_LORA] = (qlat[:, sl] * (MLA_SCALE * LOG2E)).astype(BF16)
        qm_b[hh, :, KV_LORA:] = (rope * (MLA_SCALE * LOG2E)).astype(BF16)


N_STATE = 7


def _inproj(h, seq_len, layer, prev_states, gpre, win, bf, gbq, gbkv, wcomb, wqr, wkvt, cos_t, sin_t):
    n, d = h.shape
    b = n // seq_len
    depth = win.shape[0]
    tm = _row_tile(n)
    row = lambda w: pl.BlockSpec((tm, w), lambda i: (i, 0))
    consts = [gpre, win, bf, gbq, gbkv, wcomb, wqr, wkvt]
    feature_major = seq_len % tm == 0 and tm % KEY_BLOCK == 0
    if feature_major:
        per_seq = seq_len // tm
        tab = pl.BlockSpec((tm, LANES), lambda i: (i % per_seq, 0))
        narrow = lambda w: (pl.BlockSpec((None, None, w, tm), lambda i: (layer, i // per_seq, 0, i % per_seq)),
                            (depth, b, w, seq_len), F32)
        latent = (pl.BlockSpec((None, tm, KV_LORA), lambda i: (layer, i, 0)), (depth, n, KV_LORA), F32)
        vals = lambda w: (pl.BlockSpec((tm // KEY_BLOCK, w, KEY_BLOCK), lambda i: (i, 0, 0)),
                          (n // KEY_BLOCK, w, KEY_BLOCK), BF16)
    else:
        cos_t, sin_t = (jnp.tile(t, (b, 1)) for t in (cos_t, sin_t))
        tab = row(LANES)
        narrow = lambda w: (row(w), (n, w), F32)
        latent = narrow(KV_LORA)
        vals = lambda w: (row(w), (n, w), BF16)
    outs = [
        narrow(W_A), narrow(W_A), latent, narrow(MLA_ROPE), narrow(W_C), narrow(W_C), narrow(H_C),
        (row(W_A), (n, W_A), BF16), (row(W_A), (n, W_A), BF16), vals(W_A),
        (pl.BlockSpec((H_B, tm, 2 * LANES), lambda i: (0, i, 0)), (H_B, n, 2 * LANES), BF16),
        (row(2 * LANES), (n, 2 * LANES), BF16),
        (row(W_C), (n, W_C), BF16), (row(W_C), (n, W_C), BF16), vals(W_C),
    ]
    ins = [h, *consts, cos_t, sin_t]
    in_specs = [row(d)] + [_layer_spec(c, layer) for c in consts] + [tab, tab]
    aliases = {}
    if feature_major:
        if prev_states is None:
            prev_states = [jnp.zeros(o[1], o[2]) for o in outs[:N_STATE]]
        aliases = {len(ins) + s: s for s in range(N_STATE)}
        ins = ins + list(prev_states)
        in_specs = in_specs + [pl.BlockSpec(memory_space=pl.ANY)] * N_STATE
    res = pl.pallas_call(
        functools.partial(_inproj_kernel, feature_major=feature_major, n_alias=len(aliases)),
        grid=(n // tm,),
        in_specs=in_specs,
        out_specs=[o[0] for o in outs],
        out_shape=[jax.ShapeDtypeStruct(o[1], o[2]) for o in outs],
        input_output_aliases=aliases,
        compiler_params=_params(1),
        name="inproj",
    )(*ins)
    return res, feature_major


def _cumsum_kernel(x_ref, o_ref, *, n_blocks):
    r = lax.broadcasted_iota(jnp.int32, (KEY_BLOCK, KEY_BLOCK), 0)
    c = lax.broadcasted_iota(jnp.int32, (KEY_BLOCK, KEY_BLOCK), 1)
    upper = jnp.where(r <= c, 1.0, 0.0).astype(BF16)
    carry = jnp.zeros((x_ref.shape[0], 1), F32)
    for j in range(n_blocks):
        sl = slice(j * KEY_BLOCK, (j + 1) * KEY_BLOCK)
        x = x_ref[:, sl]
        hi = x.astype(BF16)
        mid, lo = _split2(x - hi.astype(F32))
        f = _dot(hi, upper) + _dot(mid, upper) + _dot(lo, upper) + carry
        o_ref[:, sl] = f * LOG2E
        carry = f[:, KEY_BLOCK - 1:KEY_BLOCK]


def _cumsum_rows(x):
    b, r, tp = x.shape
    spec = pl.BlockSpec((None, r, tp), lambda i: (i, 0, 0))
    return pl.pallas_call(
        functools.partial(_cumsum_kernel, n_blocks=tp // KEY_BLOCK),
        grid=(b,),
        in_specs=[spec],
        out_specs=spec,
        out_shape=jax.ShapeDtypeStruct(x.shape, F32),
        compiler_params=_params(1),
        name="cumsum_logf",
    )(x)


def _block_range(q_start, tq, n_valid):
    n_full = q_start // KEY_BLOCK
    last = jnp.minimum(((q_start + tq + CHUNK - 1) // CHUNK) * CHUNK, n_valid)
    n_total = (last + KEY_BLOCK - 1) // KEY_BLOCK
    return n_full, n_total


def _positions(q_start, tq, j, cols):
    k_pos = j * KEY_BLOCK + lax.broadcasted_iota(jnp.int32, (KEY_BLOCK, cols), 0)
    lane = lax.broadcasted_iota(jnp.int32, (KEY_BLOCK, cols), 1)
    q_pos = q_start + (lane if cols == tq else lane % tq)
    return q_pos, k_pos


def _transpose_blocks(src_ref, dst_ref, width):
    for j in range(dst_ref.shape[0]):
        blk = src_ref[j * KEY_BLOCK:(j + 1) * KEY_BLOCK, :width].astype(F32)
        dst_ref[j] = blk.T.astype(BF16)


def _pipelined_blocks(n_full, produce, consume, store, load):
    f = lambda i: n_full - 1 - i
    n_pairs = jnp.maximum(n_full - 1, 0) // 2
    rest = n_full - 2 * n_pairs

    def both(i, slot):
        consume(f(i), load(slot), overlap=(lambda: produce(f(i + 1)), lambda vals: store(1 - slot, vals)))

    @pl.when(n_full > 0)
    def _():
        store(0, produce(f(0)))

    def pair_body(t, carry):
        both(2 * t, 0)
        both(2 * t + 1, 1)
        return carry

    lax.fori_loop(0, n_pairs, pair_body, 0)

    @pl.when((n_full > 0) & (rest == 1))
    def _():
        consume(f(2 * n_pairs), load(0))

    @pl.when((n_full > 0) & (rest == 2))
    def _():
        both(2 * n_pairs, 0)
        consume(f(2 * n_pairs + 1), load(1))


def _finite_or_zero(m):
    return jnp.where(m == -jnp.inf, 0.0, m)


def _head_masks(n_heads):
    lane = lax.broadcasted_iota(jnp.int32, (1, n_heads * HEAD_DIM), 1)
    return [(lane >= hh * HEAD_DIM) & (lane < (hh + 1) * HEAD_DIM) for hh in range(n_heads)]


def _sb_kernel(q_ref, k_ref, vt_ref, o_ref, qm_ref, z_ref, x_ref, acc_ref, c_ref, *, tq, q_pos0, n_valid):
    q_start = q_pos0 + pl.program_id(1) * tq
    n_full, n_total = _block_range(q_start, tq, n_valid)
    hmask = _head_masks(H_A)
    q = q_ref[...]
    for hh in range(H_A):
        qm_ref[hh] = jnp.where(hmask[hh], q, jnp.zeros_like(q))
    acc_ref[...] = jnp.zeros(acc_ref.shape, F32)
    c_ref[...] = jnp.zeros(c_ref.shape, F32)
    half = KEY_BLOCK // 2
    r = lax.broadcasted_iota(jnp.int32, (half, KEY_BLOCK), 0)
    c = lax.broadcasted_iota(jnp.int32, (half, KEY_BLOCK), 1)
    neg_upper2 = jnp.where(c % half >= r, -1.0, 0.0).astype(BF16)
    heads = range(H_A)
    rows = [slice(hh * HEAD_DIM, (hh + 1) * HEAD_DIM) for hh in heads]

    def parts(x):
        hi, lo = _split2(x)
        return jnp.concatenate([hi, lo], axis=0)

    def first_stage(j, mask=None):
        off = pl.multiple_of(j * KEY_BLOCK, KEY_BLOCK)
        kb = k_ref[pl.ds(off, KEY_BLOCK), :]
        zs = [_dot_nt(kb, qm_ref[hh]) for hh in heads]
        sps = [_softplus(z) for z in zs]
        if mask is not None:
            sps = [jnp.where(mask, sp, 0.0) for sp in sps]
        return zs, [parts(sp[:half]) for sp in sps], [parts(sp[half:]) for sp in sps]

    def second_stage(j, vals, overlap=None, mask=None):
        zs, early, late = vals
        vts = [vt_ref[j, rows[hh], :] for hh in heads]
        carry = c_ref[...]
        upcoming = overlap[0]() if overlap else None
        cum_late = [_dot(neg_upper2, late[hh]) + carry[hh:hh + 1, :] for hh in heads]
        cum_early = [_dot(neg_upper2, early[hh]) + cum_late[hh][0:1, :] for hh in heads]
        ws = [jnp.exp(zs[hh] + jnp.concatenate([cum_early[hh], cum_late[hh]], axis=0)) for hh in heads]
        if mask is not None:
            ws = [jnp.where(mask, w, 0.0) for w in ws]
        pvs = [_dot(vts[hh], ws[hh].astype(BF16)) for hh in heads]
        for hh in heads:
            acc_ref[rows[hh], :] += pvs[hh]
            c_ref[hh:hh + 1, :] = cum_early[hh][0:1, :]
        if overlap:
            overlap[1](upcoming)

    def store(slot, vals):
        zs, early, late = vals
        for hh in heads:
            z_ref[slot, hh] = zs[hh]
            x_ref[slot, hh, 0] = early[hh]
            x_ref[slot, hh, 1] = late[hh]

    def load(slot):
        return ([z_ref[slot, hh] for hh in heads], [x_ref[slot, hh, 0] for hh in heads],
                [x_ref[slot, hh, 1] for hh in heads])

    def masked_body(i, carry):
        j = n_total - 1 - i
        q_pos, k_pos = _positions(q_start, tq, j, tq)
        mask = k_pos < q_pos
        second_stage(j, first_stage(j, mask), mask=mask)
        return carry

    lax.fori_loop(0, n_total - n_full, masked_body, 0)
    _pipelined_blocks(n_full, first_stage, second_stage, store, load)
    o_ref[...] = acc_ref[...].T


def _sb_attention(q, k, vt, q_pos0, n_valid):
    b, tq_all, w = q.shape
    tk = k.shape[1]
    tq = min(tq_all, KEY_BLOCK)
    qspec = pl.BlockSpec((None, tq, w), lambda bi, qi: (bi, qi, 0))
    kspec = pl.BlockSpec((None, tk, w), lambda bi, qi: (bi, 0, 0))
    vspec = pl.BlockSpec((None,) + vt.shape[1:], lambda bi, qi: (bi, 0, 0, 0))
    return pl.pallas_call(
        functools.partial(_sb_kernel, tq=tq, q_pos0=q_pos0, n_valid=n_valid),
        grid=(b, tq_all // tq),
        in_specs=[qspec, kspec, vspec],
        out_specs=qspec,
        out_shape=jax.ShapeDtypeStruct((b, tq_all, w), F32),
        scratch_shapes=[pltpu.VMEM((H_A, tq, w), BF16), pltpu.VMEM((2, H_A, KEY_BLOCK, tq), F32),
                        pltpu.VMEM((2, H_A, 2, KEY_BLOCK, tq), BF16), pltpu.VMEM((w, tq), F32), pltpu.VMEM((8, tq), F32)],
        compiler_params=_params(2),
        name="sb_attention",
    )(q, k, vt)


def _fox_kernel(q_ref, k_ref, vt_ref, fq_ref, fk_ref, o_ref, qm_ref, z_ref, acc_ref, m_ref, l_ref, *, tq, q_pos0, n_valid):
    q_start = q_pos0 + pl.program_id(1) * tq
    n_full, n_total = _block_range(q_start, tq, n_valid)
    hmask = _head_masks(H_C)
    q = q_ref[...]
    for hh in range(H_C):
        qm_ref[hh] = jnp.where(hmask[hh], q, jnp.zeros_like(q))
    acc_ref[...] = jnp.zeros(acc_ref.shape, F32)
    m_ref[...] = jnp.full(m_ref.shape, -jnp.inf, F32)
    l_ref[...] = jnp.zeros(l_ref.shape, F32)

    heads = range(H_C)
    rows = [slice(hh * HEAD_DIM, (hh + 1) * HEAD_DIM) for hh in heads]

    def scores(j):
        off = pl.multiple_of(j * KEY_BLOCK, KEY_BLOCK)
        kb = k_ref[pl.ds(off, KEY_BLOCK), :]
        return [_dot_nt(kb, qm_ref[hh]) for hh in heads]

    def step(j, qk, overlap=None, mask=None):
        off = pl.multiple_of(j * KEY_BLOCK, KEY_BLOCK)
        fk = fk_ref[pl.ds(off, KEY_BLOCK), :]
        vts = [vt_ref[j, rows[hh], :] for hh in heads]
        fq, m_all, l_all = fq_ref[...], m_ref[...], l_ref[...]
        accs = [acc_ref[rows[hh], :] for hh in heads]
        upcoming = overlap[0]() if overlap else None
        zs = [qk[hh] + (fq[hh:hh + 1, :] - fk[:, hh:hh + 1]) for hh in heads]
        if mask is not None:
            zs = [jnp.where(mask, z, -jnp.inf) for z in zs]
        m_new = [jnp.maximum(m_all[hh:hh + 1, :], jnp.max(zs[hh], axis=0, keepdims=True)) for hh in heads]
        m_use = [_finite_or_zero(m) for m in m_new]
        ps = [jnp.exp2(zs[hh] - m_use[hh]) for hh in heads]
        alpha = [jnp.exp2(m_all[hh:hh + 1, :] - m_use[hh]) for hh in heads]
        l_new = [alpha[hh] * l_all[hh:hh + 1, :] + jnp.sum(ps[hh], axis=0, keepdims=True) for hh in heads]
        pvs = [_dot(vts[hh], ps[hh].astype(BF16)) for hh in heads]
        for hh in heads:
            m_ref[hh:hh + 1, :] = m_new[hh]
            l_ref[hh:hh + 1, :] = l_new[hh]
            acc_ref[rows[hh], :] = accs[hh] * alpha[hh] + pvs[hh]
        if overlap:
            overlap[1](upcoming)

    def store(slot, qk):
        for hh in heads:
            z_ref[slot, hh] = qk[hh]

    def load(slot):
        return [z_ref[slot, hh] for hh in heads]

    def masked_body(i, carry):
        j = n_total - 1 - i
        q_pos, k_pos = _positions(q_start, tq, j, tq)
        step(j, scores(j), mask=k_pos <= q_pos)
        return carry

    lax.fori_loop(0, n_total - n_full, masked_body, 0)
    _pipelined_blocks(n_full, scores, step, store, load)
    for hh in range(H_C):
        rows = slice(hh * HEAD_DIM, (hh + 1) * HEAD_DIM)
        acc_ref[rows, :] = acc_ref[rows, :] * (1.0 / l_ref[hh:hh + 1, :])
    o_ref[...] = acc_ref[...].T


def _fox_attention(q, k, vt, fq, fk, q_pos0, n_valid):
    b, tq_all, w = q.shape
    tk = k.shape[1]
    tq = min(tq_all, KEY_BLOCK)
    qspec = pl.BlockSpec((None, tq, w), lambda bi, qi: (bi, qi, 0))
    kspec = pl.BlockSpec((None, tk, w), lambda bi, qi: (bi, 0, 0))
    vspec = pl.BlockSpec((None,) + vt.shape[1:], lambda bi, qi: (bi, 0, 0, 0))
    fqspec = pl.BlockSpec((None, F_ROWS, tq), lambda bi, qi: (bi, 0, qi))
    fkspec = pl.BlockSpec((None, tk, F_ROWS), lambda bi, qi: (bi, 0, 0))
    return pl.pallas_call(
        functools.partial(_fox_kernel, tq=tq, q_pos0=q_pos0, n_valid=n_valid),
        grid=(b, tq_all // tq),
        in_specs=[qspec, kspec, vspec, fqspec, fkspec],
        out_specs=qspec,
        out_shape=jax.ShapeDtypeStruct((b, tq_all, w), F32),
        scratch_shapes=[pltpu.VMEM((H_C, tq, w), BF16), pltpu.VMEM((2, H_C, KEY_BLOCK, tq), F32),
                        pltpu.VMEM((w, tq), F32), pltpu.VMEM((8, tq), F32), pltpu.VMEM((8, tq), F32)],
        compiler_params=_params(2),
        name="fox_attention",
    )(q, k, vt, fq, fk)


def _mla_kernel(q_ref, kl_ref, wuvt_ref, o_ref, ct_ref, z_ref, acc_ref, m_ref, l_ref, *, tq, q_pos0, n_valid):
    @pl.when(pl.program_id(1) == 0)
    def _():
        _transpose_blocks(kl_ref, ct_ref, KV_LORA)

    q_start = q_pos0 + pl.program_id(1) * tq
    n_full, n_total = _block_range(q_start, tq, n_valid)
    cols = H_B * tq
    qs = q_ref[...].reshape(cols, 2 * LANES)
    acc_ref[...] = jnp.zeros(acc_ref.shape, F32)
    m_ref[...] = jnp.full(m_ref.shape, -jnp.inf, F32)
    l_ref[...] = jnp.zeros(l_ref.shape, F32)

    group = KEY_BLOCK
    groups = [slice(g * group, (g + 1) * group) for g in range(cols // group)]

    gs = range(len(groups))

    def scores(j):
        off = pl.multiple_of(j * KEY_BLOCK, KEY_BLOCK)
        kb = kl_ref[pl.ds(off, KEY_BLOCK), :]
        return [_dot_nt(kb, qs[g]) for g in groups]

    def step(j, zs, overlap=None):
        ct = ct_ref[j]
        m_all, l_all = m_ref[...], l_ref[...]
        accs = [acc_ref[:, g] for g in groups]
        upcoming = overlap[0]() if overlap else None
        m_new = [jnp.maximum(m_all[:, groups[gi]], jnp.max(zs[gi], axis=0, keepdims=True)) for gi in gs]
        m_use = [_finite_or_zero(m) for m in m_new]
        ps = [jnp.exp2(zs[gi] - m_use[gi]) for gi in gs]
        alpha = [jnp.exp2(m_all[:, groups[gi]] - m_use[gi]) for gi in gs]
        l_new = [alpha[gi] * l_all[:, groups[gi]] + jnp.sum(ps[gi], axis=0, keepdims=True) for gi in gs]
        pvs = [_dot(ct, ps[gi].astype(BF16)) for gi in gs]
        for gi, g in enumerate(groups):
            m_ref[:, g] = m_new[gi]
            l_ref[:, g] = l_new[gi]
            acc_ref[:, g] = accs[gi] * alpha[gi] + pvs[gi]
        if overlap:
            overlap[1](upcoming)

    def store(slot, zs):
        for gi, g in enumerate(groups):
            z_ref[slot, :, g] = zs[gi]

    def load(slot):
        return [z_ref[slot, :, g] for g in groups]

    def masked_body(i, carry):
        j = n_total - 1 - i
        q_pos, k_pos = _positions(q_start, tq, j, group)
        mask = (k_pos // CHUNK <= q_pos // CHUNK) & (k_pos < n_valid)
        step(j, [jnp.where(mask, z, -jnp.inf) for z in scores(j)])
        return carry

    lax.fori_loop(0, n_total - n_full, masked_body, 0)

    _pipelined_blocks(n_full, scores, step, store, load)

    lat = (acc_ref[...] * (1.0 / l_ref[...])).astype(BF16)
    heads = [_dot(wuvt_ref[hh], lat[:, hh * tq:(hh + 1) * tq]) for hh in range(H_B)]
    o_ref[...] = jnp.concatenate(heads, axis=0).T


def _mla_attention(qm, kl, wuvt, layer, tq_all, q_pos0, n_valid):
    b, tk, w = kl.shape
    tq = KEY_BLOCK if tq_all % KEY_BLOCK == 0 else LANES
    cols = H_B * tq
    return pl.pallas_call(
        functools.partial(_mla_kernel, tq=tq, q_pos0=q_pos0, n_valid=n_valid),
        grid=(b, tq_all // tq),
        in_specs=[pl.BlockSpec((H_B, None, tq, w), lambda bi, qi: (0, bi, qi, 0)),
                  pl.BlockSpec((None, tk, w), lambda bi, qi: (bi, 0, 0)),
                  _layer_spec(wuvt, layer)],
        out_specs=pl.BlockSpec((None, tq, W_B), lambda bi, qi: (bi, qi, 0)),
        out_shape=jax.ShapeDtypeStruct((b, tq_all, W_B), F32),
        scratch_shapes=[pltpu.VMEM((tk // KEY_BLOCK, KV_LORA, KEY_BLOCK), BF16), pltpu.VMEM((2, KEY_BLOCK, cols), F32),
                        pltpu.VMEM((KV_LORA, cols), F32), pltpu.VMEM((1, cols), F32), pltpu.VMEM((1, cols), F32)],
        compiler_params=_params(2),
        name="mla_attention",
    )(qm, kl, wuvt)


def _post_kernel(h_ref, oa_ref, ob_ref, oc_ref, p_ref, ggrp_ref, wout_ref, gmix_ref,
                 gpre_ref, gpost_ref, wgu_ref, wdown_ref, gple_pre_ref, wgate_ref, wproj_ref, gple_post_ref, o_ref,
                 *, d_ff, chunks):
    m = jnp.zeros(h_ref.shape, F32)
    c0 = 0
    for o_grp in (oa_ref, ob_ref, oc_ref):
        c1 = c0 + o_grp.shape[1]
        m = m + _dot(_rms(o_grp[...], ggrp_ref[:, c0:c1]).astype(BF16), wout_ref[c0:c1, :])
        c0 = c1
    h = h_ref[...] + _rms(m, gmix_ref[...])
    h = _ffn_rows(h, gpre_ref, gpost_ref, wgu_ref, wdown_ref, d_ff, chunks)
    gate = jax.nn.sigmoid(_dot(_rms(h, gple_pre_ref[...]).astype(BF16), wgate_ref[...]))
    e = _dot(p_ref[...].astype(BF16), wproj_ref[...]) * gate
    o_ref[...] = h + _rms(e, gple_post_ref[...])


def _post(h, oa, ob, oc, p, layer, consts):
    n, d = h.shape
    tm = _row_tile(n)
    row = lambda w: pl.BlockSpec((tm, w), lambda i: (i, 0))
    p_spec = pl.BlockSpec((None, tm, p.shape[2]), lambda i: (layer, i, 0))
    d_ff = consts[6].shape[1]
    return pl.pallas_call(
        functools.partial(_post_kernel, d_ff=d_ff, chunks=_ffn_chunks(d_ff)),
        grid=(n // tm,),
        in_specs=[row(d), row(W_A), row(W_B), row(W_C), p_spec] + [_layer_spec(c, layer) for c in consts],
        out_specs=row(d),
        out_shape=jax.ShapeDtypeStruct((n, d), F32),
        compiler_params=_params(1),
        name="post",
    )(h, oa, ob, oc, p, *consts)


def _rope_tables(pos):
    half = MLA_ROPE // 2
    inv = ROPE_THETA ** (-jnp.arange(half, dtype=F32) / half)
    ang = pos.astype(F32)[:, None] * inv[None, :]
    cos, sin = jnp.cos(ang), jnp.sin(ang)
    pad = jnp.zeros((pos.shape[0], LANES - MLA_ROPE), F32)
    return jnp.concatenate([cos, cos, pad], axis=1), jnp.concatenate([-sin, sin, pad], axis=1)


def _prep_weights(weights):
    (g_ff1_pre, g_ff1_post, w_ff1_gu, w_ff1_down, g_mix_pre, g_mix_post, w_in, b_f, g_bq, g_bkv, w_uq, w_ukv,
     g_grp, w_out, g_ff2_pre, g_ff2_post, w_ff2_gu, w_ff2_down, g_ple_pre, w_ple_gate, w_ple_proj, g_ple_post) = weights
    depth, d = w_in.shape[:2]
    half = MLA_ROPE // 2
    row = lambda g: g.reshape(depth, 1, -1).astype(F32)
    bf = lambda w: w.astype(BF16)
    c_kr = 3 * W_A + Q_LORA + KV_LORA
    c_c = c_kr + MLA_ROPE
    c_f = c_c + 3 * W_C
    w_in = bf(w_in)
    kr = w_in[:, :, c_kr:c_kr + MLA_ROPE]
    kr_sw = jnp.concatenate([kr[:, :, half:], kr[:, :, :half]], axis=2)
    zpad = jnp.zeros((depth, d, LANES - MLA_ROPE), BF16)
    win_p = jnp.concatenate([w_in[:, :, :c_kr], kr, zpad, kr_sw, zpad, w_in[:, :, c_c:c_f], w_in[:, :, c_f:],
                             jnp.zeros((depth, d, LANES - H_C), BF16)], axis=2)
    assert win_p.shape[2] == IN_COLS_P
    bf_p = jnp.pad(b_f.astype(F32), ((0, 0), (0, LANES - H_C))).reshape(depth, 1, LANES)
    wkvt = jnp.transpose(jnp.concatenate([w_in[:, :, W_A:3 * W_A], w_in[:, :, c_c + W_C:c_f]], axis=2), (0, 2, 1))

    wq4 = bf(w_uq).reshape(depth, Q_LORA, H_B, MLA_NOPE + MLA_ROPE)
    wkv4 = bf(w_ukv).reshape(depth, KV_LORA, H_B, MLA_NOPE + MLA_V)
    wq_n = jnp.transpose(wq4[..., :MLA_NOPE], (0, 2, 1, 3))
    wk_n = jnp.transpose(wkv4[..., :MLA_NOPE], (0, 2, 1, 3))
    wcomb = _wcomb(wq_n, wk_n)
    x1 = wq4[..., MLA_NOPE:MLA_NOPE + half]
    x2 = wq4[..., MLA_NOPE + half:]
    zq = jnp.zeros((depth, Q_LORA, H_B, LANES - MLA_ROPE), BF16)
    wqr = jnp.concatenate([jnp.concatenate([x1, x2, zq], axis=3).reshape(depth, Q_LORA, H_B * LANES),
                           jnp.concatenate([x2, x1, zq], axis=3).reshape(depth, Q_LORA, H_B * LANES)], axis=2)
    wuvt = jnp.transpose(wkv4[..., MLA_NOPE:], (0, 2, 3, 1))

    return dict(
        ff1=(row(g_ff1_pre), row(g_ff1_post), bf(w_ff1_gu), bf(w_ff1_down)),
        inproj=(row(g_mix_pre), win_p, bf_p, row(g_bq), row(g_bkv), wcomb, wqr, wkvt),
        wuvt=wuvt,
        post=(row(g_grp), bf(w_out), row(g_mix_post),
              row(g_ff2_pre), row(g_ff2_post), bf(w_ff2_gu), bf(w_ff2_down),
              row(g_ple_pre), bf(w_ple_gate), bf(w_ple_proj), row(g_ple_post)),
    )


def _pad_keys(a, tk_pad):
    return jnp.pad(a, ((0, 0), (0, tk_pad - a.shape[1])) + ((0, 0),) * (a.ndim - 2))


def _forget_sums(logf_rows, tk_pad):
    _, heads, tk = logf_rows.shape
    return _cumsum_rows(jnp.pad(logf_rows, ((0, 0), (0, F_ROWS - heads), (0, tk_pad - tk))))


def _layer(h, p, seq_len, q_pos0, past, layer, lp, tables, prev_states):
    n, d = h.shape
    b = n // seq_len
    h = _ffn(h, layer, *lp["ff1"])
    (ka_st, va_st, ckv_st, kr_st, kc_st, vc_st, lf_st,
     qa_b, ka_b, va_b, qm_b, kl_b, qc_b, kc_b, vc_b), feature_major = _inproj(h, seq_len, layer, prev_states,
                                                                                *lp["inproj"], *tables)
    if feature_major:
        state = (ka_st, va_st, ckv_st, kr_st, kc_st, vc_st, lf_st)
        lf_rows_new = lf_st[layer]
    else:
        state = (ka_st.reshape(b, seq_len, H_A, HEAD_DIM), va_st.reshape(b, seq_len, H_A, HEAD_DIM),
                 ckv_st.reshape(b, seq_len, KV_LORA), kr_st.reshape(b, seq_len, MLA_ROPE),
                 kc_st.reshape(b, seq_len, H_C, HEAD_DIM), vc_st.reshape(b, seq_len, H_C, HEAD_DIM),
                 lf_st.reshape(b, seq_len, H_C))
        lf_rows_new = jnp.transpose(state[6], (0, 2, 1))
    seq3 = lambda a: a.reshape(b, seq_len, a.shape[-1])
    tq_pad = -(-seq_len // LANES) * LANES
    to_blocks = lambda v: jnp.transpose(v.reshape(b, v.shape[1] // KEY_BLOCK, KEY_BLOCK, v.shape[2]), (0, 1, 3, 2))
    if past is None:
        n_valid = seq_len
        tk_pad = -(-tq_pad // KEY_BLOCK) * KEY_BLOCK
        ka_all, kl_all, kc_all = (_pad_keys(seq3(a), tk_pad) for a in (ka_b, kl_b, kc_b))
        if feature_major:
            va_all, vc_all = (v.reshape(b, seq_len // KEY_BLOCK, v.shape[1], KEY_BLOCK) for v in (va_b, vc_b))
        else:
            va_all, vc_all = (to_blocks(_pad_keys(seq3(v), tk_pad)) for v in (va_b, vc_b))
        lf_rows = lf_rows_new
    else:
        assert not feature_major
        pa_k, pa_v, pb_ckv, pb_kr, pc_k, pc_v, pc_lf = past
        past_len = pa_k.shape[1]
        n_valid = past_len + seq_len
        tk_pad = -(-(past_len + tq_pad) // KEY_BLOCK) * KEY_BLOCK
        join = lambda c, new: _pad_keys(jnp.concatenate([c.reshape(b, past_len, -1).astype(BF16), seq3(new)], axis=1), tk_pad)
        ka_all, kc_all = join(pa_k, ka_b), join(pc_k, kc_b)
        va_all, vc_all = to_blocks(join(pa_v, va_b)), to_blocks(join(pc_v, vc_b))
        kl_past = jnp.concatenate([pb_ckv, pb_kr, jnp.zeros((b, past_len, LANES - MLA_ROPE), F32)], axis=-1)
        kl_all = join(kl_past, kl_b)
        lf_rows = jnp.concatenate([jnp.transpose(pc_lf, (0, 2, 1)), lf_rows_new], axis=2)
    f_rows = _forget_sums(lf_rows, tk_pad)
    f_q = f_rows[:, :, q_pos0:q_pos0 + tq_pad]
    f_k = jnp.transpose(f_rows, (0, 2, 1))
    pad_q = lambda a: jnp.pad(a, ((0, 0),) * (a.ndim - 2) + ((0, tq_pad - seq_len), (0, 0)))

    oa = _sb_attention(pad_q(seq3(qa_b)), ka_all, va_all, q_pos0, n_valid)[:, :seq_len]
    ob = _mla_attention(pad_q(qm_b.reshape(H_B, b, seq_len, 2 * LANES)), kl_all, lp["wuvt"], layer, tq_pad, q_pos0,
                        n_valid)[:, :seq_len]
    oc = _fox_attention(pad_q(seq3(qc_b)), kc_all, vc_all, f_q, f_k, q_pos0, n_valid)[:, :seq_len]
    flat = lambda a: a.reshape(n, a.shape[-1])
    h = _post(h, flat(oa), flat(ob), flat(oc), p, layer, lp["post"])
    return h, state, feature_major


def _trunk(x, p, q_pos0, caches, lp, depth):
    b, t, d = x.shape
    tables = _rope_tables(q_pos0 + jnp.arange(t, dtype=jnp.int32))
    h = x.reshape(b * t, d)
    p = p.reshape(depth, b * t, -1)
    states, st, stacked_mode = [], None, False
    for i in range(depth):
        past = None if caches is None else [c[i] for c in caches]
        h, st, stacked_mode = _layer(h, p, t, q_pos0, past, i, lp, tables, st if stacked_mode else None)
        states.append(st)
    if stacked_mode:
        heads5 = lambda a, nh: jnp.transpose(a.reshape(depth, b, nh, HEAD_DIM, t), (0, 1, 4, 2, 3))
        swap = lambda a: jnp.transpose(a, (0, 1, 3, 2))
        ka, va, ckv, kr, kc, vc, lf = st
        stacked = [heads5(ka, H_A), heads5(va, H_A), ckv.reshape(depth, b, t, KV_LORA), swap(kr),
                   heads5(kc, H_C), heads5(vc, H_C), swap(lf)]
    else:
        stacked = [jnp.stack([s[j] for s in states]) for j in range(N_STATE)]
    return h.reshape(b, t, d), stacked


def kernel(x_prompt, x_sample, p_prompt, p_sample, cache_a_k, cache_a_v, cache_b_ckv, cache_b_krope, cache_c_k, cache_c_v, cache_c_logf, g_ff1_pre, g_ff1_post, w_ff1_gu, w_ff1_down, g_mix_pre, g_mix_post, w_in, b_f, g_bq, g_bkv, w_uq, w_ukv, g_grp, w_out, g_ff2_pre, g_ff2_post, w_ff2_gu, w_ff2_down, g_ple_pre, w_ple_gate, w_ple_proj, g_ple_post):
    weights = (g_ff1_pre, g_ff1_post, w_ff1_gu, w_ff1_down, g_mix_pre, g_mix_post, w_in, b_f,
               g_bq, g_bkv, w_uq, w_ukv, g_grp, w_out, g_ff2_pre, g_ff2_post, w_ff2_gu, w_ff2_down,
               g_ple_pre, w_ple_gate, w_ple_proj, g_ple_post)
    depth = w_in.shape[0]
    lp = _prep_weights(weights)
    y_prompt, sp = _trunk(x_prompt, p_prompt, 0, None, lp, depth)
    caches = (cache_a_k, cache_a_v, cache_b_ckv, cache_b_krope, cache_c_k, cache_c_v, cache_c_logf)
    y_sample, ss = _trunk(x_sample, p_sample, cache_a_k.shape[2], caches, lp, depth)
    return (y_prompt, y_sample, *sp, *ss)
```

```python
import functools
import math

import jax
import jax.numpy as jnp
from jax import lax
from jax.experimental import pallas as pl
from jax.experimental.pallas import tpu as pltpu

CHUNK = 64
HEAD_DIM = 64
H_A = 4
H_B = 8
H_C = 4
W_A = H_A * HEAD_DIM
MLA_NOPE = 64
MLA_ROPE = 32
MLA_V = 64
W_B = H_B * MLA_V
W_C = H_C * HEAD_DIM
Q_LORA = 256
KV_LORA = 128
ROPE_THETA = 10000.0
EPS = 1e-6
FFN_RES = 0.5
SB_SCALE = HEAD_DIM ** -0.5
MLA_SCALE = (MLA_NOPE + MLA_ROPE) ** -0.5
FOX_SCALE = HEAD_DIM ** -0.5
LOG2E = math.log2(math.e)

LANES = 128
KEY_BLOCK = 256
F_ROWS = 16
VMEM_LIMIT = 56 * 1024 * 1024

COL_A = 0
COL_CQ = COL_A + 3 * W_A
COL_CKV = COL_CQ + Q_LORA
COL_KRA = COL_CKV + KV_LORA
COL_KRB = COL_KRA + LANES
COL_C = COL_KRB + LANES
COL_F = COL_C + 3 * W_C
IN_COLS_P = COL_F + LANES

BF16 = jnp.bfloat16
F32 = jnp.float32


def _dot(a, b):
    return jnp.dot(a, b, preferred_element_type=F32)


def _dot_nt(a, b):
    return lax.dot_general(a, b, (((1,), (1,)), ((), ())), preferred_element_type=F32)


def _rms(x, g):
    ms = jnp.mean(x * x, axis=-1, keepdims=True)
    return x * lax.rsqrt(ms + EPS) * g


def _log_sigmoid(x):
    return jnp.minimum(x, 0.0) - jnp.log(1.0 + jnp.exp(-jnp.abs(x)))


def _softplus(x):
    return jnp.maximum(x, 0.0) + jnp.log(1.0 + jnp.exp2(jnp.abs(x) * (-LOG2E)))


def _split2(x):
    hi = x.astype(BF16)
    lo = (x - hi.astype(F32)).astype(BF16)
    return hi, lo


def _layer_spec(a, layer):
    idx = (layer,) + (0,) * (a.ndim - 1)
    return pl.BlockSpec((None,) + a.shape[1:], lambda *_: idx, pipeline_mode=pl.Buffered(1))


def _params(n_axes):
    return pltpu.CompilerParams(dimension_semantics=("arbitrary",) * n_axes, vmem_limit_bytes=VMEM_LIMIT)


def _row_tile(n):
    for tm in (512, 256):
        if n % tm == 0:
            return tm
    return n


def _ffn_rows(h, gpre_ref, gpost_ref, wgu_ref, wdown_ref, d_ff, chunks):
    xn = _rms(h, gpre_ref[...]).astype(BF16)
    acc = jnp.zeros(h.shape, F32)
    for c0, c1 in chunks:
        g = _dot(xn, wgu_ref[:, c0:c1])
        u = _dot(xn, wgu_ref[:, d_ff + c0:d_ff + c1])
        a = (g * jax.nn.sigmoid(g) * u).astype(BF16)
        acc = acc + _dot(a, wdown_ref[c0:c1, :])
    return h + FFN_RES * _rms(acc, gpost_ref[...])


def _ffn_chunks(d_ff):
    step = 4 * KEY_BLOCK
    return tuple((c, min(c + step, d_ff)) for c in range(0, d_ff, step))


def _ffn_kernel(h_ref, gpre_ref, gpost_ref, wgu_ref, wdown_ref, o_ref, *, d_ff, chunks):
    o_ref[...] = _ffn_rows(h_ref[...], gpre_ref, gpost_ref, wgu_ref, wdown_ref, d_ff, chunks)


def _ffn(h, layer, gpre, gpost, wgu, wdown):
    n, d = h.shape
    d_ff = wdown.shape[1]
    consts = [gpre, gpost, wgu, wdown]
    tm = _row_tile(n)
    chunks = _ffn_chunks(d_ff)
    row = pl.BlockSpec((tm, d), lambda i: (i, 0))
    return pl.pallas_call(
        functools.partial(_ffn_kernel, d_ff=d_ff, chunks=chunks),
        grid=(n // tm,),
        in_specs=[row] + [_layer_spec(c, layer) for c in consts],
        out_specs=row,
        out_shape=jax.ShapeDtypeStruct((n, d), F32),
        compiler_params=_params(1),
        name="ffn",
    )(h, gpre, gpost, wgu, wdown)


def _wcomb_kernel(wq_ref, wk_ref, o_ref):
    o_ref[...] = _dot_nt(wq_ref[...], wk_ref[...]).astype(BF16)


def _wcomb(wq_n, wk_n):
    depth, nh, ql, dn = wq_n.shape
    kl = wk_n.shape[2]
    return pl.pallas_call(
        _wcomb_kernel,
        grid=(depth, nh),
        in_specs=[pl.BlockSpec((None, None, ql, dn), lambda i, h: (i, h, 0, 0)),
                  pl.BlockSpec((None, None, kl, dn), lambda i, h: (i, h, 0, 0))],
        out_specs=pl.BlockSpec((None, ql, kl), lambda i, h: (i, 0, h)),
        out_shape=jax.ShapeDtypeStruct((depth, ql, nh * kl), BF16),
        compiler_params=_params(2),
        name="wcomb",
    )(wq_n, wk_n)


def _inproj_kernel(*refs, feature_major, n_alias):
    (h_ref, gpre_ref, win_ref, bf_ref, gbq_ref, gbkv_ref, wcomb_ref, wqr_ref, wkvt_ref, cos_ref, sin_ref) = refs[:11]
    (ka_st, va_st, ckv_st, kr_st, kc_st, vc_st, lf_st,
     qa_b, ka_b, va_b, qm_b, kl_b, qc_b, kc_b, vc_b) = refs[11 + n_alias:]
    xn = _rms(h_ref[...], gpre_ref[...]).astype(BF16)
    proj = _dot(xn, win_ref[...])
    cos = cos_ref[...]
    sin = sin_ref[...]

    qa_b[...] = (proj[:, COL_A:COL_A + W_A] * SB_SCALE).astype(BF16)
    qc_b[...] = (proj[:, COL_C:COL_C + W_C] * (FOX_SCALE * LOG2E)).astype(BF16)
    ka = proj[:, COL_A + W_A:COL_A + 2 * W_A]
    kc = proj[:, COL_C + W_C:COL_C + 2 * W_C]
    ka_b[...] = ka.astype(BF16)
    kc_b[...] = kc.astype(BF16)
    lf = _log_sigmoid(proj[:, COL_F:COL_F + LANES] + bf_ref[...])
    ckv = _rms(proj[:, COL_CKV:COL_CKV + KV_LORA], gbkv_ref[...])
    ckv_st[...] = ckv
    kr = proj[:, COL_KRA:COL_KRA + LANES] * cos + proj[:, COL_KRB:COL_KRB + LANES] * sin
    kl_b[:, :KV_LORA] = ckv.astype(BF16)
    kl_b[:, KV_LORA:] = kr.astype(BF16)

    if feature_major:
        kvt = _dot_nt(wkvt_ref[...], xn)
        for idx, st in enumerate((ka_st, va_st, kc_st, vc_st)):
            st[...] = kvt[idx * W_A:(idx + 1) * W_A]
        for r in range(va_b.shape[0]):
            cols = slice(r * KEY_BLOCK, (r + 1) * KEY_BLOCK)
            va_b[r] = kvt[W_A:2 * W_A, cols].astype(BF16)
            vc_b[r] = kvt[3 * W_A:4 * W_A, cols].astype(BF16)
        kr_st[...] = kr.T[:MLA_ROPE]
        lf_st[...] = lf.T[:H_C]
    else:
        va = proj[:, COL_A + 2 * W_A:COL_A + 3 * W_A]
        vc = proj[:, COL_C + 2 * W_C:COL_C + 3 * W_C]
        ka_st[...] = ka
        va_st[...] = va
        kc_st[...] = kc
        vc_st[...] = vc
        va_b[...] = va.astype(BF16)
        vc_b[...] = vc.astype(BF16)
        kr_st[...] = kr[:, :MLA_ROPE]
        lf_st[...] = lf[:, :H_C]

    cqn = _rms(proj[:, COL_CQ:COL_CQ + Q_LORA], gbq_ref[...]).astype(BF16)
    qlat = _dot(cqn, wcomb_ref[...])
    qr = _dot(cqn, wqr_ref[...])
    half = H_B * LANES
    for hh in range(H_B):
        sl = slice(hh * LANES, (hh + 1) * LANES)
        rope = qr[:, sl] * cos + qr[:, half + hh * LANES:half + (hh + 1) * LANES] * sin
        qm_b[hh, :, :KV_LORA] = (qlat[:, sl] * (MLA_SCALE * LOG2E)).astype(BF16)
        qm_b[hh, :, KV_LORA:] = (rope * (MLA_SCALE * LOG2E)).astype(BF16)


N_STATE = 7


def _inproj(h, seq_len, layer, prev_states, gpre, win, bf, gbq, gbkv, wcomb, wqr, wkvt, cos_t, sin_t):
    n, d = h.shape
    b = n // seq_len
    depth = win.shape[0]
    tm = _row_tile(n)
    row = lambda w: pl.BlockSpec((tm, w), lambda i: (i, 0))
    consts = [gpre, win, bf, gbq, gbkv, wcomb, wqr, wkvt]
    feature_major = seq_len % tm == 0 and tm % KEY_BLOCK == 0
    if feature_major:
        per_seq = seq_len // tm
        tab = pl.BlockSpec((tm, LANES), lambda i: (i % per_seq, 0))
        narrow = lambda w: (pl.BlockSpec((None, None, w, tm), lambda i: (layer, i // per_seq, 0, i % per_seq)),
                            (depth, b, w, seq_len), F32)
        latent = (pl.BlockSpec((None, tm, KV_LORA), lambda i: (layer, i, 0)), (depth, n, KV_LORA), F32)
        vals = lambda w: (pl.BlockSpec((tm // KEY_BLOCK, w, KEY_BLOCK), lambda i: (i, 0, 0)),
                          (n // KEY_BLOCK, w, KEY_BLOCK), BF16)
    else:
        cos_t, sin_t = (jnp.tile(t, (b, 1)) for t in (cos_t, sin_t))
        tab = row(LANES)
        narrow = lambda w: (row(w), (n, w), F32)
        latent = narrow(KV_LORA)
        vals = lambda w: (row(w), (n, w), BF16)
    outs = [
        narrow(W_A), narrow(W_A), latent, narrow(MLA_ROPE), narrow(W_C), narrow(W_C), narrow(H_C),
        (row(W_A), (n, W_A), BF16), (row(W_A), (n, W_A), BF16), vals(W_A),
        (pl.BlockSpec((H_B, tm, 2 * LANES), lambda i: (0, i, 0)), (H_B, n, 2 * LANES), BF16),
        (row(2 * LANES), (n, 2 * LANES), BF16),
        (row(W_C), (n, W_C), BF16), (row(W_C), (n, W_C), BF16), vals(W_C),
    ]
    ins = [h, *consts, cos_t, sin_t]
    in_specs = [row(d)] + [_layer_spec(c, layer) for c in consts] + [tab, tab]
    aliases = {}
    if feature_major:
        if prev_states is None:
            prev_states = [jnp.zeros(o[1], o[2]) for o in outs[:N_STATE]]
        aliases = {len(ins) + s: s for s in range(N_STATE)}
        ins = ins + list(prev_states)
        in_specs = in_specs + [pl.BlockSpec(memory_space=pl.ANY)] * N_STATE
    res = pl.pallas_call(
        functools.partial(_inproj_kernel, feature_major=feature_major, n_alias=len(aliases)),
        grid=(n // tm,),
        in_specs=in_specs,
        out_specs=[o[0] for o in outs],
        out_shape=[jax.ShapeDtypeStruct(o[1], o[2]) for o in outs],
        input_output_aliases=aliases,
        compiler_params=_params(1),
        name="inproj",
    )(*ins)
    return res, feature_major


def _cumsum_kernel(x_ref, o_ref, *, n_blocks):
    r = lax.broadcasted_iota(jnp.int32, (KEY_BLOCK, KEY_BLOCK), 0)
    c = lax.broadcasted_iota(jnp.int32, (KEY_BLOCK, KEY_BLOCK), 1)
    upper = jnp.where(r <= c, 1.0, 0.0).astype(BF16)
    carry = jnp.zeros((x_ref.shape[0], 1), F32)
    for j in range(n_blocks):
        sl = slice(j * KEY_BLOCK, (j + 1) * KEY_BLOCK)
        x = x_ref[:, sl]
        hi = x.astype(BF16)
        mid, lo = _split2(x - hi.astype(F32))
        f = _dot(hi, upper) + _dot(mid, upper) + _dot(lo, upper) + carry
        o_ref[:, sl] = f * LOG2E
        carry = f[:, KEY_BLOCK - 1:KEY_BLOCK]


def _cumsum_rows(x):
    b, r, tp = x.shape
    spec = pl.BlockSpec((None, r, tp), lambda i: (i, 0, 0))
    return pl.pallas_call(
        functools.partial(_cumsum_kernel, n_blocks=tp // KEY_BLOCK),
        grid=(b,),
        in_specs=[spec],
        out_specs=spec,
        out_shape=jax.ShapeDtypeStruct(x.shape, F32),
        compiler_params=_params(1),
        name="cumsum_logf",
    )(x)


def _block_range(q_start, tq, n_valid):
    n_full = q_start // KEY_BLOCK
    last = jnp.minimum(((q_start + tq + CHUNK - 1) // CHUNK) * CHUNK, n_valid)
    n_total = (last + KEY_BLOCK - 1) // KEY_BLOCK
    return n_full, n_total


def _positions(q_start, tq, j, cols):
    k_pos = j * KEY_BLOCK + lax.broadcasted_iota(jnp.int32, (KEY_BLOCK, cols), 0)
    lane = lax.broadcasted_iota(jnp.int32, (KEY_BLOCK, cols), 1)
    q_pos = q_start + (lane if cols == tq else lane % tq)
    return q_pos, k_pos


def _transpose_blocks(src_ref, dst_ref, width):
    for j in range(dst_ref.shape[0]):
        blk = src_ref[j * KEY_BLOCK:(j + 1) * KEY_BLOCK, :width].astype(F32)
        dst_ref[j] = blk.T.astype(BF16)


def _pipelined_blocks(n_full, produce, consume, store, load, lead):
    f = lambda i: n_full - 1 - i
    n_pairs = jnp.maximum(n_full - 1, 0) // 2
    rest = n_full - 2 * n_pairs

    def both(i, slot):
        consume(f(i), load(slot), overlap=(lambda: produce(f(i + 1)), lambda vals: store(1 - slot, vals)))

    @pl.when(n_full > 0)
    def _():
        lead((lambda: produce(f(0)), lambda vals: store(0, vals)))

    @pl.when(n_full <= 0)
    def _():
        lead(None)

    def pair_body(t, carry):
        both(2 * t, 0)
        both(2 * t + 1, 1)
        return carry

    lax.fori_loop(0, n_pairs, pair_body, 0)

    @pl.when((n_full > 0) & (rest == 1))
    def _():
        consume(f(2 * n_pairs), load(0))

    @pl.when((n_full > 0) & (rest == 2))
    def _():
        both(2 * n_pairs, 0)
        consume(f(2 * n_pairs + 1), load(1))


def _finite_or_zero(m):
    return jnp.where(m == -jnp.inf, 0.0, m)


def _head_masks(n_heads):
    lane = lax.broadcasted_iota(jnp.int32, (1, n_heads * HEAD_DIM), 1)
    return [(lane >= hh * HEAD_DIM) & (lane < (hh + 1) * HEAD_DIM) for hh in range(n_heads)]


def _sb_kernel(q_ref, k_ref, vt_ref, o_ref, qm_ref, z_ref, x_ref, acc_ref, c_ref, *, tq, q_pos0, n_valid):
    q_start = q_pos0 + pl.program_id(1) * tq
    n_full, n_total = _block_range(q_start, tq, n_valid)
    hmask = _head_masks(H_A)
    q = q_ref[...]
    for hh in range(H_A):
        qm_ref[hh] = jnp.where(hmask[hh], q, jnp.zeros_like(q))
    acc_ref[...] = jnp.zeros(acc_ref.shape, F32)
    c_ref[...] = jnp.zeros(c_ref.shape, F32)
    half = KEY_BLOCK // 2
    r = lax.broadcasted_iota(jnp.int32, (half, KEY_BLOCK), 0)
    c = lax.broadcasted_iota(jnp.int32, (half, KEY_BLOCK), 1)
    neg_upper2 = jnp.where(c % half >= r, -1.0, 0.0).astype(BF16)
    heads = range(H_A)
    rows = [slice(hh * HEAD_DIM, (hh + 1) * HEAD_DIM) for hh in heads]

    def parts(x):
        hi, lo = _split2(x)
        return jnp.concatenate([hi, lo], axis=0)

    def first_stage(j, mask=None):
        off = pl.multiple_of(j * KEY_BLOCK, KEY_BLOCK)
        kb = k_ref[pl.ds(off, KEY_BLOCK), :]
        zs = [_dot_nt(kb, qm_ref[hh]) for hh in heads]
        sps = [_softplus(z) for z in zs]
        if mask is not None:
            sps = [jnp.where(mask, sp, 0.0) for sp in sps]
        return zs, [parts(sp[:half]) for sp in sps], [parts(sp[half:]) for sp in sps]

    def second_stage(j, vals, overlap=None, mask=None):
        zs, early, late = vals
        vts = [vt_ref[j, rows[hh], :] for hh in heads]
        carry = c_ref[...]
        upcoming = overlap[0]() if overlap else None
        cum_late = [_dot(neg_upper2, late[hh]) + carry[hh:hh + 1, :] for hh in heads]
        cum_early = [_dot(neg_upper2, early[hh]) + cum_late[hh][0:1, :] for hh in heads]
        ws = [jnp.exp(zs[hh] + jnp.concatenate([cum_early[hh], cum_late[hh]], axis=0)) for hh in heads]
        if mask is not None:
            ws = [jnp.where(mask, w, 0.0) for w in ws]
        pvs = [_dot(vts[hh], ws[hh].astype(BF16)) for hh in heads]
        for hh in heads:
            acc_ref[rows[hh], :] += pvs[hh]
            c_ref[hh:hh + 1, :] = cum_early[hh][0:1, :]
        if overlap:
            overlap[1](upcoming)

    def store(slot, vals):
        zs, early, late = vals
        for hh in heads:
            z_ref[slot, hh] = zs[hh]
            x_ref[slot, hh, 0] = early[hh]
            x_ref[slot, hh, 1] = late[hh]

    def load(slot):
        return ([z_ref[slot, hh] for hh in heads], [x_ref[slot, hh, 0] for hh in heads],
                [x_ref[slot, hh, 1] for hh in heads])

    def masked_block(j, overlap=None):
        q_pos, k_pos = _positions(q_start, tq, j, tq)
        mask = k_pos < q_pos
        second_stage(j, first_stage(j, mask), overlap=overlap, mask=mask)

    def masked_body(i, carry):
        masked_block(n_total - 1 - i)
        return carry

    lax.fori_loop(0, n_total - n_full - 1, masked_body, 0)
    _pipelined_blocks(n_full, first_stage, second_stage, store, load, lead=functools.partial(masked_block, n_full))
    o_ref[...] = acc_ref[...].T


def _sb_attention(q, k, vt, q_pos0, n_valid):
    b, tq_all, w = q.shape
    tk = k.shape[1]
    tq = min(tq_all, KEY_BLOCK)
    qspec = pl.BlockSpec((None, tq, w), lambda bi, qi: (bi, qi, 0))
    kspec = pl.BlockSpec((None, tk, w), lambda bi, qi: (bi, 0, 0))
    vspec = pl.BlockSpec((None,) + vt.shape[1:], lambda bi, qi: (bi, 0, 0, 0))
    return pl.pallas_call(
        functools.partial(_sb_kernel, tq=tq, q_pos0=q_pos0, n_valid=n_valid),
        grid=(b, tq_all // tq),
        in_specs=[qspec, kspec, vspec],
        out_specs=qspec,
        out_shape=jax.ShapeDtypeStruct((b, tq_all, w), F32),
        scratch_shapes=[pltpu.VMEM((H_A, tq, w), BF16), pltpu.VMEM((2, H_A, KEY_BLOCK, tq), F32),
                        pltpu.VMEM((2, H_A, 2, KEY_BLOCK, tq), BF16), pltpu.VMEM((w, tq), F32), pltpu.VMEM((8, tq), F32)],
        compiler_params=_params(2),
        name="sb_attention",
    )(q, k, vt)


def _fox_kernel(q_ref, k_ref, vt_ref, fq_ref, fk_ref, o_ref, qm_ref, z_ref, acc_ref, m_ref, l_ref, *, tq, q_pos0, n_valid):
    q_start = q_pos0 + pl.program_id(1) * tq
    n_full, n_total = _block_range(q_start, tq, n_valid)
    hmask = _head_masks(H_C)
    q = q_ref[...]
    for hh in range(H_C):
        qm_ref[hh] = jnp.where(hmask[hh], q, jnp.zeros_like(q))
    acc_ref[...] = jnp.zeros(acc_ref.shape, F32)
    m_ref[...] = jnp.full(m_ref.shape, -jnp.inf, F32)
    l_ref[...] = jnp.zeros(l_ref.shape, F32)

    heads = range(H_C)
    rows = [slice(hh * HEAD_DIM, (hh + 1) * HEAD_DIM) for hh in heads]

    def scores(j):
        off = pl.multiple_of(j * KEY_BLOCK, KEY_BLOCK)
        kb = k_ref[pl.ds(off, KEY_BLOCK), :]
        return [_dot_nt(kb, qm_ref[hh]) for hh in heads]

    def step(j, qk, overlap=None, mask=None):
        off = pl.multiple_of(j * KEY_BLOCK, KEY_BLOCK)
        fk = fk_ref[pl.ds(off, KEY_BLOCK), :]
        vts = [vt_ref[j, rows[hh], :] for hh in heads]
        fq, m_all, l_all = fq_ref[...], m_ref[...], l_ref[...]
        accs = [acc_ref[rows[hh], :] for hh in heads]
        upcoming = overlap[0]() if overlap else None
        zs = [qk[hh] + (fq[hh:hh + 1, :] - fk[:, hh:hh + 1]) for hh in heads]
        if mask is not None:
            zs = [jnp.where(mask, z, -jnp.inf) for z in zs]
        m_new = [jnp.maximum(m_all[hh:hh + 1, :], jnp.max(zs[hh], axis=0, keepdims=True)) for hh in heads]
        m_use = [_finite_or_zero(m) for m in m_new]
        ps = [jnp.exp2(zs[hh] - m_use[hh]) for hh in heads]
        alpha = [jnp.exp2(m_all[hh:hh + 1, :] - m_use[hh]) for hh in heads]
        l_new = [alpha[hh] * l_all[hh:hh + 1, :] + jnp.sum(ps[hh], axis=0, keepdims=True) for hh in heads]
        pvs = [_dot(vts[hh], ps[hh].astype(BF16)) for hh in heads]
        for hh in heads:
            m_ref[hh:hh + 1, :] = m_new[hh]
            l_ref[hh:hh + 1, :] = l_new[hh]
            acc_ref[rows[hh], :] = accs[hh] * alpha[hh] + pvs[hh]
        if overlap:
            overlap[1](upcoming)

    def store(slot, qk):
        for hh in heads:
            z_ref[slot, hh] = qk[hh]

    def load(slot):
        return [z_ref[slot, hh] for hh in heads]

    def masked_block(j, overlap=None):
        q_pos, k_pos = _positions(q_start, tq, j, tq)
        step(j, scores(j), overlap=overlap, mask=k_pos <= q_pos)

    def masked_body(i, carry):
        masked_block(n_total - 1 - i)
        return carry

    lax.fori_loop(0, n_total - n_full - 1, masked_body, 0)
    _pipelined_blocks(n_full, scores, step, store, load, lead=functools.partial(masked_block, n_full))
    for hh in range(H_C):
        rows = slice(hh * HEAD_DIM, (hh + 1) * HEAD_DIM)
        acc_ref[rows, :] = acc_ref[rows, :] * (1.0 / l_ref[hh:hh + 1, :])
    o_ref[...] = acc_ref[...].T


def _fox_attention(q, k, vt, fq, fk, q_pos0, n_valid):
    b, tq_all, w = q.shape
    tk = k.shape[1]
    tq = min(tq_all, KEY_BLOCK)
    qspec = pl.BlockSpec((None, tq, w), lambda bi, qi: (bi, qi, 0))
    kspec = pl.BlockSpec((None, tk, w), lambda bi, qi: (bi, 0, 0))
    vspec = pl.BlockSpec((None,) + vt.shape[1:], lambda bi, qi: (bi, 0, 0, 0))
    fqspec = pl.BlockSpec((None, F_ROWS, tq), lambda bi, qi: (bi, 0, qi))
    fkspec = pl.BlockSpec((None, tk, F_ROWS), lambda bi, qi: (bi, 0, 0))
    return pl.pallas_call(
        functools.partial(_fox_kernel, tq=tq, q_pos0=q_pos0, n_valid=n_valid),
        grid=(b, tq_all // tq),
        in_specs=[qspec, kspec, vspec, fqspec, fkspec],
        out_specs=qspec,
        out_shape=jax.ShapeDtypeStruct((b, tq_all, w), F32),
        scratch_shapes=[pltpu.VMEM((H_C, tq, w), BF16), pltpu.VMEM((2, H_C, KEY_BLOCK, tq), F32),
                        pltpu.VMEM((w, tq), F32), pltpu.VMEM((8, tq), F32), pltpu.VMEM((8, tq), F32)],
        compiler_params=_params(2),
        name="fox_attention",
    )(q, k, vt, fq, fk)


def _mla_kernel(q_ref, kl_ref, wuvt_ref, o_ref, ct_ref, z_ref, acc_ref, m_ref, l_ref, *, tq, q_pos0, n_valid):
    @pl.when(pl.program_id(1) == 0)
    def _():
        _transpose_blocks(kl_ref, ct_ref, KV_LORA)

    q_start = q_pos0 + pl.program_id(1) * tq
    n_full, n_total = _block_range(q_start, tq, n_valid)
    cols = H_B * tq
    qs = q_ref[...].reshape(cols, 2 * LANES)
    acc_ref[...] = jnp.zeros(acc_ref.shape, F32)
    m_ref[...] = jnp.full(m_ref.shape, -jnp.inf, F32)
    l_ref[...] = jnp.zeros(l_ref.shape, F32)

    group = KEY_BLOCK
    groups = [slice(g * group, (g + 1) * group) for g in range(cols // group)]

    gs = range(len(groups))

    def scores(j):
        off = pl.multiple_of(j * KEY_BLOCK, KEY_BLOCK)
        kb = kl_ref[pl.ds(off, KEY_BLOCK), :]
        return [_dot_nt(kb, qs[g]) for g in groups]

    def step(j, zs, overlap=None):
        ct = ct_ref[j]
        m_all, l_all = m_ref[...], l_ref[...]
        accs = [acc_ref[:, g] for g in groups]
        upcoming = overlap[0]() if overlap else None
        m_new = [jnp.maximum(m_all[:, groups[gi]], jnp.max(zs[gi], axis=0, keepdims=True)) for gi in gs]
        m_use = [_finite_or_zero(m) for m in m_new]
        ps = [jnp.exp2(zs[gi] - m_use[gi]) for gi in gs]
        alpha = [jnp.exp2(m_all[:, groups[gi]] - m_use[gi]) for gi in gs]
        l_new = [alpha[gi] * l_all[:, groups[gi]] + jnp.sum(ps[gi], axis=0, keepdims=True) for gi in gs]
        pvs = [_dot(ct, ps[gi].astype(BF16)) for gi in gs]
        for gi, g in enumerate(groups):
            m_ref[:, g] = m_new[gi]
            l_ref[:, g] = l_new[gi]
            acc_ref[:, g] = accs[gi] * alpha[gi] + pvs[gi]
        if overlap:
            overlap[1](upcoming)

    def store(slot, zs):
        for gi, g in enumerate(groups):
            z_ref[slot, :, g] = zs[gi]

    def load(slot):
        return [z_ref[slot, :, g] for g in groups]

    def masked_block(j, overlap=None):
        q_pos, k_pos = _positions(q_start, tq, j, group)
        mask = (k_pos // CHUNK <= q_pos // CHUNK) & (k_pos < n_valid)
        step(j, [jnp.where(mask, z, -jnp.inf) for z in scores(j)], overlap=overlap)

    def masked_body(i, carry):
        masked_block(n_total - 1 - i)
        return carry

    lax.fori_loop(0, n_total - n_full - 1, masked_body, 0)
    _pipelined_blocks(n_full, scores, step, store, load, lead=functools.partial(masked_block, n_full))

    lat = (acc_ref[...] * (1.0 / l_ref[...])).astype(BF16)
    heads = [_dot(wuvt_ref[hh], lat[:, hh * tq:(hh + 1) * tq]) for hh in range(H_B)]
    o_ref[...] = jnp.concatenate(heads, axis=0).T


def _mla_attention(qm, kl, wuvt, layer, tq_all, q_pos0, n_valid):
    b, tk, w = kl.shape
    tq = KEY_BLOCK if tq_all % KEY_BLOCK == 0 else LANES
    cols = H_B * tq
    return pl.pallas_call(
        functools.partial(_mla_kernel, tq=tq, q_pos0=q_pos0, n_valid=n_valid),
        grid=(b, tq_all // tq),
        in_specs=[pl.BlockSpec((H_B, None, tq, w), lambda bi, qi: (0, bi, qi, 0)),
                  pl.BlockSpec((None, tk, w), lambda bi, qi: (bi, 0, 0)),
                  _layer_spec(wuvt, layer)],
        out_specs=pl.BlockSpec((None, tq, W_B), lambda bi, qi: (bi, qi, 0)),
        out_shape=jax.ShapeDtypeStruct((b, tq_all, W_B), F32),
        scratch_shapes=[pltpu.VMEM((tk // KEY_BLOCK, KV_LORA, KEY_BLOCK), BF16), pltpu.VMEM((2, KEY_BLOCK, cols), F32),
                        pltpu.VMEM((KV_LORA, cols), F32), pltpu.VMEM((1, cols), F32), pltpu.VMEM((1, cols), F32)],
        compiler_params=_params(2),
        name="mla_attention",
    )(qm, kl, wuvt)


def _post_kernel(h_ref, oa_ref, ob_ref, oc_ref, p_ref, ggrp_ref, wout_ref, gmix_ref,
                 gpre_ref, gpost_ref, wgu_ref, wdown_ref, gple_pre_ref, wgate_ref, wproj_ref, gple_post_ref, o_ref,
                 *, d_ff, chunks):
    m = jnp.zeros(h_ref.shape, F32)
    c0 = 0
    for o_grp in (oa_ref, ob_ref, oc_ref):
        c1 = c0 + o_grp.shape[1]
        m = m + _dot(_rms(o_grp[...], ggrp_ref[:, c0:c1]).astype(BF16), wout_ref[c0:c1, :])
        c0 = c1
    h = h_ref[...] + _rms(m, gmix_ref[...])
    h = _ffn_rows(h, gpre_ref, gpost_ref, wgu_ref, wdown_ref, d_ff, chunks)
    gate = jax.nn.sigmoid(_dot(_rms(h, gple_pre_ref[...]).astype(BF16), wgate_ref[...]))
    e = _dot(p_ref[...].astype(BF16), wproj_ref[...]) * gate
    o_ref[...] = h + _rms(e, gple_post_ref[...])


def _post(h, oa, ob, oc, p, layer, consts):
    n, d = h.shape
    tm = _row_tile(n)
    row = lambda w: pl.BlockSpec((tm, w), lambda i: (i, 0))
    p_spec = pl.BlockSpec((None, tm, p.shape[2]), lambda i: (layer, i, 0))
    d_ff = consts[6].shape[1]
    return pl.pallas_call(
        functools.partial(_post_kernel, d_ff=d_ff, chunks=_ffn_chunks(d_ff)),
        grid=(n // tm,),
        in_specs=[row(d), row(W_A), row(W_B), row(W_C), p_spec] + [_layer_spec(c, layer) for c in consts],
        out_specs=row(d),
        out_shape=jax.ShapeDtypeStruct((n, d), F32),
        compiler_params=_params(1),
        name="post",
    )(h, oa, ob, oc, p, *consts)


def _rope_tables(pos):
    half = MLA_ROPE // 2
    inv = ROPE_THETA ** (-jnp.arange(half, dtype=F32) / half)
    ang = pos.astype(F32)[:, None] * inv[None, :]
    cos, sin = jnp.cos(ang), jnp.sin(ang)
    pad = jnp.zeros((pos.shape[0], LANES - MLA_ROPE), F32)
    return jnp.concatenate([cos, cos, pad], axis=1), jnp.concatenate([-sin, sin, pad], axis=1)


def _prep_weights(weights):
    (g_ff1_pre, g_ff1_post, w_ff1_gu, w_ff1_down, g_mix_pre, g_mix_post, w_in, b_f, g_bq, g_bkv, w_uq, w_ukv,
     g_grp, w_out, g_ff2_pre, g_ff2_post, w_ff2_gu, w_ff2_down, g_ple_pre, w_ple_gate, w_ple_proj, g_ple_post) = weights
    depth, d = w_in.shape[:2]
    half = MLA_ROPE // 2
    row = lambda g: g.reshape(depth, 1, -1).astype(F32)
    bf = lambda w: w.astype(BF16)
    c_kr = 3 * W_A + Q_LORA + KV_LORA
    c_c = c_kr + MLA_ROPE
    c_f = c_c + 3 * W_C
    w_in = bf(w_in)
    kr = w_in[:, :, c_kr:c_kr + MLA_ROPE]
    kr_sw = jnp.concatenate([kr[:, :, half:], kr[:, :, :half]], axis=2)
    zpad = jnp.zeros((depth, d, LANES - MLA_ROPE), BF16)
    win_p = jnp.concatenate([w_in[:, :, :c_kr], kr, zpad, kr_sw, zpad, w_in[:, :, c_c:c_f], w_in[:, :, c_f:],
                             jnp.zeros((depth, d, LANES - H_C), BF16)], axis=2)
    assert win_p.shape[2] == IN_COLS_P
    bf_p = jnp.pad(b_f.astype(F32), ((0, 0), (0, LANES - H_C))).reshape(depth, 1, LANES)
    wkvt = jnp.transpose(jnp.concatenate([w_in[:, :, W_A:3 * W_A], w_in[:, :, c_c + W_C:c_f]], axis=2), (0, 2, 1))

    wq4 = bf(w_uq).reshape(depth, Q_LORA, H_B, MLA_NOPE + MLA_ROPE)
    wkv4 = bf(w_ukv).reshape(depth, KV_LORA, H_B, MLA_NOPE + MLA_V)
    wq_n = jnp.transpose(wq4[..., :MLA_NOPE], (0, 2, 1, 3))
    wk_n = jnp.transpose(wkv4[..., :MLA_NOPE], (0, 2, 1, 3))
    wcomb = _wcomb(wq_n, wk_n)
    x1 = wq4[..., MLA_NOPE:MLA_NOPE + half]
    x2 = wq4[..., MLA_NOPE + half:]
    zq = jnp.zeros((depth, Q_LORA, H_B, LANES - MLA_ROPE), BF16)
    wqr = jnp.concatenate([jnp.concatenate([x1, x2, zq], axis=3).reshape(depth, Q_LORA, H_B * LANES),
                           jnp.concatenate([x2, x1, zq], axis=3).reshape(depth, Q_LORA, H_B * LANES)], axis=2)
    wuvt = jnp.transpose(wkv4[..., MLA_NOPE:], (0, 2, 3, 1))

    return dict(
        ff1=(row(g_ff1_pre), row(g_ff1_post), bf(w_ff1_gu), bf(w_ff1_down)),
        inproj=(row(g_mix_pre), win_p, bf_p, row(g_bq), row(g_bkv), wcomb, wqr, wkvt),
        wuvt=wuvt,
        post=(row(g_grp), bf(w_out), row(g_mix_post),
              row(g_ff2_pre), row(g_ff2_post), bf(w_ff2_gu), bf(w_ff2_down),
              row(g_ple_pre), bf(w_ple_gate), bf(w_ple_proj), row(g_ple_post)),
    )


def _pad_keys(a, tk_pad):
    return jnp.pad(a, ((0, 0), (0, tk_pad - a.shape[1])) + ((0, 0),) * (a.ndim - 2))


def _forget_sums(logf_rows, tk_pad):
    _, heads, tk = logf_rows.shape
    return _cumsum_rows(jnp.pad(logf_rows, ((0, 0), (0, F_ROWS - heads), (0, tk_pad - tk))))


def _layer(h, p, seq_len, q_pos0, past, layer, lp, tables, prev_states):
    n, d = h.shape
    b = n // seq_len
    h = _ffn(h, layer, *lp["ff1"])
    (ka_st, va_st, ckv_st, kr_st, kc_st, vc_st, lf_st,
     qa_b, ka_b, va_b, qm_b, kl_b, qc_b, kc_b, vc_b), feature_major = _inproj(h, seq_len, layer, prev_states,
                                                                                *lp["inproj"], *tables)
    if feature_major:
        state = (ka_st, va_st, ckv_st, kr_st, kc_st, vc_st, lf_st)
        lf_rows_new = lf_st[layer]
    else:
        state = (ka_st.reshape(b, seq_len, H_A, HEAD_DIM), va_st.reshape(b, seq_len, H_A, HEAD_DIM),
                 ckv_st.reshape(b, seq_len, KV_LORA), kr_st.reshape(b, seq_len, MLA_ROPE),
                 kc_st.reshape(b, seq_len, H_C, HEAD_DIM), vc_st.reshape(b, seq_len, H_C, HEAD_DIM),
                 lf_st.reshape(b, seq_len, H_C))
        lf_rows_new = jnp.transpose(state[6], (0, 2, 1))
    seq3 = lambda a: a.reshape(b, seq_len, a.shape[-1])
    tq_pad = -(-seq_len // LANES) * LANES
    to_blocks = lambda v: jnp.transpose(v.reshape(b, v.shape[1] // KEY_BLOCK, KEY_BLOCK, v.shape[2]), (0, 1, 3, 2))
    if past is None:
        n_valid = seq_len
        tk_pad = -(-tq_pad // KEY_BLOCK) * KEY_BLOCK
        ka_all, kl_all, kc_all = (_pad_keys(seq3(a), tk_pad) for a in (ka_b, kl_b, kc_b))
        if feature_major:
            va_all, vc_all = (v.reshape(b, seq_len // KEY_BLOCK, v.shape[1], KEY_BLOCK) for v in (va_b, vc_b))
        else:
            va_all, vc_all = (to_blocks(_pad_keys(seq3(v), tk_pad)) for v in (va_b, vc_b))
        lf_rows = lf_rows_new
    else:
        assert not feature_major
        pa_k, pa_v, pb_ckv, pb_kr, pc_k, pc_v, pc_lf = past
        past_len = pa_k.shape[1]
        n_valid = past_len + seq_len
        tk_pad = -(-(past_len + tq_pad) // KEY_BLOCK) * KEY_BLOCK
        join = lambda c, new: _pad_keys(jnp.concatenate([c.reshape(b, past_len, -1).astype(BF16), seq3(new)], axis=1), tk_pad)
        ka_all, kc_all = join(pa_k, ka_b), join(pc_k, kc_b)
        va_all, vc_all = to_blocks(join(pa_v, va_b)), to_blocks(join(pc_v, vc_b))
        kl_past = jnp.concatenate([pb_ckv, pb_kr, jnp.zeros((b, past_len, LANES - MLA_ROPE), F32)], axis=-1)
        kl_all = join(kl_past, kl_b)
        lf_rows = jnp.concatenate([jnp.transpose(pc_lf, (0, 2, 1)), lf_rows_new], axis=2)
    f_rows = _forget_sums(lf_rows, tk_pad)
    f_q = f_rows[:, :, q_pos0:q_pos0 + tq_pad]
    f_k = jnp.transpose(f_rows, (0, 2, 1))
    pad_q = lambda a: jnp.pad(a, ((0, 0),) * (a.ndim - 2) + ((0, tq_pad - seq_len), (0, 0)))

    oa = _sb_attention(pad_q(seq3(qa_b)), ka_all, va_all, q_pos0, n_valid)[:, :seq_len]
    ob = _mla_attention(pad_q(qm_b.reshape(H_B, b, seq_len, 2 * LANES)), kl_all, lp["wuvt"], layer, tq_pad, q_pos0,
                        n_valid)[:, :seq_len]
    oc = _fox_attention(pad_q(seq3(qc_b)), kc_all, vc_all, f_q, f_k, q_pos0, n_valid)[:, :seq_len]
    flat = lambda a: a.reshape(n, a.shape[-1])
    h = _post(h, flat(oa), flat(ob), flat(oc), p, layer, lp["post"])
    return h, state, feature_major


def _trunk(x, p, q_pos0, caches, lp, depth):
    b, t, d = x.shape
    tables = _rope_tables(q_pos0 + jnp.arange(t, dtype=jnp.int32))
    h = x.reshape(b * t, d)
    p = p.reshape(depth, b * t, -1)
    states, st, stacked_mode = [], None, False
    for i in range(depth):
        past = None if caches is None else [c[i] for c in caches]
        h, st, stacked_mode = _layer(h, p, t, q_pos0, past, i, lp, tables, st if stacked_mode else None)
        states.append(st)
    if stacked_mode:
        heads5 = lambda a, nh: jnp.transpose(a.reshape(depth, b, nh, HEAD_DIM, t), (0, 1, 4, 2, 3))
        swap = lambda a: jnp.transpose(a, (0, 1, 3, 2))
        ka, va, ckv, kr, kc, vc, lf = st
        stacked = [heads5(ka, H_A), heads5(va, H_A), ckv.reshape(depth, b, t, KV_LORA), swap(kr),
                   heads5(kc, H_C), heads5(vc, H_C), swap(lf)]
    else:
        stacked = [jnp.stack([s[j] for s in states]) for j in range(N_STATE)]
    return h.reshape(b, t, d), stacked


def kernel(x_prompt, x_sample, p_prompt, p_sample, cache_a_k, cache_a_v, cache_b_ckv, cache_b_krope, cache_c_k, cache_c_v, cache_c_logf, g_ff1_pre, g_ff1_post, w_ff1_gu, w_ff1_down, g_mix_pre, g_mix_post, w_in, b_f, g_bq, g_bkv, w_uq, w_ukv, g_grp, w_out, g_ff2_pre, g_ff2_post, w_ff2_gu, w_ff2_down, g_ple_pre, w_ple_gate, w_ple_proj, g_ple_post):
    weights = (g_ff1_pre, g_ff1_post, w_ff1_gu, w_ff1_down, g_mix_pre, g_mix_post, w_in, b_f,
               g_bq, g_bkv, w_uq, w_ukv, g_grp, w_out, g_ff2_pre, g_ff2_post, w_ff2_gu, w_ff2_down,
               g_ple_pre, w_ple_gate, w_ple_proj, g_ple_post)
    depth = w_in.shape[0]
    lp = _prep_weights(weights)
    y_prompt, sp = _trunk(x_prompt, p_prompt, 0, None, lp, depth)
    caches = (cache_a_k, cache_a_v, cache_b_ckv, cache_b_krope, cache_c_k, cache_c_v, cache_c_logf)
    y_sample, ss = _trunk(x_sample, p_sample, cache_a_k.shape[2], caches, lp, depth)
    return (y_prompt, y_sample, *sp, *ss)
```

```python
import functools
import math

import jax
import jax.numpy as jnp
from jax import lax
from jax.experimental import pallas as pl
from jax.experimental.pallas import tpu as pltpu

CHUNK = 64
HEAD_DIM = 64
H_A = 4
H_B = 8
H_C = 4
W_A = H_A * HEAD_DIM
MLA_NOPE = 64
MLA_ROPE = 32
MLA_V = 64
W_B = H_B * MLA_V
W_C = H_C * HEAD_DIM
Q_LORA = 256
KV_LORA = 128
ROPE_THETA = 10000.0
EPS = 1e-6
FFN_RES = 0.5
SB_SCALE = HEAD_DIM ** -0.5
MLA_SCALE = (MLA_NOPE + MLA_ROPE) ** -0.5
FOX_SCALE = HEAD_DIM ** -0.5
LOG2E = math.log2(math.e)

LANES = 128
KEY_BLOCK = 256
F_ROWS = 16
VMEM_LIMIT = 56 * 1024 * 1024

COL_A = 0
COL_CQ = COL_A + 3 * W_A
COL_CKV = COL_CQ + Q_LORA
COL_KRA = COL_CKV + KV_LORA
COL_KRB = COL_KRA + LANES
COL_C = COL_KRB + LANES
COL_F = COL_C + 3 * W_C
IN_COLS_P = COL_F + LANES

BF16 = jnp.bfloat16
F32 = jnp.float32


def _dot(a, b):
    return jnp.dot(a, b, preferred_element_type=F32)


def _dot_nt(a, b):
    return lax.dot_general(a, b, (((1,), (1,)), ((), ())), preferred_element_type=F32)


def _rms(x, g):
    ms = jnp.mean(x * x, axis=-1, keepdims=True)
    return x * lax.rsqrt(ms + EPS) * g


def _log_sigmoid(x):
    return jnp.minimum(x, 0.0) - jnp.log(1.0 + jnp.exp(-jnp.abs(x)))


def _softplus(x):
    return jnp.maximum(x, 0.0) + jnp.log(1.0 + jnp.exp2(jnp.abs(x) * (-LOG2E)))


def _split2(x):
    hi = x.astype(BF16)
    lo = (x - hi.astype(F32)).astype(BF16)
    return hi, lo


def _layer_spec(a, layer):
    idx = (layer,) + (0,) * (a.ndim - 1)
    return pl.BlockSpec((None,) + a.shape[1:], lambda *_: idx, pipeline_mode=pl.Buffered(1))


def _params(n_axes):
    return pltpu.CompilerParams(dimension_semantics=("arbitrary",) * n_axes, vmem_limit_bytes=VMEM_LIMIT)


def _row_tile(n):
    for tm in (512, 256):
        if n % tm == 0:
            return tm
    return n


def _ffn_rows(h, gpre_ref, gpost_ref, wgu_ref, wdown_ref, d_ff, chunks):
    xn = _rms(h, gpre_ref[...]).astype(BF16)
    acc = jnp.zeros(h.shape, F32)
    for c0, c1 in chunks:
        g = _dot(xn, wgu_ref[:, c0:c1])
        u = _dot(xn, wgu_ref[:, d_ff + c0:d_ff + c1])
        a = (g * jax.nn.sigmoid(g) * u).astype(BF16)
        acc = acc + _dot(a, wdown_ref[c0:c1, :])
    return h + FFN_RES * _rms(acc, gpost_ref[...])


def _ffn_chunks(d_ff):
    step = 4 * KEY_BLOCK
    return tuple((c, min(c + step, d_ff)) for c in range(0, d_ff, step))


def _ffn_kernel(h_ref, gpre_ref, gpost_ref, wgu_ref, wdown_ref, o_ref, *, d_ff, chunks):
    o_ref[...] = _ffn_rows(h_ref[...], gpre_ref, gpost_ref, wgu_ref, wdown_ref, d_ff, chunks)


def _ffn(h, layer, gpre, gpost, wgu, wdown):
    n, d = h.shape
    d_ff = wdown.shape[1]
    consts = [gpre, gpost, wgu, wdown]
    tm = _row_tile(n)
    chunks = _ffn_chunks(d_ff)
    row = pl.BlockSpec((tm, d), lambda i: (i, 0))
    return pl.pallas_call(
        functools.partial(_ffn_kernel, d_ff=d_ff, chunks=chunks),
        grid=(n // tm,),
        in_specs=[row] + [_layer_spec(c, layer) for c in consts],
        out_specs=row,
        out_shape=jax.ShapeDtypeStruct((n, d), F32),
        compiler_params=_params(1),
        name="ffn",
    )(h, gpre, gpost, wgu, wdown)


def _wcomb_kernel(wq_ref, wk_ref, o_ref):
    kl = wk_ref.shape[1]
    for hh in range(wq_ref.shape[0]):
        o_ref[:, hh * kl:(hh + 1) * kl] = _dot_nt(wq_ref[hh], wk_ref[hh]).astype(BF16)


def _wcomb(wq_n, wk_n):
    depth, nh, ql, dn = wq_n.shape
    kl = wk_n.shape[2]
    return pl.pallas_call(
        _wcomb_kernel,
        grid=(depth,),
        in_specs=[pl.BlockSpec((None, nh, ql, dn), lambda i: (i, 0, 0, 0)),
                  pl.BlockSpec((None, nh, kl, dn), lambda i: (i, 0, 0, 0))],
        out_specs=pl.BlockSpec((None, ql, nh * kl), lambda i: (i, 0, 0)),
        out_shape=jax.ShapeDtypeStruct((depth, ql, nh * kl), BF16),
        compiler_params=_params(1),
        name="wcomb",
    )(wq_n, wk_n)


def _inproj_kernel(*refs, feature_major, n_alias, fill_layers):
    (h_ref, gpre_ref, win_ref, bf_ref, gbq_ref, gbkv_ref, wcomb_ref, wqr_ref, wkvt_ref, cos_ref, sin_ref) = refs[:11]
    (ka_st, va_st, ckv_st, kr_st, kc_st, vc_st, lf_st,
     qa_b, ka_b, va_b, qm_b, kl_b, qc_b, kc_b, vc_b) = refs[11 + n_alias:]

    def put(st, val):
        if fill_layers:
            for slab in range(fill_layers):
                st[slab] = val
        else:
            st[...] = val

    xn = _rms(h_ref[...], gpre_ref[...]).astype(BF16)
    proj = _dot(xn, win_ref[...])
    cos = cos_ref[...]
    sin = sin_ref[...]

    qa_b[...] = (proj[:, COL_A:COL_A + W_A] * SB_SCALE).astype(BF16)
    qc_b[...] = (proj[:, COL_C:COL_C + W_C] * (FOX_SCALE * LOG2E)).astype(BF16)
    ka = proj[:, COL_A + W_A:COL_A + 2 * W_A]
    kc = proj[:, COL_C + W_C:COL_C + 2 * W_C]
    ka_b[...] = ka.astype(BF16)
    kc_b[...] = kc.astype(BF16)
    lf = _log_sigmoid(proj[:, COL_F:COL_F + LANES] + bf_ref[...])
    ckv = _rms(proj[:, COL_CKV:COL_CKV + KV_LORA], gbkv_ref[...])
    put(ckv_st, ckv)
    kr = proj[:, COL_KRA:COL_KRA + LANES] * cos + proj[:, COL_KRB:COL_KRB + LANES] * sin
    kl_b[:, :KV_LORA] = ckv.astype(BF16)
    kl_b[:, KV_LORA:] = kr.astype(BF16)

    if feature_major:
        kvt = _dot_nt(wkvt_ref[...], xn)
        for idx, st in enumerate((ka_st, va_st, kc_st, vc_st)):
            put(st, kvt[idx * W_A:(idx + 1) * W_A])
        for r in range(va_b.shape[0]):
            cols = slice(r * KEY_BLOCK, (r + 1) * KEY_BLOCK)
            va_b[r] = kvt[W_A:2 * W_A, cols].astype(BF16)
            vc_b[r] = kvt[3 * W_A:4 * W_A, cols].astype(BF16)
        put(kr_st, kr.T[:MLA_ROPE])
        put(lf_st, lf.T[:H_C])
    else:
        va = proj[:, COL_A + 2 * W_A:COL_A + 3 * W_A]
        vc = proj[:, COL_C + 2 * W_C:COL_C + 3 * W_C]
        ka_st[...] = ka
        va_st[...] = va
        kc_st[...] = kc
        vc_st[...] = vc
        va_b[...] = va.astype(BF16)
        vc_b[...] = vc.astype(BF16)
        kr_st[...] = kr[:, :MLA_ROPE]
        lf_st[...] = lf[:, :H_C]

    cqn = _rms(proj[:, COL_CQ:COL_CQ + Q_LORA], gbq_ref[...]).astype(BF16)
    qlat = _dot(cqn, wcomb_ref[...])
    qr = _dot(cqn, wqr_ref[...])
    half = H_B * LANES
    for hh in range(H_B):
        sl = slice(hh * LANES, (hh + 1) * LANES)
        rope = qr[:, sl] * cos + qr[:, half + hh * LANES:half + (hh + 1) * LANES] * sin
        qm_b[hh, :, :KV_LORA] = (qlat[:, sl] * (MLA_SCALE * LOG2E)).astype(BF16)
        qm_b[hh, :, KV_LORA:] = (rope * (MLA_SCALE * LOG2E)).astype(BF16)


N_STATE = 7


def _inproj(h, seq_len, layer, prev_states, gpre, win, bf, gbq, gbkv, wcomb, wqr, wkvt, cos_t, sin_t):
    n, d = h.shape
    b = n // seq_len
    depth = win.shape[0]
    tm = _row_tile(n)
    row = lambda w: pl.BlockSpec((tm, w), lambda i: (i, 0))
    consts = [gpre, win, bf, gbq, gbkv, wcomb, wqr, wkvt]
    feature_major = seq_len % tm == 0 and tm % KEY_BLOCK == 0
    fill_all = feature_major and prev_states is None
    if feature_major:
        per_seq = seq_len // tm
        tab = pl.BlockSpec((tm, LANES), lambda i: (i % per_seq, 0))
        lead, at = (depth, 0) if fill_all else (None, layer)
        narrow = lambda w: (pl.BlockSpec((lead, None, w, tm), lambda i: (at, i // per_seq, 0, i % per_seq)),
                            (depth, b, w, seq_len), F32)
        latent = (pl.BlockSpec((lead, tm, KV_LORA), lambda i: (at, i, 0)), (depth, n, KV_LORA), F32)
        vals = lambda w: (pl.BlockSpec((tm // KEY_BLOCK, w, KEY_BLOCK), lambda i: (i, 0, 0)),
                          (n // KEY_BLOCK, w, KEY_BLOCK), BF16)
    else:
        cos_t, sin_t = (jnp.tile(t, (b, 1)) for t in (cos_t, sin_t))
        tab = row(LANES)
        narrow = lambda w: (row(w), (n, w), F32)
        latent = narrow(KV_LORA)
        vals = lambda w: (row(w), (n, w), BF16)
    outs = [
        narrow(W_A), narrow(W_A), latent, narrow(MLA_ROPE), narrow(W_C), narrow(W_C), narrow(H_C),
        (row(W_A), (n, W_A), BF16), (row(W_A), (n, W_A), BF16), vals(W_A),
        (pl.BlockSpec((H_B, tm, 2 * LANES), lambda i: (0, i, 0)), (H_B, n, 2 * LANES), BF16),
        (row(2 * LANES), (n, 2 * LANES), BF16),
        (row(W_C), (n, W_C), BF16), (row(W_C), (n, W_C), BF16), vals(W_C),
    ]
    ins = [h, *consts, cos_t, sin_t]
    in_specs = [row(d)] + [_layer_spec(c, layer) for c in consts] + [tab, tab]
    aliases = {}
    if feature_major and prev_states is not None:
        aliases = {len(ins) + s: s for s in range(N_STATE)}
        ins = ins + list(prev_states)
        in_specs = in_specs + [pl.BlockSpec(memory_space=pl.ANY)] * N_STATE
    res = pl.pallas_call(
        functools.partial(_inproj_kernel, feature_major=feature_major, n_alias=len(aliases),
                          fill_layers=depth if fill_all else 0),
        grid=(n // tm,),
        in_specs=in_specs,
        out_specs=[o[0] for o in outs],
        out_shape=[jax.ShapeDtypeStruct(o[1], o[2]) for o in outs],
        input_output_aliases=aliases,
        compiler_params=_params(1),
        name="inproj",
    )(*ins)
    return res, feature_major


def _cumsum_kernel(x_ref, o_ref, *, n_blocks):
    r = lax.broadcasted_iota(jnp.int32, (KEY_BLOCK, KEY_BLOCK), 0)
    c = lax.broadcasted_iota(jnp.int32, (KEY_BLOCK, KEY_BLOCK), 1)
    upper = jnp.where(r <= c, 1.0, 0.0).astype(BF16)
    carry = jnp.zeros((x_ref.shape[0], 1), F32)
    for j in range(n_blocks):
        sl = slice(j * KEY_BLOCK, (j + 1) * KEY_BLOCK)
        x = x_ref[:, sl]
        hi = x.astype(BF16)
        mid, lo = _split2(x - hi.astype(F32))
        f = _dot(hi, upper) + _dot(mid, upper) + _dot(lo, upper) + carry
        o_ref[:, sl] = f * LOG2E
        carry = f[:, KEY_BLOCK - 1:KEY_BLOCK]


def _cumsum_rows(x):
    b, r, tp = x.shape
    spec = pl.BlockSpec((None, r, tp), lambda i: (i, 0, 0))
    return pl.pallas_call(
        functools.partial(_cumsum_kernel, n_blocks=tp // KEY_BLOCK),
        grid=(b,),
        in_specs=[spec],
        out_specs=spec,
        out_shape=jax.ShapeDtypeStruct(x.shape, F32),
        compiler_params=_params(1),
        name="cumsum_logf",
    )(x)


def _block_range(q_start, tq, n_valid):
    n_full = q_start // KEY_BLOCK
    last = jnp.minimum(((q_start + tq + CHUNK - 1) // CHUNK) * CHUNK, n_valid)
    n_total = (last + KEY_BLOCK - 1) // KEY_BLOCK
    return n_full, n_total


def _positions(q_start, tq, j, cols):
    k_pos = j * KEY_BLOCK + lax.broadcasted_iota(jnp.int32, (KEY_BLOCK, cols), 0)
    lane = lax.broadcasted_iota(jnp.int32, (KEY_BLOCK, cols), 1)
    q_pos = q_start + (lane if cols == tq else lane % tq)
    return q_pos, k_pos


def _transpose_blocks(src_ref, dst_ref, width):
    for j in range(dst_ref.shape[0]):
        blk = src_ref[j * KEY_BLOCK:(j + 1) * KEY_BLOCK, :width].astype(F32)
        dst_ref[j] = blk.T.astype(BF16)


def _pipelined_blocks(n_full, produce, consume, store, load, lead):
    f = lambda i: n_full - 1 - i
    n_pairs = jnp.maximum(n_full - 1, 0) // 2
    rest = n_full - 2 * n_pairs

    def both(i, slot):
        consume(f(i), load(slot), overlap=(lambda: produce(f(i + 1)), lambda vals: store(1 - slot, vals)))

    @pl.when(n_full > 0)
    def _():
        lead((lambda: produce(f(0)), lambda vals: store(0, vals)))

    @pl.when(n_full <= 0)
    def _():
        lead(None)

    def pair_body(t, carry):
        both(2 * t, 0)
        both(2 * t + 1, 1)
        return carry

    lax.fori_loop(0, n_pairs, pair_body, 0)

    @pl.when((n_full > 0) & (rest == 1))
    def _():
        consume(f(2 * n_pairs), load(0))

    @pl.when((n_full > 0) & (rest == 2))
    def _():
        both(2 * n_pairs, 0)
        consume(f(2 * n_pairs + 1), load(1))


def _finite_or_zero(m):
    return jnp.where(m == -jnp.inf, 0.0, m)


def _head_masks(n_heads):
    lane = lax.broadcasted_iota(jnp.int32, (1, n_heads * HEAD_DIM), 1)
    return [(lane >= hh * HEAD_DIM) & (lane < (hh + 1) * HEAD_DIM) for hh in range(n_heads)]


def _sb_kernel(q_ref, k_ref, vt_ref, o_ref, qm_ref, z_ref, x_ref, acc_ref, c_ref, *, tq, q_pos0, n_valid):
    q_start = q_pos0 + pl.program_id(1) * tq
    n_full, n_total = _block_range(q_start, tq, n_valid)
    hmask = _head_masks(H_A)
    q = q_ref[...]
    for hh in range(H_A):
        qm_ref[hh] = jnp.where(hmask[hh], q, jnp.zeros_like(q))
    acc_ref[...] = jnp.zeros(acc_ref.shape, F32)
    c_ref[...] = jnp.zeros(c_ref.shape, F32)
    half = KEY_BLOCK // 2
    r = lax.broadcasted_iota(jnp.int32, (half, KEY_BLOCK), 0)
    c = lax.broadcasted_iota(jnp.int32, (half, KEY_BLOCK), 1)
    neg_upper2 = jnp.where(c % half >= r, -1.0, 0.0).astype(BF16)
    heads = range(H_A)
    rows = [slice(hh * HEAD_DIM, (hh + 1) * HEAD_DIM) for hh in heads]

    def parts(x):
        hi, lo = _split2(x)
        return jnp.concatenate([hi, lo], axis=0)

    def first_stage(j, mask=None):
        off = pl.multiple_of(j * KEY_BLOCK, KEY_BLOCK)
        kb = k_ref[pl.ds(off, KEY_BLOCK), :]
        zs = [_dot_nt(kb, qm_ref[hh]) for hh in heads]
        sps = [_softplus(z) for z in zs]
        if mask is not None:
            sps = [jnp.where(mask, sp, 0.0) for sp in sps]
        return zs, [parts(sp[:half]) for sp in sps], [parts(sp[half:]) for sp in sps]

    def second_stage(j, vals, overlap=None, mask=None):
        zs, early, late = vals
        vts = [vt_ref[j, rows[hh], :] for hh in heads]
        carry = c_ref[...]
        upcoming = overlap[0]() if overlap else None
        cum_late = [_dot(neg_upper2, late[hh]) + carry[hh:hh + 1, :] for hh in heads]
        cum_early = [_dot(neg_upper2, early[hh]) + cum_late[hh][0:1, :] for hh in heads]
        ws = [jnp.exp(zs[hh] + jnp.concatenate([cum_early[hh], cum_late[hh]], axis=0)) for hh in heads]
        if mask is not None:
            ws = [jnp.where(mask, w, 0.0) for w in ws]
        pvs = [_dot(vts[hh], ws[hh].astype(BF16)) for hh in heads]
        for hh in heads:
            acc_ref[rows[hh], :] += pvs[hh]
            c_ref[hh:hh + 1, :] = cum_early[hh][0:1, :]
        if overlap:
            overlap[1](upcoming)

    def store(slot, vals):
        zs, early, late = vals
        for hh in heads:
            z_ref[slot, hh] = zs[hh]
            x_ref[slot, hh, 0] = early[hh]
            x_ref[slot, hh, 1] = late[hh]

    def load(slot):
        return ([z_ref[slot, hh] for hh in heads], [x_ref[slot, hh, 0] for hh in heads],
                [x_ref[slot, hh, 1] for hh in heads])

    def masked_block(j, overlap=None):
        q_pos, k_pos = _positions(q_start, tq, j, tq)
        mask = k_pos < q_pos
        second_stage(j, first_stage(j, mask), overlap=overlap, mask=mask)

    def masked_body(i, carry):
        masked_block(n_total - 1 - i)
        return carry

    lax.fori_loop(0, n_total - n_full - 1, masked_body, 0)
    _pipelined_blocks(n_full, first_stage, second_stage, store, load, lead=functools.partial(masked_block, n_full))
    o_ref[...] = acc_ref[...].T


def _sb_attention(q, k, vt, q_pos0, n_valid):
    b, tq_all, w = q.shape
    tk = k.shape[1]
    tq = min(tq_all, KEY_BLOCK)
    qspec = pl.BlockSpec((None, tq, w), lambda bi, qi: (bi, qi, 0))
    kspec = pl.BlockSpec((None, tk, w), lambda bi, qi: (bi, 0, 0))
    vspec = pl.BlockSpec((None,) + vt.shape[1:], lambda bi, qi: (bi, 0, 0, 0))
    return pl.pallas_call(
        functools.partial(_sb_kernel, tq=tq, q_pos0=q_pos0, n_valid=n_valid),
        grid=(b, tq_all // tq),
        in_specs=[qspec, kspec, vspec],
        out_specs=qspec,
        out_shape=jax.ShapeDtypeStruct((b, tq_all, w), F32),
        scratch_shapes=[pltpu.VMEM((H_A, tq, w), BF16), pltpu.VMEM((2, H_A, KEY_BLOCK, tq), F32),
                        pltpu.VMEM((2, H_A, 2, KEY_BLOCK, tq), BF16), pltpu.VMEM((w, tq), F32), pltpu.VMEM((8, tq), F32)],
        compiler_params=_params(2),
        name="sb_attention",
    )(q, k, vt)


def _fox_kernel(q_ref, k_ref, vt_ref, fq_ref, fk_ref, o_ref, qm_ref, z_ref, acc_ref, m_ref, l_ref, *, tq, q_pos0, n_valid):
    q_start = q_pos0 + pl.program_id(1) * tq
    n_full, n_total = _block_range(q_start, tq, n_valid)
    hmask = _head_masks(H_C)
    q = q_ref[...]
    for hh in range(H_C):
        qm_ref[hh] = jnp.where(hmask[hh], q, jnp.zeros_like(q))
    acc_ref[...] = jnp.zeros(acc_ref.shape, F32)
    m_ref[...] = jnp.full(m_ref.shape, -jnp.inf, F32)
    l_ref[...] = jnp.zeros(l_ref.shape, F32)

    heads = range(H_C)
    rows = [slice(hh * HEAD_DIM, (hh + 1) * HEAD_DIM) for hh in heads]

    def scores(j):
        off = pl.multiple_of(j * KEY_BLOCK, KEY_BLOCK)
        kb = k_ref[pl.ds(off, KEY_BLOCK), :]
        return [_dot_nt(kb, qm_ref[hh]) for hh in heads]

    def step(j, qk, overlap=None, mask=None):
        off = pl.multiple_of(j * KEY_BLOCK, KEY_BLOCK)
        fk = fk_ref[pl.ds(off, KEY_BLOCK), :]
        vts = [vt_ref[j, rows[hh], :] for hh in heads]
        fq, m_all, l_all = fq_ref[...], m_ref[...], l_ref[...]
        accs = [acc_ref[rows[hh], :] for hh in heads]
        upcoming = overlap[0]() if overlap else None
        zs = [qk[hh] + (fq[hh:hh + 1, :] - fk[:, hh:hh + 1]) for hh in heads]
        if mask is not None:
            zs = [jnp.where(mask, z, -jnp.inf) for z in zs]
        m_new = [jnp.maximum(m_all[hh:hh + 1, :], jnp.max(zs[hh], axis=0, keepdims=True)) for hh in heads]
        m_use = [_finite_or_zero(m) for m in m_new]
        ps = [jnp.exp2(zs[hh] - m_use[hh]) for hh in heads]
        alpha = [jnp.exp2(m_all[hh:hh + 1, :] - m_use[hh]) for hh in heads]
        l_new = [alpha[hh] * l_all[hh:hh + 1, :] + jnp.sum(ps[hh], axis=0, keepdims=True) for hh in heads]
        pvs = [_dot(vts[hh], ps[hh].astype(BF16)) for hh in heads]
        for hh in heads:
            m_ref[hh:hh + 1, :] = m_new[hh]
            l_ref[hh:hh + 1, :] = l_new[hh]
            acc_ref[rows[hh], :] = accs[hh] * alpha[hh] + pvs[hh]
        if overlap:
            overlap[1](upcoming)

    def store(slot, qk):
        for hh in heads:
            z_ref[slot, hh] = qk[hh]

    def load(slot):
        return [z_ref[slot, hh] for hh in heads]

    def masked_block(j, overlap=None):
        q_pos, k_pos = _positions(q_start, tq, j, tq)
        step(j, scores(j), overlap=overlap, mask=k_pos <= q_pos)

    def masked_body(i, carry):
        masked_block(n_total - 1 - i)
        return carry

    lax.fori_loop(0, n_total - n_full - 1, masked_body, 0)
    _pipelined_blocks(n_full, scores, step, store, load, lead=functools.partial(masked_block, n_full))
    for hh in range(H_C):
        rows = slice(hh * HEAD_DIM, (hh + 1) * HEAD_DIM)
        acc_ref[rows, :] = acc_ref[rows, :] * (1.0 / l_ref[hh:hh + 1, :])
    o_ref[...] = acc_ref[...].T


def _fox_attention(q, k, vt, fq, fk, q_pos0, n_valid):
    b, tq_all, w = q.shape
    tk = k.shape[1]
    tq = min(tq_all, KEY_BLOCK)
    qspec = pl.BlockSpec((None, tq, w), lambda bi, qi: (bi, qi, 0))
    kspec = pl.BlockSpec((None, tk, w), lambda bi, qi: (bi, 0, 0))
    vspec = pl.BlockSpec((None,) + vt.shape[1:], lambda bi, qi: (bi, 0, 0, 0))
    fqspec = pl.BlockSpec((None, F_ROWS, tq), lambda bi, qi: (bi, 0, qi))
    fkspec = pl.BlockSpec((None, tk, F_ROWS), lambda bi, qi: (bi, 0, 0))
    return pl.pallas_call(
        functools.partial(_fox_kernel, tq=tq, q_pos0=q_pos0, n_valid=n_valid),
        grid=(b, tq_all // tq),
        in_specs=[qspec, kspec, vspec, fqspec, fkspec],
        out_specs=qspec,
        out_shape=jax.ShapeDtypeStruct((b, tq_all, w), F32),
        scratch_shapes=[pltpu.VMEM((H_C, tq, w), BF16), pltpu.VMEM((2, H_C, KEY_BLOCK, tq), F32),
                        pltpu.VMEM((w, tq), F32), pltpu.VMEM((8, tq), F32), pltpu.VMEM((8, tq), F32)],
        compiler_params=_params(2),
        name="fox_attention",
    )(q, k, vt, fq, fk)


def _mla_kernel(q_ref, kl_ref, wuvt_ref, o_ref, ct_ref, z_ref, acc_ref, m_ref, l_ref, *, tq, q_pos0, n_valid):
    @pl.when(pl.program_id(1) == 0)
    def _():
        _transpose_blocks(kl_ref, ct_ref, KV_LORA)

    q_start = q_pos0 + pl.program_id(1) * tq
    n_full, n_total = _block_range(q_start, tq, n_valid)
    cols = H_B * tq
    qs = q_ref[...].reshape(cols, 2 * LANES)
    acc_ref[...] = jnp.zeros(acc_ref.shape, F32)
    m_ref[...] = jnp.full(m_ref.shape, -jnp.inf, F32)
    l_ref[...] = jnp.zeros(l_ref.shape, F32)

    group = KEY_BLOCK
    groups = [slice(g * group, (g + 1) * group) for g in range(cols // group)]

    gs = range(len(groups))

    def scores(j):
        off = pl.multiple_of(j * KEY_BLOCK, KEY_BLOCK)
        kb = kl_ref[pl.ds(off, KEY_BLOCK), :]
        return [_dot_nt(kb, qs[g]) for g in groups]

    def step(j, zs, overlap=None):
        ct = ct_ref[j]
        m_all, l_all = m_ref[...], l_ref[...]
        accs = [acc_ref[:, g] for g in groups]
        upcoming = overlap[0]() if overlap else None
        m_new = [jnp.maximum(m_all[:, groups[gi]], jnp.max(zs[gi], axis=0, keepdims=True)) for gi in gs]
        m_use = [_finite_or_zero(m) for m in m_new]
        ps = [jnp.exp2(zs[gi] - m_use[gi]) for gi in gs]
        alpha = [jnp.exp2(m_all[:, groups[gi]] - m_use[gi]) for gi in gs]
        l_new = [alpha[gi] * l_all[:, groups[gi]] + jnp.sum(ps[gi], axis=0, keepdims=True) for gi in gs]
        pvs = [_dot(ct, ps[gi].astype(BF16)) for gi in gs]
        for gi, g in enumerate(groups):
            m_ref[:, g] = m_new[gi]
            l_ref[:, g] = l_new[gi]
            acc_ref[:, g] = accs[gi] * alpha[gi] + pvs[gi]
        if overlap:
            overlap[1](upcoming)

    def store(slot, zs):
        for gi, g in enumerate(groups):
            z_ref[slot, :, g] = zs[gi]

    def load(slot):
        return [z_ref[slot, :, g] for g in groups]

    def masked_block(j, overlap=None):
        q_pos, k_pos = _positions(q_start, tq, j, group)
        mask = (k_pos // CHUNK <= q_pos // CHUNK) & (k_pos < n_valid)
        step(j, [jnp.where(mask, z, -jnp.inf) for z in scores(j)], overlap=overlap)

    def masked_body(i, carry):
        masked_block(n_total - 1 - i)
        return carry

    lax.fori_loop(0, n_total - n_full - 1, masked_body, 0)
    _pipelined_blocks(n_full, scores, step, store, load, lead=functools.partial(masked_block, n_full))

    lat = (acc_ref[...] * (1.0 / l_ref[...])).astype(BF16)
    heads = [_dot(wuvt_ref[hh], lat[:, hh * tq:(hh + 1) * tq]) for hh in range(H_B)]
    o_ref[...] = jnp.concatenate(heads, axis=0).T


def _mla_attention(qm, kl, wuvt, layer, tq_all, q_pos0, n_valid):
    b, tk, w = kl.shape
    tq = KEY_BLOCK if tq_all % KEY_BLOCK == 0 else LANES
    cols = H_B * tq
    return pl.pallas_call(
        functools.partial(_mla_kernel, tq=tq, q_pos0=q_pos0, n_valid=n_valid),
        grid=(b, tq_all // tq),
        in_specs=[pl.BlockSpec((H_B, None, tq, w), lambda bi, qi: (0, bi, qi, 0)),
                  pl.BlockSpec((None, tk, w), lambda bi, qi: (bi, 0, 0)),
                  _layer_spec(wuvt, layer)],
        out_specs=pl.BlockSpec((None, tq, W_B), lambda bi, qi: (bi, qi, 0)),
        out_shape=jax.ShapeDtypeStruct((b, tq_all, W_B), F32),
        scratch_shapes=[pltpu.VMEM((tk // KEY_BLOCK, KV_LORA, KEY_BLOCK), BF16), pltpu.VMEM((2, KEY_BLOCK, cols), F32),
                        pltpu.VMEM((KV_LORA, cols), F32), pltpu.VMEM((1, cols), F32), pltpu.VMEM((1, cols), F32)],
        compiler_params=_params(2),
        name="mla_attention",
    )(qm, kl, wuvt)


def _post_kernel(h_ref, oa_ref, ob_ref, oc_ref, p_ref, ggrp_ref, wout_ref, gmix_ref,
                 gpre_ref, gpost_ref, wgu_ref, wdown_ref, gple_pre_ref, wgate_ref, wproj_ref, gple_post_ref, o_ref,
                 *, d_ff, chunks):
    m = jnp.zeros(h_ref.shape, F32)
    c0 = 0
    for o_grp in (oa_ref, ob_ref, oc_ref):
        c1 = c0 + o_grp.shape[1]
        m = m + _dot(_rms(o_grp[...], ggrp_ref[:, c0:c1]).astype(BF16), wout_ref[c0:c1, :])
        c0 = c1
    h = h_ref[...] + _rms(m, gmix_ref[...])
    h = _ffn_rows(h, gpre_ref, gpost_ref, wgu_ref, wdown_ref, d_ff, chunks)
    gate = jax.nn.sigmoid(_dot(_rms(h, gple_pre_ref[...]).astype(BF16), wgate_ref[...]))
    e = _dot(p_ref[...].astype(BF16), wproj_ref[...]) * gate
    o_ref[...] = h + _rms(e, gple_post_ref[...])


def _post(h, oa, ob, oc, p, layer, consts):
    n, d = h.shape
    tm = _row_tile(n)
    row = lambda w: pl.BlockSpec((tm, w), lambda i: (i, 0))
    p_spec = pl.BlockSpec((None, tm, p.shape[2]), lambda i: (layer, i, 0))
    d_ff = consts[6].shape[1]
    return pl.pallas_call(
        functools.partial(_post_kernel, d_ff=d_ff, chunks=_ffn_chunks(d_ff)),
        grid=(n // tm,),
        in_specs=[row(d), row(W_A), row(W_B), row(W_C), p_spec] + [_layer_spec(c, layer) for c in consts],
        out_specs=row(d),
        out_shape=jax.ShapeDtypeStruct((n, d), F32),
        compiler_params=_params(1),
        name="post",
    )(h, oa, ob, oc, p, *consts)


def _rope_tables(pos):
    half = MLA_ROPE // 2
    inv = ROPE_THETA ** (-jnp.arange(half, dtype=F32) / half)
    ang = pos.astype(F32)[:, None] * inv[None, :]
    cos, sin = jnp.cos(ang), jnp.sin(ang)
    pad = jnp.zeros((pos.shape[0], LANES - MLA_ROPE), F32)
    return jnp.concatenate([cos, cos, pad], axis=1), jnp.concatenate([-sin, sin, pad], axis=1)


def _prep_weights(weights):
    (g_ff1_pre, g_ff1_post, w_ff1_gu, w_ff1_down, g_mix_pre, g_mix_post, w_in, b_f, g_bq, g_bkv, w_uq, w_ukv,
     g_grp, w_out, g_ff2_pre, g_ff2_post, w_ff2_gu, w_ff2_down, g_ple_pre, w_ple_gate, w_ple_proj, g_ple_post) = weights
    depth, d = w_in.shape[:2]
    half = MLA_ROPE // 2
    row = lambda g: g.reshape(depth, 1, -1).astype(F32)
    bf = lambda w: w.astype(BF16)
    c_kr = 3 * W_A + Q_LORA + KV_LORA
    c_c = c_kr + MLA_ROPE
    c_f = c_c + 3 * W_C
    w_in = bf(w_in)
    kr = w_in[:, :, c_kr:c_kr + MLA_ROPE]
    kr_sw = jnp.concatenate([kr[:, :, half:], kr[:, :, :half]], axis=2)
    zpad = jnp.zeros((depth, d, LANES - MLA_ROPE), BF16)
    win_p = jnp.concatenate([w_in[:, :, :c_kr], kr, zpad, kr_sw, zpad, w_in[:, :, c_c:c_f], w_in[:, :, c_f:],
                             jnp.zeros((depth, d, LANES - H_C), BF16)], axis=2)
    assert win_p.shape[2] == IN_COLS_P
    bf_p = jnp.pad(b_f.astype(F32), ((0, 0), (0, LANES - H_C))).reshape(depth, 1, LANES)
    wkvt = jnp.transpose(jnp.concatenate([w_in[:, :, W_A:3 * W_A], w_in[:, :, c_c + W_C:c_f]], axis=2), (0, 2, 1))

    wq4 = bf(w_uq).reshape(depth, Q_LORA, H_B, MLA_NOPE + MLA_ROPE)
    wkv4 = bf(w_ukv).reshape(depth, KV_LORA, H_B, MLA_NOPE + MLA_V)
    wq_n = jnp.transpose(wq4[..., :MLA_NOPE], (0, 2, 1, 3))
    wk_n = jnp.transpose(wkv4[..., :MLA_NOPE], (0, 2, 1, 3))
    wcomb = _wcomb(wq_n, wk_n)
    x1 = wq4[..., MLA_NOPE:MLA_NOPE + half]
    x2 = wq4[..., MLA_NOPE + half:]
    zq = jnp.zeros((depth, Q_LORA, H_B, LANES - MLA_ROPE), BF16)
    wqr = jnp.concatenate([jnp.concatenate([x1, x2, zq], axis=3).reshape(depth, Q_LORA, H_B * LANES),
                           jnp.concatenate([x2, x1, zq], axis=3).reshape(depth, Q_LORA, H_B * LANES)], axis=2)
    wuvt = jnp.transpose(wkv4[..., MLA_NOPE:], (0, 2, 3, 1))

    return dict(
        ff1=(row(g_ff1_pre), row(g_ff1_post), bf(w_ff1_gu), bf(w_ff1_down)),
        inproj=(row(g_mix_pre), win_p, bf_p, row(g_bq), row(g_bkv), wcomb, wqr, wkvt),
        wuvt=wuvt,
        post=(row(g_grp), bf(w_out), row(g_mix_post),
              row(g_ff2_pre), row(g_ff2_post), bf(w_ff2_gu), bf(w_ff2_down),
              row(g_ple_pre), bf(w_ple_gate), bf(w_ple_proj), row(g_ple_post)),
    )


def _pad_keys(a, tk_pad):
    return jnp.pad(a, ((0, 0), (0, tk_pad - a.shape[1])) + ((0, 0),) * (a.ndim - 2))


def _forget_sums(logf_rows, tk_pad):
    _, heads, tk = logf_rows.shape
    return _cumsum_rows(jnp.pad(logf_rows, ((0, 0), (0, F_ROWS - heads), (0, tk_pad - tk))))


def _layer(h, p, seq_len, q_pos0, past, layer, lp, tables, prev_states):
    n, d = h.shape
    b = n // seq_len
    h = _ffn(h, layer, *lp["ff1"])
    (ka_st, va_st, ckv_st, kr_st, kc_st, vc_st, lf_st,
     qa_b, ka_b, va_b, qm_b, kl_b, qc_b, kc_b, vc_b), feature_major = _inproj(h, seq_len, layer, prev_states,
                                                                                *lp["inproj"], *tables)
    if feature_major:
        state = (ka_st, va_st, ckv_st, kr_st, kc_st, vc_st, lf_st)
        lf_rows_new = lf_st[layer]
    else:
        state = (ka_st.reshape(b, seq_len, H_A, HEAD_DIM), va_st.reshape(b, seq_len, H_A, HEAD_DIM),
                 ckv_st.reshape(b, seq_len, KV_LORA), kr_st.reshape(b, seq_len, MLA_ROPE),
                 kc_st.reshape(b, seq_len, H_C, HEAD_DIM), vc_st.reshape(b, seq_len, H_C, HEAD_DIM),
                 lf_st.reshape(b, seq_len, H_C))
        lf_rows_new = jnp.transpose(state[6], (0, 2, 1))
    seq3 = lambda a: a.reshape(b, seq_len, a.shape[-1])
    tq_pad = -(-seq_len // LANES) * LANES
    to_blocks = lambda v: jnp.transpose(v.reshape(b, v.shape[1] // KEY_BLOCK, KEY_BLOCK, v.shape[2]), (0, 1, 3, 2))
    if past is None:
        n_valid = seq_len
        tk_pad = -(-tq_pad // KEY_BLOCK) * KEY_BLOCK
        ka_all, kl_all, kc_all = (_pad_keys(seq3(a), tk_pad) for a in (ka_b, kl_b, kc_b))
        if feature_major:
            va_all, vc_all = (v.reshape(b, seq_len // KEY_BLOCK, v.shape[1], KEY_BLOCK) for v in (va_b, vc_b))
        else:
            va_all, vc_all = (to_blocks(_pad_keys(seq3(v), tk_pad)) for v in (va_b, vc_b))
        lf_rows = lf_rows_new
    else:
        assert not feature_major
        pa_k, pa_v, pb_ckv, pb_kr, pc_k, pc_v, pc_lf = past
        past_len = pa_k.shape[1]
        n_valid = past_len + seq_len
        tk_pad = -(-(past_len + tq_pad) // KEY_BLOCK) * KEY_BLOCK
        join = lambda c, new: _pad_keys(jnp.concatenate([c.reshape(b, past_len, -1).astype(BF16), seq3(new)], axis=1), tk_pad)
        ka_all, kc_all = join(pa_k, ka_b), join(pc_k, kc_b)
        va_all, vc_all = to_blocks(join(pa_v, va_b)), to_blocks(join(pc_v, vc_b))
        kl_past = jnp.concatenate([pb_ckv, pb_kr, jnp.zeros((b, past_len, LANES - MLA_ROPE), F32)], axis=-1)
        kl_all = join(kl_past, kl_b)
        lf_rows = jnp.concatenate([jnp.transpose(pc_lf, (0, 2, 1)), lf_rows_new], axis=2)
    f_rows = _forget_sums(lf_rows, tk_pad)
    f_q = f_rows[:, :, q_pos0:q_pos0 + tq_pad]
    f_k = jnp.transpose(f_rows, (0, 2, 1))
    pad_q = lambda a: jnp.pad(a, ((0, 0),) * (a.ndim - 2) + ((0, tq_pad - seq_len), (0, 0)))

    oa = _sb_attention(pad_q(seq3(qa_b)), ka_all, va_all, q_pos0, n_valid)[:, :seq_len]
    ob = _mla_attention(pad_q(qm_b.reshape(H_B, b, seq_len, 2 * LANES)), kl_all, lp["wuvt"], layer, tq_pad, q_pos0,
                        n_valid)[:, :seq_len]
    oc = _fox_attention(pad_q(seq3(qc_b)), kc_all, vc_all, f_q, f_k, q_pos0, n_valid)[:, :seq_len]
    flat = lambda a: a.reshape(n, a.shape[-1])
    h = _post(h, flat(oa), flat(ob), flat(oc), p, layer, lp["post"])
    return h, state, feature_major


def _trunk(x, p, q_pos0, caches, lp, depth):
    b, t, d = x.shape
    tables = _rope_tables(q_pos0 + jnp.arange(t, dtype=jnp.int32))
    h = x.reshape(b * t, d)
    p = p.reshape(depth, b * t, -1)
    states, st, stacked_mode = [], None, False
    for i in range(depth):
        past = None if caches is None else [c[i] for c in caches]
        h, st, stacked_mode = _layer(h, p, t, q_pos0, past, i, lp, tables, st if stacked_mode else None)
        states.append(st)
    if stacked_mode:
        heads5 = lambda a, nh: jnp.transpose(a.reshape(depth, b, nh, HEAD_DIM, t), (0, 1, 4, 2, 3))
        swap = lambda a: jnp.transpose(a, (0, 1, 3, 2))
        ka, va, ckv, kr, kc, vc, lf = st
        stacked = [heads5(ka, H_A), heads5(va, H_A), ckv.reshape(depth, b, t, KV_LORA), swap(kr),
                   heads5(kc, H_C), heads5(vc, H_C), swap(lf)]
    else:
        stacked = [jnp.stack([s[j] for s in states]) for j in range(N_STATE)]
    return h.reshape(b, t, d), stacked


def kernel(x_prompt, x_sample, p_prompt, p_sample, cache_a_k, cache_a_v, cache_b_ckv, cache_b_krope, cache_c_k, cache_c_v, cache_c_logf, g_ff1_pre, g_ff1_post, w_ff1_gu, w_ff1_down, g_mix_pre, g_mix_post, w_in, b_f, g_bq, g_bkv, w_uq, w_ukv, g_grp, w_out, g_ff2_pre, g_ff2_post, w_ff2_gu, w_ff2_down, g_ple_pre, w_ple_gate, w_ple_proj, g_ple_post):
    weights = (g_ff1_pre, g_ff1_post, w_ff1_gu, w_ff1_down, g_mix_pre, g_mix_post, w_in, b_f,
               g_bq, g_bkv, w_uq, w_ukv, g_grp, w_out, g_ff2_pre, g_ff2_post, w_ff2_gu, w_ff2_down,
               g_ple_pre, w_ple_gate, w_ple_proj, g_ple_post)
    depth = w_in.shape[0]
    lp = _prep_weights(weights)
    y_prompt, sp = _trunk(x_prompt, p_prompt, 0, None, lp, depth)
    caches = (cache_a_k, cache_a_v, cache_b_ckv, cache_b_krope, cache_c_k, cache_c_v, cache_c_logf)
    y_sample, ss = _trunk(x_sample, p_sample, cache_a_k.shape[2], caches, lp, depth)
    return (y_prompt, y_sample, *sp, *ss)
```

```python
import functools
import math

import jax
import jax.numpy as jnp
from jax import lax
from jax.experimental import pallas as pl
from jax.experimental.pallas import tpu as pltpu

CHUNK = 64
HEAD_DIM = 64
H_A = 4
H_B = 8
H_C = 4
W_A = H_A * HEAD_DIM
MLA_NOPE = 64
MLA_ROPE = 32
MLA_V = 64
W_B = H_B * MLA_V
W_C = H_C * HEAD_DIM
Q_LORA = 256
KV_LORA = 128
ROPE_THETA = 10000.0
EPS = 1e-6
FFN_RES = 0.5
SB_SCALE = HEAD_DIM ** -0.5
MLA_SCALE = (MLA_NOPE + MLA_ROPE) ** -0.5
FOX_SCALE = HEAD_DIM ** -0.5
LOG2E = math.log2(math.e)

LANES = 128
KEY_BLOCK = 256
F_ROWS = 16
VMEM_LIMIT = 56 * 1024 * 1024

COL_A = 0
COL_CQ = COL_A + 3 * W_A
COL_CKV = COL_CQ + Q_LORA
COL_KRA = COL_CKV + KV_LORA
COL_KRB = COL_KRA + LANES
COL_C = COL_KRB + LANES
COL_F = COL_C + 3 * W_C
IN_COLS_P = COL_F + LANES

BF16 = jnp.bfloat16
F32 = jnp.float32


def _dot(a, b):
    return jnp.dot(a, b, preferred_element_type=F32)


def _dot_nt(a, b):
    return lax.dot_general(a, b, (((1,), (1,)), ((), ())), preferred_element_type=F32)


def _rms(x, g):
    ms = jnp.mean(x * x, axis=-1, keepdims=True)
    return x * lax.rsqrt(ms + EPS) * g


def _log_sigmoid(x):
    return jnp.minimum(x, 0.0) - jnp.log(1.0 + jnp.exp(-jnp.abs(x)))


def _softplus(x):
    return jnp.maximum(x, 0.0) + jnp.log(1.0 + jnp.exp2(jnp.abs(x) * (-LOG2E)))


def _split2(x):
    hi = x.astype(BF16)
    lo = (x - hi.astype(F32)).astype(BF16)
    return hi, lo


def _layer_spec(a, layer):
    idx = (layer,) + (0,) * (a.ndim - 1)
    return pl.BlockSpec((None,) + a.shape[1:], lambda *_: idx, pipeline_mode=pl.Buffered(1))


def _params(n_axes):
    return pltpu.CompilerParams(dimension_semantics=("arbitrary",) * n_axes, vmem_limit_bytes=VMEM_LIMIT)


def _row_tile(n):
    for tm in (512, 256):
        if n % tm == 0:
            return tm
    return n


def _ffn_rows(h, gpre_ref, gpost_ref, wgu_ref, wdown_ref, d_ff, chunks):
    xn = _rms(h, gpre_ref[...]).astype(BF16)
    acc = jnp.zeros(h.shape, F32)
    for c0, c1 in chunks:
        g = _dot(xn, wgu_ref[:, c0:c1])
        u = _dot(xn, wgu_ref[:, d_ff + c0:d_ff + c1])
        a = (g * jax.nn.sigmoid(g) * u).astype(BF16)
        acc = acc + _dot(a, wdown_ref[c0:c1, :])
    return h + FFN_RES * _rms(acc, gpost_ref[...])


def _ffn_chunks(d_ff):
    step = 4 * KEY_BLOCK
    return tuple((c, min(c + step, d_ff)) for c in range(0, d_ff, step))


def _ffn_kernel(h_ref, gpre_ref, gpost_ref, wgu_ref, wdown_ref, o_ref, *, d_ff, chunks):
    o_ref[...] = _ffn_rows(h_ref[...], gpre_ref, gpost_ref, wgu_ref, wdown_ref, d_ff, chunks)


def _ffn(h, layer, gpre, gpost, wgu, wdown):
    n, d = h.shape
    d_ff = wdown.shape[1]
    consts = [gpre, gpost, wgu, wdown]
    tm = _row_tile(n)
    chunks = _ffn_chunks(d_ff)
    row = pl.BlockSpec((tm, d), lambda i: (i, 0))
    return pl.pallas_call(
        functools.partial(_ffn_kernel, d_ff=d_ff, chunks=chunks),
        grid=(n // tm,),
        in_specs=[row] + [_layer_spec(c, layer) for c in consts],
        out_specs=row,
        out_shape=jax.ShapeDtypeStruct((n, d), F32),
        compiler_params=_params(1),
        name="ffn",
    )(h, gpre, gpost, wgu, wdown)


def _wcomb_kernel(wq_ref, wk_ref, o_ref):
    kl = wk_ref.shape[1]
    for hh in range(wq_ref.shape[0]):
        o_ref[:, hh * kl:(hh + 1) * kl] = _dot_nt(wq_ref[hh], wk_ref[hh]).astype(BF16)


def _wcomb(wq_n, wk_n):
    depth, nh, ql, dn = wq_n.shape
    kl = wk_n.shape[2]
    return pl.pallas_call(
        _wcomb_kernel,
        grid=(depth,),
        in_specs=[pl.BlockSpec((None, nh, ql, dn), lambda i: (i, 0, 0, 0)),
                  pl.BlockSpec((None, nh, kl, dn), lambda i: (i, 0, 0, 0))],
        out_specs=pl.BlockSpec((None, ql, nh * kl), lambda i: (i, 0, 0)),
        out_shape=jax.ShapeDtypeStruct((depth, ql, nh * kl), BF16),
        compiler_params=_params(1),
        name="wcomb",
    )(wq_n, wk_n)


def _inproj_kernel(*refs, feature_major, n_alias, fill_layers):
    (h_ref, gpre_ref, win_ref, bf_ref, gbq_ref, gbkv_ref, wcomb_ref, wqr_ref, wkvt_ref, cos_ref, sin_ref) = refs[:11]
    (ka_st, va_st, ckv_st, kr_st, kc_st, vc_st, lf_st,
     qa_b, ka_b, va_b, qm_b, kl_b, qc_b, kc_b, vc_b) = refs[11 + n_alias:]

    def put(st, val):
        if fill_layers:
            for slab in range(fill_layers):
                st[slab] = val
        else:
            st[...] = val

    xn = _rms(h_ref[...], gpre_ref[...]).astype(BF16)
    proj = _dot(xn, win_ref[...])
    cos = cos_ref[...]
    sin = sin_ref[...]

    qa_b[...] = (proj[:, COL_A:COL_A + W_A] * SB_SCALE).astype(BF16)
    qc_b[...] = (proj[:, COL_C:COL_C + W_C] * (FOX_SCALE * LOG2E)).astype(BF16)
    ka = proj[:, COL_A + W_A:COL_A + 2 * W_A]
    kc = proj[:, COL_C + W_C:COL_C + 2 * W_C]
    ka_b[...] = ka.astype(BF16)
    kc_b[...] = kc.astype(BF16)
    lf = _log_sigmoid(proj[:, COL_F:COL_F + LANES] + bf_ref[...])
    ckv = _rms(proj[:, COL_CKV:COL_CKV + KV_LORA], gbkv_ref[...])
    put(ckv_st, ckv)
    kr = proj[:, COL_KRA:COL_KRA + LANES] * cos + proj[:, COL_KRB:COL_KRB + LANES] * sin
    kl_b[:, :KV_LORA] = ckv.astype(BF16)
    kl_b[:, KV_LORA:] = kr.astype(BF16)

    if feature_major:
        kvt = _dot_nt(wkvt_ref[...], xn)
        for idx, st in enumerate((ka_st, va_st, kc_st, vc_st)):
            put(st, kvt[idx * W_A:(idx + 1) * W_A])
        for r in range(va_b.shape[0]):
            cols = slice(r * KEY_BLOCK, (r + 1) * KEY_BLOCK)
            va_b[r] = kvt[W_A:2 * W_A, cols].astype(BF16)
            vc_b[r] = kvt[3 * W_A:4 * W_A, cols].astype(BF16)
        put(kr_st, kr.T[:MLA_ROPE])
        put(lf_st, lf.T[:H_C])
    else:
        va = proj[:, COL_A + 2 * W_A:COL_A + 3 * W_A]
        vc = proj[:, COL_C + 2 * W_C:COL_C + 3 * W_C]
        ka_st[...] = ka
        va_st[...] = va
        kc_st[...] = kc
        vc_st[...] = vc
        va_b[...] = va.astype(BF16)
        vc_b[...] = vc.astype(BF16)
        kr_st[...] = kr[:, :MLA_ROPE]
        lf_st[...] = lf[:, :H_C]

    cqn = _rms(proj[:, COL_CQ:COL_CQ + Q_LORA], gbq_ref[...]).astype(BF16)
    qlat = _dot(cqn, wcomb_ref[...])
    qr = _dot(cqn, wqr_ref[...])
    half = H_B * LANES
    for hh in range(H_B):
        sl = slice(hh * LANES, (hh + 1) * LANES)
        rope = qr[:, sl] * cos + qr[:, half + hh * LANES:half + (hh + 1) * LANES] * sin
        qm_b[hh, :, :KV_LORA] = (qlat[:, sl] * (MLA_SCALE * LOG2E)).astype(BF16)
        qm_b[hh, :, KV_LORA:] = (rope * (MLA_SCALE * LOG2E)).astype(BF16)


N_STATE = 7


def _inproj(h, seq_len, layer, prev_states, gpre, win, bf, gbq, gbkv, wcomb, wqr, wkvt, cos_t, sin_t):
    n, d = h.shape
    b = n // seq_len
    depth = win.shape[0]
    tm = _row_tile(n)
    row = lambda w: pl.BlockSpec((tm, w), lambda i: (i, 0))
    consts = [gpre, win, bf, gbq, gbkv, wcomb, wqr, wkvt]
    feature_major = seq_len % tm == 0 and tm % KEY_BLOCK == 0
    fill_all = feature_major and prev_states is None
    if feature_major:
        per_seq = seq_len // tm
        tab = pl.BlockSpec((tm, LANES), lambda i: (i % per_seq, 0))
        lead, at = (depth, 0) if fill_all else (None, layer)
        narrow = lambda w: (pl.BlockSpec((lead, None, w, tm), lambda i: (at, i // per_seq, 0, i % per_seq)),
                            (depth, b, w, seq_len), F32)
        latent = (pl.BlockSpec((lead, tm, KV_LORA), lambda i: (at, i, 0)), (depth, n, KV_LORA), F32)
        vals = lambda w: (pl.BlockSpec((tm // KEY_BLOCK, w, KEY_BLOCK), lambda i: (i, 0, 0)),
                          (n // KEY_BLOCK, w, KEY_BLOCK), BF16)
    else:
        cos_t, sin_t = (jnp.tile(t, (b, 1)) for t in (cos_t, sin_t))
        tab = row(LANES)
        narrow = lambda w: (row(w), (n, w), F32)
        latent = narrow(KV_LORA)
        vals = lambda w: (row(w), (n, w), BF16)
    outs = [
        narrow(W_A), narrow(W_A), latent, narrow(MLA_ROPE), narrow(W_C), narrow(W_C), narrow(H_C),
        (row(W_A), (n, W_A), BF16), (row(W_A), (n, W_A), BF16), vals(W_A),
        (pl.BlockSpec((H_B, tm, 2 * LANES), lambda i: (0, i, 0)), (H_B, n, 2 * LANES), BF16),
        (row(2 * LANES), (n, 2 * LANES), BF16),
        (row(W_C), (n, W_C), BF16), (row(W_C), (n, W_C), BF16), vals(W_C),
    ]
    ins = [h, *consts, cos_t, sin_t]
    in_specs = [row(d)] + [_layer_spec(c, layer) for c in consts] + [tab, tab]
    aliases = {}
    if feature_major and prev_states is not None:
        aliases = {len(ins) + s: s for s in range(N_STATE)}
        ins = ins + list(prev_states)
        in_specs = in_specs + [pl.BlockSpec(memory_space=pl.ANY)] * N_STATE
    res = pl.pallas_call(
        functools.partial(_inproj_kernel, feature_major=feature_major, n_alias=len(aliases),
                          fill_layers=depth if fill_all else 0),
        grid=(n // tm,),
        in_specs=in_specs,
        out_specs=[o[0] for o in outs],
        out_shape=[jax.ShapeDtypeStruct(o[1], o[2]) for o in outs],
        input_output_aliases=aliases,
        compiler_params=_params(1),
        name="inproj",
    )(*ins)
    return res, feature_major


def _cumsum_kernel(x_ref, o_ref, *, n_blocks):
    r = lax.broadcasted_iota(jnp.int32, (KEY_BLOCK, KEY_BLOCK), 0)
    c = lax.broadcasted_iota(jnp.int32, (KEY_BLOCK, KEY_BLOCK), 1)
    upper = jnp.where(r <= c, 1.0, 0.0).astype(BF16)
    carry = jnp.zeros((x_ref.shape[0], 1), F32)
    for j in range(n_blocks):
        sl = slice(j * KEY_BLOCK, (j + 1) * KEY_BLOCK)
        x = x_ref[:, sl]
        hi = x.astype(BF16)
        mid, lo = _split2(x - hi.astype(F32))
        f = _dot(hi, upper) + _dot(mid, upper) + _dot(lo, upper) + carry
        o_ref[:, sl] = f * LOG2E
        carry = f[:, KEY_BLOCK - 1:KEY_BLOCK]


def _cumsum_rows(x):
    b, r, tp = x.shape
    spec = pl.BlockSpec((None, r, tp), lambda i: (i, 0, 0))
    return pl.pallas_call(
        functools.partial(_cumsum_kernel, n_blocks=tp // KEY_BLOCK),
        grid=(b,),
        in_specs=[spec],
        out_specs=spec,
        out_shape=jax.ShapeDtypeStruct(x.shape, F32),
        compiler_params=_params(1),
        name="cumsum_logf",
    )(x)


def _block_range(q_start, tq, n_valid):
    n_full = q_start // KEY_BLOCK
    last = jnp.minimum(((q_start + tq + CHUNK - 1) // CHUNK) * CHUNK, n_valid)
    n_total = (last + KEY_BLOCK - 1) // KEY_BLOCK
    return n_full, n_total


def _positions(q_start, tq, j, cols):
    k_pos = j * KEY_BLOCK + lax.broadcasted_iota(jnp.int32, (KEY_BLOCK, cols), 0)
    lane = lax.broadcasted_iota(jnp.int32, (KEY_BLOCK, cols), 1)
    q_pos = q_start + (lane if cols == tq else lane % tq)
    return q_pos, k_pos


def _transpose_blocks(src_ref, dst_ref, width):
    for j in range(dst_ref.shape[0]):
        blk = src_ref[j * KEY_BLOCK:(j + 1) * KEY_BLOCK, :width].astype(F32)
        dst_ref[j] = blk.T.astype(BF16)


def _pipelined_blocks(n_full, produce, consume, store, load, lead):
    f = lambda i: n_full - 1 - i
    n_pairs = jnp.maximum(n_full - 1, 0) // 2
    rest = n_full - 2 * n_pairs

    def both(i, slot):
        consume(f(i), load(slot), overlap=(lambda: produce(f(i + 1)), lambda vals: store(1 - slot, vals)))

    @pl.when(n_full > 0)
    def _():
        lead((lambda: produce(f(0)), lambda vals: store(0, vals)))

    @pl.when(n_full <= 0)
    def _():
        lead(None)

    def pair_body(t, carry):
        both(2 * t, 0)
        both(2 * t + 1, 1)
        return carry

    lax.fori_loop(0, n_pairs, pair_body, 0)

    @pl.when((n_full > 0) & (rest == 1))
    def _():
        consume(f(2 * n_pairs), load(0))

    @pl.when((n_full > 0) & (rest == 2))
    def _():
        both(2 * n_pairs, 0)
        consume(f(2 * n_pairs + 1), load(1))


def _finite_or_zero(m):
    return jnp.where(m == -jnp.inf, 0.0, m)


def _head_masks(n_heads):
    lane = lax.broadcasted_iota(jnp.int32, (1, n_heads * HEAD_DIM), 1)
    return [(lane >= hh * HEAD_DIM) & (lane < (hh + 1) * HEAD_DIM) for hh in range(n_heads)]


def _sb_kernel(q_ref, k_ref, vt_ref, o_ref, qm_ref, z_ref, x_ref, acc_ref, c_ref, *, tq, q_pos0, n_valid):
    q_start = q_pos0 + pl.program_id(1) * tq
    n_full, n_total = _block_range(q_start, tq, n_valid)
    hmask = _head_masks(H_A)
    q = q_ref[...]
    for hh in range(H_A):
        qm_ref[hh] = jnp.where(hmask[hh], q, jnp.zeros_like(q))
    acc_ref[...] = jnp.zeros(acc_ref.shape, F32)
    c_ref[...] = jnp.zeros(c_ref.shape, F32)
    half = KEY_BLOCK // 2
    r = lax.broadcasted_iota(jnp.int32, (half, KEY_BLOCK), 0)
    c = lax.broadcasted_iota(jnp.int32, (half, KEY_BLOCK), 1)
    neg_upper2 = jnp.where(c % half >= r, -1.0, 0.0).astype(BF16)
    heads = range(H_A)
    rows = [slice(hh * HEAD_DIM, (hh + 1) * HEAD_DIM) for hh in heads]

    def parts(x):
        hi, lo = _split2(x)
        return jnp.concatenate([hi, lo], axis=0)

    def first_stage(j, mask=None):
        off = pl.multiple_of(j * KEY_BLOCK, KEY_BLOCK)
        kb = k_ref[pl.ds(off, KEY_BLOCK), :]
        zs = [_dot_nt(kb, qm_ref[hh]) for hh in heads]
        sps = [_softplus(z) for z in zs]
        if mask is not None:
            sps = [jnp.where(mask, sp, 0.0) for sp in sps]
        return zs, [parts(sp[:half]) for sp in sps], [parts(sp[half:]) for sp in sps]

    def second_stage(j, vals, overlap=None, mask=None):
        zs, early, late = vals
        vts = [vt_ref[j, rows[hh], :] for hh in heads]
        carry = c_ref[...]
        upcoming = overlap[0]() if overlap else None
        cum_late = [_dot(neg_upper2, late[hh]) + carry[hh:hh + 1, :] for hh in heads]
        cum_early = [_dot(neg_upper2, early[hh]) + cum_late[hh][0:1, :] for hh in heads]
        ws = [jnp.exp(zs[hh] + jnp.concatenate([cum_early[hh], cum_late[hh]], axis=0)) for hh in heads]
        if mask is not None:
            ws = [jnp.where(mask, w, 0.0) for w in ws]
        pvs = [_dot(vts[hh], ws[hh].astype(BF16)) for hh in heads]
        for hh in heads:
            acc_ref[rows[hh], :] += pvs[hh]
            c_ref[hh:hh + 1, :] = cum_early[hh][0:1, :]
        if overlap:
            overlap[1](upcoming)

    def store(slot, vals):
        zs, early, late = vals
        for hh in heads:
            z_ref[slot, hh] = zs[hh]
            x_ref[slot, hh, 0] = early[hh]
            x_ref[slot, hh, 1] = late[hh]

    def load(slot):
        return ([z_ref[slot, hh] for hh in heads], [x_ref[slot, hh, 0] for hh in heads],
                [x_ref[slot, hh, 1] for hh in heads])

    def masked_block(j, overlap=None):
        q_pos, k_pos = _positions(q_start, tq, j, tq)
        mask = k_pos < q_pos
        second_stage(j, first_stage(j, mask), overlap=overlap, mask=mask)

    def masked_body(i, carry):
        masked_block(n_total - 1 - i)
        return carry

    lax.fori_loop(0, n_total - n_full - 1, masked_body, 0)
    _pipelined_blocks(n_full, first_stage, second_stage, store, load, lead=functools.partial(masked_block, n_full))
    o_ref[...] = acc_ref[...].T


def _sb_attention(q, k, vt, q_pos0, n_valid):
    b, tq_all, w = q.shape
    tk = k.shape[1]
    tq = min(tq_all, KEY_BLOCK)
    qspec = pl.BlockSpec((None, tq, w), lambda bi, qi: (bi, qi, 0))
    kspec = pl.BlockSpec((None, tk, w), lambda bi, qi: (bi, 0, 0))
    vspec = pl.BlockSpec((None,) + vt.shape[1:], lambda bi, qi: (bi, 0, 0, 0))
    return pl.pallas_call(
        functools.partial(_sb_kernel, tq=tq, q_pos0=q_pos0, n_valid=n_valid),
        grid=(b, tq_all // tq),
        in_specs=[qspec, kspec, vspec],
        out_specs=qspec,
        out_shape=jax.ShapeDtypeStruct((b, tq_all, w), F32),
        scratch_shapes=[pltpu.VMEM((H_A, tq, w), BF16), pltpu.VMEM((2, H_A, KEY_BLOCK, tq), F32),
                        pltpu.VMEM((2, H_A, 2, KEY_BLOCK, tq), BF16), pltpu.VMEM((w, tq), F32), pltpu.VMEM((8, tq), F32)],
        compiler_params=_params(2),
        name="sb_attention",
    )(q, k, vt)


def _fox_kernel(q_ref, k_ref, vt_ref, fq_ref, fk_ref, o_ref, qm_ref, z_ref, acc_ref, m_ref, l_ref, *, tq, q_pos0, n_valid):
    q_start = q_pos0 + pl.program_id(1) * tq
    n_full, n_total = _block_range(q_start, tq, n_valid)
    hmask = _head_masks(H_C)
    q = q_ref[...]
    for hh in range(H_C):
        qm_ref[hh] = jnp.where(hmask[hh], q, jnp.zeros_like(q))
    acc_ref[...] = jnp.zeros(acc_ref.shape, F32)
    m_ref[...] = jnp.full(m_ref.shape, -jnp.inf, F32)
    l_ref[...] = jnp.zeros(l_ref.shape, F32)

    heads = range(H_C)
    rows = [slice(hh * HEAD_DIM, (hh + 1) * HEAD_DIM) for hh in heads]

    def scores(j):
        off = pl.multiple_of(j * KEY_BLOCK, KEY_BLOCK)
        kb = k_ref[pl.ds(off, KEY_BLOCK), :]
        return [_dot_nt(kb, qm_ref[hh]) for hh in heads]

    def step(j, qk, overlap=None, mask=None):
        off = pl.multiple_of(j * KEY_BLOCK, KEY_BLOCK)
        fk = fk_ref[pl.ds(off, KEY_BLOCK), :]
        vts = [vt_ref[j, rows[hh], :] for hh in heads]
        fq, m_all, l_all = fq_ref[...], m_ref[...], l_ref[...]
        accs = [acc_ref[rows[hh], :] for hh in heads]
        upcoming = overlap[0]() if overlap else None
        zs = [qk[hh] + (fq[hh:hh + 1, :] - fk[:, hh:hh + 1]) for hh in heads]
        if mask is not None:
            zs = [jnp.where(mask, z, -jnp.inf) for z in zs]
        m_new = [jnp.maximum(m_all[hh:hh + 1, :], jnp.max(zs[hh], axis=0, keepdims=True)) for hh in heads]
        m_use = [_finite_or_zero(m) for m in m_new]
        ps = [jnp.exp2(zs[hh] - m_use[hh]) for hh in heads]
        alpha = [jnp.exp2(m_all[hh:hh + 1, :] - m_use[hh]) for hh in heads]
        l_new = [alpha[hh] * l_all[hh:hh + 1, :] + jnp.sum(ps[hh], axis=0, keepdims=True) for hh in heads]
        pvs = [_dot(vts[hh], ps[hh].astype(BF16)) for hh in heads]
        for hh in heads:
            m_ref[hh:hh + 1, :] = m_new[hh]
            l_ref[hh:hh + 1, :] = l_new[hh]
            acc_ref[rows[hh], :] = accs[hh] * alpha[hh] + pvs[hh]
        if overlap:
            overlap[1](upcoming)

    def store(slot, qk):
        for hh in heads:
            z_ref[slot, hh] = qk[hh]

    def load(slot):
        return [z_ref[slot, hh] for hh in heads]

    def masked_block(j, overlap=None):
        q_pos, k_pos = _positions(q_start, tq, j, tq)
        step(j, scores(j), overlap=overlap, mask=k_pos <= q_pos)

    def masked_body(i, carry):
        masked_block(n_total - 1 - i)
        return carry

    lax.fori_loop(0, n_total - n_full - 1, masked_body, 0)
    _pipelined_blocks(n_full, scores, step, store, load, lead=functools.partial(masked_block, n_full))
    for hh in range(H_C):
        rows = slice(hh * HEAD_DIM, (hh + 1) * HEAD_DIM)
        acc_ref[rows, :] = acc_ref[rows, :] * (1.0 / l_ref[hh:hh + 1, :])
    o_ref[...] = acc_ref[...].T


def _fox_attention(q, k, vt, fq, fk, q_pos0, n_valid):
    b, tq_all, w = q.shape
    tk = k.shape[1]
    tq = min(tq_all, KEY_BLOCK)
    qspec = pl.BlockSpec((None, tq, w), lambda bi, qi: (bi, qi, 0))
    kspec = pl.BlockSpec((None, tk, w), lambda bi, qi: (bi, 0, 0))
    vspec = pl.BlockSpec((None,) + vt.shape[1:], lambda bi, qi: (bi, 0, 0, 0))
    fqspec = pl.BlockSpec((None, F_ROWS, tq), lambda bi, qi: (bi, 0, qi))
    fkspec = pl.BlockSpec((None, tk, F_ROWS), lambda bi, qi: (bi, 0, 0))
    return pl.pallas_call(
        functools.partial(_fox_kernel, tq=tq, q_pos0=q_pos0, n_valid=n_valid),
        grid=(b, tq_all // tq),
        in_specs=[qspec, kspec, vspec, fqspec, fkspec],
        out_specs=qspec,
        out_shape=jax.ShapeDtypeStruct((b, tq_all, w), F32),
        scratch_shapes=[pltpu.VMEM((H_C, tq, w), BF16), pltpu.VMEM((2, H_C, KEY_BLOCK, tq), F32),
                        pltpu.VMEM((w, tq), F32), pltpu.VMEM((8, tq), F32), pltpu.VMEM((8, tq), F32)],
        compiler_params=_params(2),
        name="fox_attention",
    )(q, k, vt, fq, fk)


def _mla_kernel(q_ref, kl_ref, wuvt_ref, o_ref, ct_ref, z_ref, acc_ref, m_ref, l_ref, *, tq, q_pos0, n_valid):
    @pl.when(pl.program_id(1) == 0)
    def _():
        _transpose_blocks(kl_ref, ct_ref, KV_LORA)

    q_start = q_pos0 + pl.program_id(1) * tq
    n_full, n_total = _block_range(q_start, tq, n_valid)
    cols = H_B * tq
    qs = q_ref[...].reshape(cols, 2 * LANES)
    acc_ref[...] = jnp.zeros(acc_ref.shape, F32)
    m_ref[...] = jnp.full(m_ref.shape, -jnp.inf, F32)
    l_ref[...] = jnp.zeros(l_ref.shape, F32)

    group = KEY_BLOCK
    groups = [slice(g * group, (g + 1) * group) for g in range(cols // group)]

    gs = range(len(groups))

    def scores(j):
        off = pl.multiple_of(j * KEY_BLOCK, KEY_BLOCK)
        kb = kl_ref[pl.ds(off, KEY_BLOCK), :]
        return [_dot_nt(kb, qs[g]) for g in groups]

    def step(j, zs, overlap=None):
        ct = ct_ref[j]
        m_all, l_all = m_ref[...], l_ref[...]
        accs = [acc_ref[:, g] for g in groups]
        upcoming = overlap[0]() if overlap else None
        m_new = [jnp.maximum(m_all[:, groups[gi]], jnp.max(zs[gi], axis=0, keepdims=True)) for gi in gs]
        m_use = [_finite_or_zero(m) for m in m_new]
        ps = [jnp.exp2(zs[gi] - m_use[gi]) for gi in gs]
        alpha = [jnp.exp2(m_all[:, groups[gi]] - m_use[gi]) for gi in gs]
        l_new = [alpha[gi] * l_all[:, groups[gi]] + jnp.sum(ps[gi], axis=0, keepdims=True) for gi in gs]
        pvs = [_dot(ct, ps[gi].astype(BF16)) for gi in gs]
        for gi, g in enumerate(groups):
            m_ref[:, g] = m_new[gi]
            l_ref[:, g] = l_new[gi]
            acc_ref[:, g] = accs[gi] * alpha[gi] + pvs[gi]
        if overlap:
            overlap[1](upcoming)

    def store(slot, zs):
        for gi, g in enumerate(groups):
            z_ref[slot, :, g] = zs[gi]

    def load(slot):
        return [z_ref[slot, :, g] for g in groups]

    def masked_block(j, overlap=None):
        q_pos, k_pos = _positions(q_start, tq, j, group)
        mask = (k_pos // CHUNK <= q_pos // CHUNK) & (k_pos < n_valid)
        step(j, [jnp.where(mask, z, -jnp.inf) for z in scores(j)], overlap=overlap)

    def masked_body(i, carry):
        masked_block(n_total - 1 - i)
        return carry

    lax.fori_loop(0, n_total - n_full - 1, masked_body, 0)
    _pipelined_blocks(n_full, scores, step, store, load, lead=functools.partial(masked_block, n_full))

    lat = (acc_ref[...] * (1.0 / l_ref[...])).astype(BF16)
    heads = [_dot(wuvt_ref[hh], lat[:, hh * tq:(hh + 1) * tq]) for hh in range(H_B)]
    o_ref[...] = jnp.concatenate(heads, axis=0).T


def _mla_attention(qm, kl, wuvt, layer, tq_all, q_pos0, n_valid):
    b, tk, w = kl.shape
    tq = KEY_BLOCK if tq_all % KEY_BLOCK == 0 else LANES
    cols = H_B * tq
    return pl.pallas_call(
        functools.partial(_mla_kernel, tq=tq, q_pos0=q_pos0, n_valid=n_valid),
        grid=(b, tq_all // tq),
        in_specs=[pl.BlockSpec((H_B, None, tq, w), lambda bi, qi: (0, bi, qi, 0)),
                  pl.BlockSpec((None, tk, w), lambda bi, qi: (bi, 0, 0)),
                  _layer_spec(wuvt, layer)],
        out_specs=pl.BlockSpec((None, tq, W_B), lambda bi, qi: (bi, qi, 0)),
        out_shape=jax.ShapeDtypeStruct((b, tq_all, W_B), F32),
        scratch_shapes=[pltpu.VMEM((tk // KEY_BLOCK, KV_LORA, KEY_BLOCK), BF16), pltpu.VMEM((2, KEY_BLOCK, cols), F32),
                        pltpu.VMEM((KV_LORA, cols), F32), pltpu.VMEM((1, cols), F32), pltpu.VMEM((1, cols), F32)],
        compiler_params=_params(2),
        name="mla_attention",
    )(qm, kl, wuvt)


def _post_kernel(h_ref, oa_ref, ob_ref, oc_ref, p_ref, ggrp_ref, wout_ref, gmix_ref,
                 gpre_ref, gpost_ref, wgu_ref, wdown_ref, gple_pre_ref, wgate_ref, wproj_ref, gple_post_ref, o_ref,
                 *, d_ff, chunks):
    m = jnp.zeros(h_ref.shape, F32)
    c0 = 0
    for o_grp in (oa_ref, ob_ref, oc_ref):
        c1 = c0 + o_grp.shape[1]
        m = m + _dot(_rms(o_grp[...], ggrp_ref[:, c0:c1]).astype(BF16), wout_ref[c0:c1, :])
        c0 = c1
    h = h_ref[...] + _rms(m, gmix_ref[...])
    h = _ffn_rows(h, gpre_ref, gpost_ref, wgu_ref, wdown_ref, d_ff, chunks)
    gate = jax.nn.sigmoid(_dot(_rms(h, gple_pre_ref[...]).astype(BF16), wgate_ref[...]))
    e = _dot(p_ref[...].astype(BF16), wproj_ref[...]) * gate
    o_ref[...] = h + _rms(e, gple_post_ref[...])


def _post(h, oa, ob, oc, p, layer, consts):
    n, d = h.shape
    tm = _row_tile(n)
    row = lambda w: pl.BlockSpec((tm, w), lambda i: (i, 0))
    p_spec = pl.BlockSpec((None, tm, p.shape[2]), lambda i: (layer, i, 0))
    d_ff = consts[6].shape[1]
    return pl.pallas_call(
        functools.partial(_post_kernel, d_ff=d_ff, chunks=_ffn_chunks(d_ff)),
        grid=(n // tm,),
        in_specs=[row(d), row(W_A), row(W_B), row(W_C), p_spec] + [_layer_spec(c, layer) for c in consts],
        out_specs=row(d),
        out_shape=jax.ShapeDtypeStruct((n, d), F32),
        compiler_params=_params(1),
        name="post",
    )(h, oa, ob, oc, p, *consts)


def _rope_tables(pos):
    half = MLA_ROPE // 2
    inv = ROPE_THETA ** (-jnp.arange(half, dtype=F32) / half)
    ang = pos.astype(F32)[:, None] * inv[None, :]
    cos, sin = jnp.cos(ang), jnp.sin(ang)
    pad = jnp.zeros((pos.shape[0], LANES - MLA_ROPE), F32)
    return jnp.concatenate([cos, cos, pad], axis=1), jnp.concatenate([-sin, sin, pad], axis=1)


def _prep_weights(weights):
    (g_ff1_pre, g_ff1_post, w_ff1_gu, w_ff1_down, g_mix_pre, g_mix_post, w_in, b_f, g_bq, g_bkv, w_uq, w_ukv,
     g_grp, w_out, g_ff2_pre, g_ff2_post, w_ff2_gu, w_ff2_down, g_ple_pre, w_ple_gate, w_ple_proj, g_ple_post) = weights
    depth, d = w_in.shape[:2]
    half = MLA_ROPE // 2
    row = lambda g: g.reshape(depth, 1, -1).astype(F32)
    bf = lambda w: w.astype(BF16)
    c_kr = 3 * W_A + Q_LORA + KV_LORA
    c_c = c_kr + MLA_ROPE
    c_f = c_c + 3 * W_C
    w_in = bf(w_in)
    kr = w_in[:, :, c_kr:c_kr + MLA_ROPE]
    kr_sw = jnp.concatenate([kr[:, :, half:], kr[:, :, :half]], axis=2)
    zpad = jnp.zeros((depth, d, LANES - MLA_ROPE), BF16)
    win_p = jnp.concatenate([w_in[:, :, :c_kr], kr, zpad, kr_sw, zpad, w_in[:, :, c_c:c_f], w_in[:, :, c_f:],
                             jnp.zeros((depth, d, LANES - H_C), BF16)], axis=2)
    assert win_p.shape[2] == IN_COLS_P
    bf_p = jnp.pad(b_f.astype(F32), ((0, 0), (0, LANES - H_C))).reshape(depth, 1, LANES)
    wkvt = jnp.transpose(jnp.concatenate([w_in[:, :, W_A:3 * W_A], w_in[:, :, c_c + W_C:c_f]], axis=2), (0, 2, 1))

    wq4 = bf(w_uq).reshape(depth, Q_LORA, H_B, MLA_NOPE + MLA_ROPE)
    wkv4 = bf(w_ukv).reshape(depth, KV_LORA, H_B, MLA_NOPE + MLA_V)
    wq_n = jnp.transpose(wq4[..., :MLA_NOPE], (0, 2, 1, 3))
    wk_n = jnp.transpose(wkv4[..., :MLA_NOPE], (0, 2, 1, 3))
    wcomb = _wcomb(wq_n, wk_n)
    x1 = wq4[..., MLA_NOPE:MLA_NOPE + half]
    x2 = wq4[..., MLA_NOPE + half:]
    zq = jnp.zeros((depth, Q_LORA, H_B, LANES - MLA_ROPE), BF16)
    wqr = jnp.concatenate([jnp.concatenate([x1, x2, zq], axis=3).reshape(depth, Q_LORA, H_B * LANES),
                           jnp.concatenate([x2, x1, zq], axis=3).reshape(depth, Q_LORA, H_B * LANES)], axis=2)
    wuvt = jnp.transpose(wkv4[..., MLA_NOPE:], (0, 2, 3, 1))

    return dict(
        ff1=(row(g_ff1_pre), row(g_ff1_post), bf(w_ff1_gu), bf(w_ff1_down)),
        inproj=(row(g_mix_pre), win_p, bf_p, row(g_bq), row(g_bkv), wcomb, wqr, wkvt),
        wuvt=wuvt,
        post=(row(g_grp), bf(w_out), row(g_mix_post),
              row(g_ff2_pre), row(g_ff2_post), bf(w_ff2_gu), bf(w_ff2_down),
              row(g_ple_pre), bf(w_ple_gate), bf(w_ple_proj), row(g_ple_post)),
    )


def _pad_keys(a, tk_pad):
    return jnp.pad(a, ((0, 0), (0, tk_pad - a.shape[1])) + ((0, 0),) * (a.ndim - 2))


def _forget_sums(logf_rows, tk_pad):
    _, heads, tk = logf_rows.shape
    return _cumsum_rows(jnp.pad(logf_rows, ((0, 0), (0, F_ROWS - heads), (0, tk_pad - tk))))


def _prep_caches(caches, seq_len):
    ca_k, ca_v, cb_ckv, cb_kr, cc_k, cc_v, cc_lf = caches
    depth, b, past_len = ca_k.shape[:3]
    tq_pad = -(-seq_len // LANES) * LANES
    tk_pad = -(-(past_len + tq_pad) // KEY_BLOCK) * KEY_BLOCK
    grow = tk_pad - past_len
    rows = lambda c: jnp.pad(c.reshape(depth, b, past_len, -1).astype(BF16), ((0, 0), (0, 0), (0, grow), (0, 0)))
    blocks = lambda r: jnp.transpose(r.reshape(depth, b, tk_pad // KEY_BLOCK, KEY_BLOCK, r.shape[3]), (0, 1, 2, 4, 3))
    kl = jnp.concatenate([cb_ckv, cb_kr, jnp.zeros((depth, b, past_len, LANES - MLA_ROPE), F32)], axis=-1)
    return dict(past_len=past_len, tk_pad=tk_pad, ka=rows(ca_k), kc=rows(cc_k), kl=rows(kl),
                va=blocks(rows(ca_v)), vc=blocks(rows(cc_v)),
                lf=jnp.pad(jnp.transpose(cc_lf, (0, 1, 3, 2)), ((0, 0), (0, 0), (0, 0), (0, grow))))


def _layer(h, p, seq_len, q_pos0, past, layer, lp, tables, prev_states):
    n, d = h.shape
    b = n // seq_len
    h = _ffn(h, layer, *lp["ff1"])
    (ka_st, va_st, ckv_st, kr_st, kc_st, vc_st, lf_st,
     qa_b, ka_b, va_b, qm_b, kl_b, qc_b, kc_b, vc_b), feature_major = _inproj(h, seq_len, layer, prev_states,
                                                                                *lp["inproj"], *tables)
    if feature_major:
        state = (ka_st, va_st, ckv_st, kr_st, kc_st, vc_st, lf_st)
        lf_rows_new = lf_st[layer]
    else:
        state = (ka_st.reshape(b, seq_len, H_A, HEAD_DIM), va_st.reshape(b, seq_len, H_A, HEAD_DIM),
                 ckv_st.reshape(b, seq_len, KV_LORA), kr_st.reshape(b, seq_len, MLA_ROPE),
                 kc_st.reshape(b, seq_len, H_C, HEAD_DIM), vc_st.reshape(b, seq_len, H_C, HEAD_DIM),
                 lf_st.reshape(b, seq_len, H_C))
        lf_rows_new = jnp.transpose(state[6], (0, 2, 1))
    seq3 = lambda a: a.reshape(b, seq_len, a.shape[-1])
    tq_pad = -(-seq_len // LANES) * LANES
    to_blocks = lambda v: jnp.transpose(v.reshape(b, v.shape[1] // KEY_BLOCK, KEY_BLOCK, v.shape[2]), (0, 1, 3, 2))
    if past is None:
        n_valid = seq_len
        tk_pad = -(-tq_pad // KEY_BLOCK) * KEY_BLOCK
        ka_all, kl_all, kc_all = (_pad_keys(seq3(a), tk_pad) for a in (ka_b, kl_b, kc_b))
        if feature_major:
            va_all, vc_all = (v.reshape(b, seq_len // KEY_BLOCK, v.shape[1], KEY_BLOCK) for v in (va_b, vc_b))
        else:
            va_all, vc_all = (to_blocks(_pad_keys(seq3(v), tk_pad)) for v in (va_b, vc_b))
        lf_rows = lf_rows_new
    else:
        assert not feature_major
        past_len, tk_pad = past["past_len"], past["tk_pad"]
        n_valid = past_len + seq_len
        blk, off = divmod(past_len, KEY_BLOCK)
        assert off + seq_len <= KEY_BLOCK
        rows_at = lambda base, new: lax.dynamic_update_slice(base[layer], seq3(new), (0, past_len, 0))
        vals_at = lambda base, new: lax.dynamic_update_slice(base[layer], jnp.transpose(seq3(new), (0, 2, 1))[:, None],
                                                             (0, blk, 0, off))
        ka_all, kc_all, kl_all = rows_at(past["ka"], ka_b), rows_at(past["kc"], kc_b), rows_at(past["kl"], kl_b)
        va_all, vc_all = vals_at(past["va"], va_b), vals_at(past["vc"], vc_b)
        lf_rows = lax.dynamic_update_slice(past["lf"][layer], lf_rows_new, (0, 0, past_len))
    f_rows = _forget_sums(lf_rows, tk_pad)
    f_q = f_rows[:, :, q_pos0:q_pos0 + tq_pad]
    f_k = jnp.transpose(f_rows, (0, 2, 1))
    pad_q = lambda a: jnp.pad(a, ((0, 0),) * (a.ndim - 2) + ((0, tq_pad - seq_len), (0, 0)))

    oa = _sb_attention(pad_q(seq3(qa_b)), ka_all, va_all, q_pos0, n_valid)[:, :seq_len]
    ob = _mla_attention(pad_q(qm_b.reshape(H_B, b, seq_len, 2 * LANES)), kl_all, lp["wuvt"], layer, tq_pad, q_pos0,
                        n_valid)[:, :seq_len]
    oc = _fox_attention(pad_q(seq3(qc_b)), kc_all, vc_all, f_q, f_k, q_pos0, n_valid)[:, :seq_len]
    flat = lambda a: a.reshape(n, a.shape[-1])
    h = _post(h, flat(oa), flat(ob), flat(oc), p, layer, lp["post"])
    return h, state, feature_major


def _trunk(x, p, q_pos0, caches, lp, depth):
    b, t, d = x.shape
    tables = _rope_tables(q_pos0 + jnp.arange(t, dtype=jnp.int32))
    h = x.reshape(b * t, d)
    p = p.reshape(depth, b * t, -1)
    past = None if caches is None else _prep_caches(caches, t)
    states, st, stacked_mode = [], None, False
    for i in range(depth):
        h, st, stacked_mode = _layer(h, p, t, q_pos0, past, i, lp, tables, st if stacked_mode else None)
        states.append(st)
    if stacked_mode:
        heads5 = lambda a, nh: jnp.transpose(a.reshape(depth, b, nh, HEAD_DIM, t), (0, 1, 4, 2, 3))
        swap = lambda a: jnp.transpose(a, (0, 1, 3, 2))
        ka, va, ckv, kr, kc, vc, lf = st
        stacked = [heads5(ka, H_A), heads5(va, H_A), ckv.reshape(depth, b, t, KV_LORA), swap(kr),
                   heads5(kc, H_C), heads5(vc, H_C), swap(lf)]
    else:
        stacked = [jnp.stack([s[j] for s in states]) for j in range(N_STATE)]
    return h.reshape(b, t, d), stacked


def kernel(x_prompt, x_sample, p_prompt, p_sample, cache_a_k, cache_a_v, cache_b_ckv, cache_b_krope, cache_c_k, cache_c_v, cache_c_logf, g_ff1_pre, g_ff1_post, w_ff1_gu, w_ff1_down, g_mix_pre, g_mix_post, w_in, b_f, g_bq, g_bkv, w_uq, w_ukv, g_grp, w_out, g_ff2_pre, g_ff2_post, w_ff2_gu, w_ff2_down, g_ple_pre, w_ple_gate, w_ple_proj, g_ple_post):
    weights = (g_ff1_pre, g_ff1_post, w_ff1_gu, w_ff1_down, g_mix_pre, g_mix_post, w_in, b_f,
               g_bq, g_bkv, w_uq, w_ukv, g_grp, w_out, g_ff2_pre, g_ff2_post, w_ff2_gu, w_ff2_down,
               g_ple_pre, w_ple_gate, w_ple_proj, g_ple_post)
    depth = w_in.shape[0]
    lp = _prep_weights(weights)
    y_prompt, sp = _trunk(x_prompt, p_prompt, 0, None, lp, depth)
    caches = (cache_a_k, cache_a_v, cache_b_ckv, cache_b_krope, cache_c_k, cache_c_v, cache_c_logf)
    y_sample, ss = _trunk(x_sample, p_sample, cache_a_k.shape[2], caches, lp, depth)
    return (y_prompt, y_sample, *sp, *ss)
```

```python
import functools
import math

import jax
import jax.numpy as jnp
from jax import lax
from jax.experimental import pallas as pl
from jax.experimental.pallas import tpu as pltpu

CHUNK = 64
HEAD_DIM = 64
H_A = 4
H_B = 8
H_C = 4
W_A = H_A * HEAD_DIM
MLA_NOPE = 64
MLA_ROPE = 32
MLA_V = 64
W_B = H_B * MLA_V
W_C = H_C * HEAD_DIM
Q_LORA = 256
KV_LORA = 128
ROPE_THETA = 10000.0
EPS = 1e-6
FFN_RES = 0.5
SB_SCALE = HEAD_DIM ** -0.5
MLA_SCALE = (MLA_NOPE + MLA_ROPE) ** -0.5
FOX_SCALE = HEAD_DIM ** -0.5
LOG2E = math.log2(math.e)

LANES = 128
KEY_BLOCK = 256
F_ROWS = 16
VMEM_LIMIT = 56 * 1024 * 1024

COL_A = 0
COL_CQ = COL_A + 3 * W_A
COL_CKV = COL_CQ + Q_LORA
COL_KRA = COL_CKV + KV_LORA
COL_KRB = COL_KRA + LANES
COL_C = COL_KRB + LANES
COL_F = COL_C + 3 * W_C
IN_COLS_P = COL_F + LANES

BF16 = jnp.bfloat16
F32 = jnp.float32


def _dot(a, b):
    return jnp.dot(a, b, preferred_element_type=F32)


def _dot_nt(a, b):
    return lax.dot_general(a, b, (((1,), (1,)), ((), ())), preferred_element_type=F32)


def _rms(x, g):
    ms = jnp.mean(x * x, axis=-1, keepdims=True)
    return x * lax.rsqrt(ms + EPS) * g


def _log_sigmoid(x):
    return jnp.minimum(x, 0.0) - jnp.log(1.0 + jnp.exp(-jnp.abs(x)))


def _softplus(x):
    return jnp.maximum(x, 0.0) + jnp.log(1.0 + jnp.exp2(jnp.abs(x) * (-LOG2E)))


def _split2(x):
    hi = x.astype(BF16)
    lo = (x - hi.astype(F32)).astype(BF16)
    return hi, lo


def _layer_spec(a, layer):
    idx = (layer,) + (0,) * (a.ndim - 1)
    return pl.BlockSpec((None,) + a.shape[1:], lambda *_: idx, pipeline_mode=pl.Buffered(1))


def _params(n_axes):
    return pltpu.CompilerParams(dimension_semantics=("arbitrary",) * n_axes, vmem_limit_bytes=VMEM_LIMIT)


def _row_tile(n):
    for tm in (512, 256):
        if n % tm == 0:
            return tm
    return n


def _ffn_rows(h, gpre_ref, gpost_ref, wgu_ref, wdown_ref, d_ff, chunks):
    xn = _rms(h, gpre_ref[...]).astype(BF16)
    acc = jnp.zeros(h.shape, F32)
    for c0, c1 in chunks:
        g = _dot(xn, wgu_ref[:, c0:c1])
        u = _dot(xn, wgu_ref[:, d_ff + c0:d_ff + c1])
        a = (g * jax.nn.sigmoid(g) * u).astype(BF16)
        acc = acc + _dot(a, wdown_ref[c0:c1, :])
    return h + FFN_RES * _rms(acc, gpost_ref[...])


def _ffn_chunks(d_ff):
    step = 4 * KEY_BLOCK
    return tuple((c, min(c + step, d_ff)) for c in range(0, d_ff, step))


def _ffn_kernel(h_ref, gpre_ref, gpost_ref, wgu_ref, wdown_ref, o_ref, *, d_ff, chunks):
    o_ref[...] = _ffn_rows(h_ref[...], gpre_ref, gpost_ref, wgu_ref, wdown_ref, d_ff, chunks)


def _ffn(h, layer, gpre, gpost, wgu, wdown):
    n, d = h.shape
    d_ff = wdown.shape[1]
    consts = [gpre, gpost, wgu, wdown]
    tm = _row_tile(n)
    chunks = _ffn_chunks(d_ff)
    row = pl.BlockSpec((tm, d), lambda i: (i, 0))
    return pl.pallas_call(
        functools.partial(_ffn_kernel, d_ff=d_ff, chunks=chunks),
        grid=(n // tm,),
        in_specs=[row] + [_layer_spec(c, layer) for c in consts],
        out_specs=row,
        out_shape=jax.ShapeDtypeStruct((n, d), F32),
        compiler_params=_params(1),
        name="ffn",
    )(h, gpre, gpost, wgu, wdown)


def _wcomb_kernel(wq_ref, wk_ref, o_ref):
    kl = wk_ref.shape[1]
    for hh in range(wq_ref.shape[0]):
        o_ref[:, hh * kl:(hh + 1) * kl] = _dot_nt(wq_ref[hh], wk_ref[hh]).astype(BF16)


def _wcomb(wq_n, wk_n):
    depth, nh, ql, dn = wq_n.shape
    kl = wk_n.shape[2]
    return pl.pallas_call(
        _wcomb_kernel,
        grid=(depth,),
        in_specs=[pl.BlockSpec((None, nh, ql, dn), lambda i: (i, 0, 0, 0)),
                  pl.BlockSpec((None, nh, kl, dn), lambda i: (i, 0, 0, 0))],
        out_specs=pl.BlockSpec((None, ql, nh * kl), lambda i: (i, 0, 0)),
        out_shape=jax.ShapeDtypeStruct((depth, ql, nh * kl), BF16),
        compiler_params=_params(1),
        name="wcomb",
    )(wq_n, wk_n)


def _inproj_kernel(*refs, feature_major, n_alias, fill_layers):
    (h_ref, gpre_ref, win_ref, bf_ref, gbq_ref, gbkv_ref, wcomb_ref, wqr_ref, wkvt_ref, cos_ref, sin_ref) = refs[:11]
    (ka_st, va_st, ckv_st, kr_st, kc_st, vc_st, lf_st,
     qa_b, ka_b, va_b, qm_b, kl_b, qc_b, kc_b, vc_b) = refs[11 + n_alias:]

    def put(st, val):
        if fill_layers:
            for slab in range(fill_layers):
                st[slab] = val
        else:
            st[...] = val

    xn = _rms(h_ref[...], gpre_ref[...]).astype(BF16)
    proj = _dot(xn, win_ref[...])
    cos = cos_ref[...]
    sin = sin_ref[...]

    qa_b[...] = (proj[:, COL_A:COL_A + W_A] * SB_SCALE).astype(BF16)
    qc_b[...] = (proj[:, COL_C:COL_C + W_C] * (FOX_SCALE * LOG2E)).astype(BF16)
    ka = proj[:, COL_A + W_A:COL_A + 2 * W_A]
    kc = proj[:, COL_C + W_C:COL_C + 2 * W_C]
    ka_b[...] = ka.astype(BF16)
    kc_b[...] = kc.astype(BF16)
    lf = _log_sigmoid(proj[:, COL_F:COL_F + LANES] + bf_ref[...])
    ckv = _rms(proj[:, COL_CKV:COL_CKV + KV_LORA], gbkv_ref[...])
    put(ckv_st, ckv)
    kr = proj[:, COL_KRA:COL_KRA + LANES] * cos + proj[:, COL_KRB:COL_KRB + LANES] * sin
    kl_b[:, :KV_LORA] = ckv.astype(BF16)
    kl_b[:, KV_LORA:] = kr.astype(BF16)

    if feature_major:
        kvt = _dot_nt(wkvt_ref[...], xn)
        for idx, st in enumerate((ka_st, va_st, kc_st, vc_st)):
            put(st, kvt[idx * W_A:(idx + 1) * W_A])
        for r in range(va_b.shape[0]):
            cols = slice(r * KEY_BLOCK, (r + 1) * KEY_BLOCK)
            va_b[r] = kvt[W_A:2 * W_A, cols].astype(BF16)
            vc_b[r] = kvt[3 * W_A:4 * W_A, cols].astype(BF16)
        put(kr_st, kr.T[:MLA_ROPE])
        put(lf_st, lf.T[:H_C])
    else:
        va = proj[:, COL_A + 2 * W_A:COL_A + 3 * W_A]
        vc = proj[:, COL_C + 2 * W_C:COL_C + 3 * W_C]
        ka_st[...] = ka
        va_st[...] = va
        kc_st[...] = kc
        vc_st[...] = vc
        va_b[...] = va.astype(BF16)
        vc_b[...] = vc.astype(BF16)
        kr_st[...] = kr[:, :MLA_ROPE]
        lf_st[...] = lf[:, :H_C]

    cqn = _rms(proj[:, COL_CQ:COL_CQ + Q_LORA], gbq_ref[...]).astype(BF16)
    qlat = _dot(cqn, wcomb_ref[...])
    qr = _dot(cqn, wqr_ref[...])
    half = H_B * LANES
    for hh in range(H_B):
        sl = slice(hh * LANES, (hh + 1) * LANES)
        rope = qr[:, sl] * cos + qr[:, half + hh * LANES:half + (hh + 1) * LANES] * sin
        qm_b[hh, :, :KV_LORA] = (qlat[:, sl] * (MLA_SCALE * LOG2E)).astype(BF16)
        qm_b[hh, :, KV_LORA:] = (rope * (MLA_SCALE * LOG2E)).astype(BF16)


N_STATE = 7


def _inproj(h, seq_len, layer, prev_states, gpre, win, bf, gbq, gbkv, wcomb, wqr, wkvt, cos_t, sin_t):
    n, d = h.shape
    b = n // seq_len
    depth = win.shape[0]
    tm = _row_tile(n)
    row = lambda w: pl.BlockSpec((tm, w), lambda i: (i, 0))
    consts = [gpre, win, bf, gbq, gbkv, wcomb, wqr, wkvt]
    feature_major = seq_len % tm == 0 and tm % KEY_BLOCK == 0
    fill_all = feature_major and prev_states is None
    if feature_major:
        per_seq = seq_len // tm
        tab = pl.BlockSpec((tm, LANES), lambda i: (i % per_seq, 0))
        lead, at = (depth, 0) if fill_all else (None, layer)
        narrow = lambda w: (pl.BlockSpec((lead, None, w, tm), lambda i: (at, i // per_seq, 0, i % per_seq)),
                            (depth, b, w, seq_len), F32)
        latent = (pl.BlockSpec((lead, tm, KV_LORA), lambda i: (at, i, 0)), (depth, n, KV_LORA), F32)
        vals = lambda w: (pl.BlockSpec((tm // KEY_BLOCK, w, KEY_BLOCK), lambda i: (i, 0, 0)),
                          (n // KEY_BLOCK, w, KEY_BLOCK), BF16)
    else:
        cos_t, sin_t = (jnp.tile(t, (b, 1)) for t in (cos_t, sin_t))
        tab = row(LANES)
        narrow = lambda w: (row(w), (n, w), F32)
        latent = narrow(KV_LORA)
        vals = lambda w: (row(w), (n, w), BF16)
    outs = [
        narrow(W_A), narrow(W_A), latent, narrow(MLA_ROPE), narrow(W_C), narrow(W_C), narrow(H_C),
        (row(W_A), (n, W_A), BF16), (row(W_A), (n, W_A), BF16), vals(W_A),
        (pl.BlockSpec((H_B, tm, 2 * LANES), lambda i: (0, i, 0)), (H_B, n, 2 * LANES), BF16),
        (row(2 * LANES), (n, 2 * LANES), BF16),
        (row(W_C), (n, W_C), BF16), (row(W_C), (n, W_C), BF16), vals(W_C),
    ]
    ins = [h, *consts, cos_t, sin_t]
    in_specs = [row(d)] + [_layer_spec(c, layer) for c in consts] + [tab, tab]
    aliases = {}
    if feature_major and prev_states is not None:
        aliases = {len(ins) + s: s for s in range(N_STATE)}
        ins = ins + list(prev_states)
        in_specs = in_specs + [pl.BlockSpec(memory_space=pl.ANY)] * N_STATE
    res = pl.pallas_call(
        functools.partial(_inproj_kernel, feature_major=feature_major, n_alias=len(aliases),
                          fill_layers=depth if fill_all else 0),
        grid=(n // tm,),
        in_specs=in_specs,
        out_specs=[o[0] for o in outs],
        out_shape=[jax.ShapeDtypeStruct(o[1], o[2]) for o in outs],
        input_output_aliases=aliases,
        compiler_params=_params(1),
        name="inproj",
    )(*ins)
    return res, feature_major


def _cumsum_kernel(x_ref, o_ref, *, n_blocks):
    r = lax.broadcasted_iota(jnp.int32, (KEY_BLOCK, KEY_BLOCK), 0)
    c = lax.broadcasted_iota(jnp.int32, (KEY_BLOCK, KEY_BLOCK), 1)
    upper = jnp.where(r <= c, 1.0, 0.0).astype(BF16)
    carry = jnp.zeros((x_ref.shape[0], 1), F32)
    for j in range(n_blocks):
        sl = slice(j * KEY_BLOCK, (j + 1) * KEY_BLOCK)
        x = x_ref[:, sl]
        hi = x.astype(BF16)
        mid, lo = _split2(x - hi.astype(F32))
        f = _dot(hi, upper) + _dot(mid, upper) + _dot(lo, upper) + carry
        o_ref[:, sl] = f * LOG2E
        carry = f[:, KEY_BLOCK - 1:KEY_BLOCK]


def _cumsum_rows(x):
    b, r, tp = x.shape
    spec = pl.BlockSpec((None, r, tp), lambda i: (i, 0, 0))
    return pl.pallas_call(
        functools.partial(_cumsum_kernel, n_blocks=tp // KEY_BLOCK),
        grid=(b,),
        in_specs=[spec],
        out_specs=spec,
        out_shape=jax.ShapeDtypeStruct(x.shape, F32),
        compiler_params=_params(1),
        name="cumsum_logf",
    )(x)


def _block_range(q_start, tq, n_valid):
    n_full = q_start // KEY_BLOCK
    last = jnp.minimum(((q_start + tq + CHUNK - 1) // CHUNK) * CHUNK, n_valid)
    n_total = (last + KEY_BLOCK - 1) // KEY_BLOCK
    return n_full, n_total


def _positions(q_start, tq, j, cols):
    k_pos = j * KEY_BLOCK + lax.broadcasted_iota(jnp.int32, (KEY_BLOCK, cols), 0)
    lane = lax.broadcasted_iota(jnp.int32, (KEY_BLOCK, cols), 1)
    q_pos = q_start + (lane if cols == tq else lane % tq)
    return q_pos, k_pos


def _transpose_blocks(src_ref, dst_ref, width):
    for j in range(dst_ref.shape[0]):
        blk = src_ref[j * KEY_BLOCK:(j + 1) * KEY_BLOCK, :width].astype(F32)
        dst_ref[j] = blk.T.astype(BF16)


def _assemble_keys_values(kt_c_ref, vt_c_ref, k_n_ref, vt_n_ref, k_ref, vt_ref):
    past_len = kt_c_ref.shape[1]
    n_past, n_new = past_len // KEY_BLOCK, k_n_ref.shape[0]
    assert past_len % KEY_BLOCK == 0 and n_new <= KEY_BLOCK and vt_ref.shape[0] == n_past + 1
    for j in range(n_past):
        cols = slice(j * KEY_BLOCK, (j + 1) * KEY_BLOCK)
        k_ref[cols, :] = kt_c_ref[:, cols].T.astype(BF16)
        vt_ref[j] = vt_c_ref[:, cols].astype(BF16)
    k_ref[past_len:past_len + n_new, :] = k_n_ref[...]
    vt_ref[n_past, :, :n_new] = vt_n_ref[...]
    if n_new < KEY_BLOCK:
        k_ref[past_len + n_new:, :] = jnp.zeros((KEY_BLOCK - n_new, k_ref.shape[1]), BF16)
        vt_ref[n_past, :, n_new:] = jnp.zeros((vt_ref.shape[1], KEY_BLOCK - n_new), BF16)


def _pipelined_blocks(n_full, produce, consume, store, load, lead):
    f = lambda i: n_full - 1 - i
    n_pairs = jnp.maximum(n_full - 1, 0) // 2
    rest = n_full - 2 * n_pairs

    def both(i, slot):
        consume(f(i), load(slot), overlap=(lambda: produce(f(i + 1)), lambda vals: store(1 - slot, vals)))

    @pl.when(n_full > 0)
    def _():
        lead((lambda: produce(f(0)), lambda vals: store(0, vals)))

    @pl.when(n_full <= 0)
    def _():
        lead(None)

    def pair_body(t, carry):
        both(2 * t, 0)
        both(2 * t + 1, 1)
        return carry

    lax.fori_loop(0, n_pairs, pair_body, 0)

    @pl.when((n_full > 0) & (rest == 1))
    def _():
        consume(f(2 * n_pairs), load(0))

    @pl.when((n_full > 0) & (rest == 2))
    def _():
        both(2 * n_pairs, 0)
        consume(f(2 * n_pairs + 1), load(1))


def _finite_or_zero(m):
    return jnp.where(m == -jnp.inf, 0.0, m)


def _head_masks(n_heads):
    lane = lax.broadcasted_iota(jnp.int32, (1, n_heads * HEAD_DIM), 1)
    return [(lane >= hh * HEAD_DIM) & (lane < (hh + 1) * HEAD_DIM) for hh in range(n_heads)]


def _sb_kernel(*refs, tq, q_pos0, n_valid, assemble):
    if assemble:
        q_ref, kt_c_ref, vt_c_ref, k_n_ref, vt_n_ref, o_ref, k_ref, vt_ref, qm_ref, z_ref, x_ref, acc_ref, c_ref = refs

        @pl.when(pl.program_id(1) == 0)
        def _():
            _assemble_keys_values(kt_c_ref, vt_c_ref, k_n_ref, vt_n_ref, k_ref, vt_ref)
    else:
        q_ref, k_ref, vt_ref, o_ref, qm_ref, z_ref, x_ref, acc_ref, c_ref = refs
    q_start = q_pos0 + pl.program_id(1) * tq
    n_full, n_total = _block_range(q_start, tq, n_valid)
    hmask = _head_masks(H_A)
    q = q_ref[...]
    for hh in range(H_A):
        qm_ref[hh] = jnp.where(hmask[hh], q, jnp.zeros_like(q))
    acc_ref[...] = jnp.zeros(acc_ref.shape, F32)
    c_ref[...] = jnp.zeros(c_ref.shape, F32)
    half = KEY_BLOCK // 2
    r = lax.broadcasted_iota(jnp.int32, (half, KEY_BLOCK), 0)
    c = lax.broadcasted_iota(jnp.int32, (half, KEY_BLOCK), 1)
    neg_upper2 = jnp.where(c % half >= r, -1.0, 0.0).astype(BF16)
    heads = range(H_A)
    rows = [slice(hh * HEAD_DIM, (hh + 1) * HEAD_DIM) for hh in heads]

    def parts(x):
        hi, lo = _split2(x)
        return jnp.concatenate([hi, lo], axis=0)

    def first_stage(j, mask=None):
        off = pl.multiple_of(j * KEY_BLOCK, KEY_BLOCK)
        kb = k_ref[pl.ds(off, KEY_BLOCK), :]
        zs = [_dot_nt(kb, qm_ref[hh]) for hh in heads]
        sps = [_softplus(z) for z in zs]
        if mask is not None:
            sps = [jnp.where(mask, sp, 0.0) for sp in sps]
        return zs, [parts(sp[:half]) for sp in sps], [parts(sp[half:]) for sp in sps]

    def second_stage(j, vals, overlap=None, mask=None):
        zs, early, late = vals
        vts = [vt_ref[j, rows[hh], :] for hh in heads]
        carry = c_ref[...]
        upcoming = overlap[0]() if overlap else None
        cum_late = [_dot(neg_upper2, late[hh]) + carry[hh:hh + 1, :] for hh in heads]
        cum_early = [_dot(neg_upper2, early[hh]) + cum_late[hh][0:1, :] for hh in heads]
        ws = [jnp.exp(zs[hh] + jnp.concatenate([cum_early[hh], cum_late[hh]], axis=0)) for hh in heads]
        if mask is not None:
            ws = [jnp.where(mask, w, 0.0) for w in ws]
        pvs = [_dot(vts[hh], ws[hh].astype(BF16)) for hh in heads]
        for hh in heads:
            acc_ref[rows[hh], :] += pvs[hh]
            c_ref[hh:hh + 1, :] = cum_early[hh][0:1, :]
        if overlap:
            overlap[1](upcoming)

    def store(slot, vals):
        zs, early, late = vals
        for hh in heads:
            z_ref[slot, hh] = zs[hh]
            x_ref[slot, hh, 0] = early[hh]
            x_ref[slot, hh, 1] = late[hh]

    def load(slot):
        return ([z_ref[slot, hh] for hh in heads], [x_ref[slot, hh, 0] for hh in heads],
                [x_ref[slot, hh, 1] for hh in heads])

    def masked_block(j, overlap=None):
        q_pos, k_pos = _positions(q_start, tq, j, tq)
        mask = k_pos < q_pos
        second_stage(j, first_stage(j, mask), overlap=overlap, mask=mask)

    def masked_body(i, carry):
        masked_block(n_total - 1 - i)
        return carry

    lax.fori_loop(0, n_total - n_full - 1, masked_body, 0)
    _pipelined_blocks(n_full, first_stage, second_stage, store, load, lead=functools.partial(masked_block, n_full))
    o_ref[...] = acc_ref[...].T


def _kv_operands(kv):
    specs = [pl.BlockSpec((None,) + a.shape[1:], lambda bi, qi, nd=a.ndim: (bi,) + (0,) * (nd - 1)) for a in kv]
    if len(kv) == 2:
        return specs, []
    w, past_len = kv[0].shape[1:]
    tk = past_len + KEY_BLOCK
    return specs, [pltpu.VMEM((tk, w), BF16), pltpu.VMEM((tk // KEY_BLOCK, w, KEY_BLOCK), BF16)]


def _sb_attention(q, kv, q_pos0, n_valid):
    b, tq_all, w = q.shape
    tq = min(tq_all, KEY_BLOCK)
    qspec = pl.BlockSpec((None, tq, w), lambda bi, qi: (bi, qi, 0))
    kv_specs, kv_scratch = _kv_operands(kv)
    return pl.pallas_call(
        functools.partial(_sb_kernel, tq=tq, q_pos0=q_pos0, n_valid=n_valid, assemble=bool(kv_scratch)),
        grid=(b, tq_all // tq),
        in_specs=[qspec] + kv_specs,
        out_specs=qspec,
        out_shape=jax.ShapeDtypeStruct((b, tq_all, w), F32),
        scratch_shapes=kv_scratch + [
            pltpu.VMEM((H_A, tq, w), BF16), pltpu.VMEM((2, H_A, KEY_BLOCK, tq), F32),
            pltpu.VMEM((2, H_A, 2, KEY_BLOCK, tq), BF16), pltpu.VMEM((w, tq), F32), pltpu.VMEM((8, tq), F32)],
        compiler_params=_params(2),
        name="sb_attention",
    )(q, *kv)


def _fox_kernel(*refs, tq, q_pos0, n_valid, assemble):
    if assemble:
        (q_ref, kt_c_ref, vt_c_ref, k_n_ref, vt_n_ref, fq_ref, fk_ref, o_ref,
         k_ref, vt_ref, qm_ref, z_ref, acc_ref, m_ref, l_ref) = refs

        @pl.when(pl.program_id(1) == 0)
        def _():
            _assemble_keys_values(kt_c_ref, vt_c_ref, k_n_ref, vt_n_ref, k_ref, vt_ref)
    else:
        q_ref, k_ref, vt_ref, fq_ref, fk_ref, o_ref, qm_ref, z_ref, acc_ref, m_ref, l_ref = refs
    q_start = q_pos0 + pl.program_id(1) * tq
    n_full, n_total = _block_range(q_start, tq, n_valid)
    hmask = _head_masks(H_C)
    q = q_ref[...]
    for hh in range(H_C):
        qm_ref[hh] = jnp.where(hmask[hh], q, jnp.zeros_like(q))
    acc_ref[...] = jnp.zeros(acc_ref.shape, F32)
    m_ref[...] = jnp.full(m_ref.shape, -jnp.inf, F32)
    l_ref[...] = jnp.zeros(l_ref.shape, F32)

    heads = range(H_C)
    rows = [slice(hh * HEAD_DIM, (hh + 1) * HEAD_DIM) for hh in heads]

    def scores(j):
        off = pl.multiple_of(j * KEY_BLOCK, KEY_BLOCK)
        kb = k_ref[pl.ds(off, KEY_BLOCK), :]
        return [_dot_nt(kb, qm_ref[hh]) for hh in heads]

    def step(j, qk, overlap=None, mask=None):
        off = pl.multiple_of(j * KEY_BLOCK, KEY_BLOCK)
        fk = fk_ref[pl.ds(off, KEY_BLOCK), :]
        vts = [vt_ref[j, rows[hh], :] for hh in heads]
        fq, m_all, l_all = fq_ref[...], m_ref[...], l_ref[...]
        accs = [acc_ref[rows[hh], :] for hh in heads]
        upcoming = overlap[0]() if overlap else None
        zs = [qk[hh] + (fq[hh:hh + 1, :] - fk[:, hh:hh + 1]) for hh in heads]
        if mask is not None:
            zs = [jnp.where(mask, z, -jnp.inf) for z in zs]
        m_new = [jnp.maximum(m_all[hh:hh + 1, :], jnp.max(zs[hh], axis=0, keepdims=True)) for hh in heads]
        m_use = [_finite_or_zero(m) for m in m_new]
        ps = [jnp.exp2(zs[hh] - m_use[hh]) for hh in heads]
        alpha = [jnp.exp2(m_all[hh:hh + 1, :] - m_use[hh]) for hh in heads]
        l_new = [alpha[hh] * l_all[hh:hh + 1, :] + jnp.sum(ps[hh], axis=0, keepdims=True) for hh in heads]
        pvs = [_dot(vts[hh], ps[hh].astype(BF16)) for hh in heads]
        for hh in heads:
            m_ref[hh:hh + 1, :] = m_new[hh]
            l_ref[hh:hh + 1, :] = l_new[hh]
            acc_ref[rows[hh], :] = accs[hh] * alpha[hh] + pvs[hh]
        if overlap:
            overlap[1](upcoming)

    def store(slot, qk):
        for hh in heads:
            z_ref[slot, hh] = qk[hh]

    def load(slot):
        return [z_ref[slot, hh] for hh in heads]

    def masked_block(j, overlap=None):
        q_pos, k_pos = _positions(q_start, tq, j, tq)
        step(j, scores(j), overlap=overlap, mask=k_pos <= q_pos)

    def masked_body(i, carry):
        masked_block(n_total - 1 - i)
        return carry

    lax.fori_loop(0, n_total - n_full - 1, masked_body, 0)
    _pipelined_blocks(n_full, scores, step, store, load, lead=functools.partial(masked_block, n_full))
    for hh in range(H_C):
        rows = slice(hh * HEAD_DIM, (hh + 1) * HEAD_DIM)
        acc_ref[rows, :] = acc_ref[rows, :] * (1.0 / l_ref[hh:hh + 1, :])
    o_ref[...] = acc_ref[...].T


def _fox_attention(q, kv, fq, fk, q_pos0, n_valid):
    b, tq_all, w = q.shape
    tk = fk.shape[1]
    tq = min(tq_all, KEY_BLOCK)
    qspec = pl.BlockSpec((None, tq, w), lambda bi, qi: (bi, qi, 0))
    kv_specs, kv_scratch = _kv_operands(kv)
    fqspec = pl.BlockSpec((None, F_ROWS, tq), lambda bi, qi: (bi, 0, qi))
    fkspec = pl.BlockSpec((None, tk, F_ROWS), lambda bi, qi: (bi, 0, 0))
    return pl.pallas_call(
        functools.partial(_fox_kernel, tq=tq, q_pos0=q_pos0, n_valid=n_valid, assemble=bool(kv_scratch)),
        grid=(b, tq_all // tq),
        in_specs=[qspec] + kv_specs + [fqspec, fkspec],
        out_specs=qspec,
        out_shape=jax.ShapeDtypeStruct((b, tq_all, w), F32),
        scratch_shapes=kv_scratch + [
            pltpu.VMEM((H_C, tq, w), BF16), pltpu.VMEM((2, H_C, KEY_BLOCK, tq), F32),
            pltpu.VMEM((w, tq), F32), pltpu.VMEM((8, tq), F32), pltpu.VMEM((8, tq), F32)],
        compiler_params=_params(2),
        name="fox_attention",
    )(q, *kv, fq, fk)


def _mla_kernel(q_ref, kl_ref, wuvt_ref, o_ref, ct_ref, z_ref, acc_ref, m_ref, l_ref, *, tq, q_pos0, n_valid):
    @pl.when(pl.program_id(1) == 0)
    def _():
        _transpose_blocks(kl_ref, ct_ref, KV_LORA)

    q_start = q_pos0 + pl.program_id(1) * tq
    n_full, n_total = _block_range(q_start, tq, n_valid)
    cols = H_B * tq
    qs = q_ref[...].reshape(cols, 2 * LANES)
    acc_ref[...] = jnp.zeros(acc_ref.shape, F32)
    m_ref[...] = jnp.full(m_ref.shape, -jnp.inf, F32)
    l_ref[...] = jnp.zeros(l_ref.shape, F32)

    group = KEY_BLOCK
    groups = [slice(g * group, (g + 1) * group) for g in range(cols // group)]

    gs = range(len(groups))

    def scores(j):
        off = pl.multiple_of(j * KEY_BLOCK, KEY_BLOCK)
        kb = kl_ref[pl.ds(off, KEY_BLOCK), :]
        return [_dot_nt(kb, qs[g]) for g in groups]

    def step(j, zs, overlap=None):
        ct = ct_ref[j]
        m_all, l_all = m_ref[...], l_ref[...]
        accs = [acc_ref[:, g] for g in groups]
        upcoming = overlap[0]() if overlap else None
        m_new = [jnp.maximum(m_all[:, groups[gi]], jnp.max(zs[gi], axis=0, keepdims=True)) for gi in gs]
        m_use = [_finite_or_zero(m) for m in m_new]
        ps = [jnp.exp2(zs[gi] - m_use[gi]) for gi in gs]
        alpha = [jnp.exp2(m_all[:, groups[gi]] - m_use[gi]) for gi in gs]
        l_new = [alpha[gi] * l_all[:, groups[gi]] + jnp.sum(ps[gi], axis=0, keepdims=True) for gi in gs]
        pvs = [_dot(ct, ps[gi].astype(BF16)) for gi in gs]
        for gi, g in enumerate(groups):
            m_ref[:, g] = m_new[gi]
            l_ref[:, g] = l_new[gi]
            acc_ref[:, g] = accs[gi] * alpha[gi] + pvs[gi]
        if overlap:
            overlap[1](upcoming)

    def store(slot, zs):
        for gi, g in enumerate(groups):
            z_ref[slot, :, g] = zs[gi]

    def load(slot):
        return [z_ref[slot, :, g] for g in groups]

    def masked_block(j, overlap=None):
        q_pos, k_pos = _positions(q_start, tq, j, group)
        mask = (k_pos // CHUNK <= q_pos // CHUNK) & (k_pos < n_valid)
        step(j, [jnp.where(mask, z, -jnp.inf) for z in scores(j)], overlap=overlap)

    def masked_body(i, carry):
        masked_block(n_total - 1 - i)
        return carry

    lax.fori_loop(0, n_total - n_full - 1, masked_body, 0)
    _pipelined_blocks(n_full, scores, step, store, load, lead=functools.partial(masked_block, n_full))

    lat = (acc_ref[...] * (1.0 / l_ref[...])).astype(BF16)
    heads = [_dot(wuvt_ref[hh], lat[:, hh * tq:(hh + 1) * tq]) for hh in range(H_B)]
    o_ref[...] = jnp.concatenate(heads, axis=0).T


def _mla_attention(qm, kl, wuvt, layer, tq_all, q_pos0, n_valid):
    b, tk, w = kl.shape
    tq = KEY_BLOCK if tq_all % KEY_BLOCK == 0 else LANES
    cols = H_B * tq
    return pl.pallas_call(
        functools.partial(_mla_kernel, tq=tq, q_pos0=q_pos0, n_valid=n_valid),
        grid=(b, tq_all // tq),
        in_specs=[pl.BlockSpec((H_B, None, tq, w), lambda bi, qi: (0, bi, qi, 0)),
                  pl.BlockSpec((None, tk, w), lambda bi, qi: (bi, 0, 0)),
                  _layer_spec(wuvt, layer)],
        out_specs=pl.BlockSpec((None, tq, W_B), lambda bi, qi: (bi, qi, 0)),
        out_shape=jax.ShapeDtypeStruct((b, tq_all, W_B), F32),
        scratch_shapes=[pltpu.VMEM((tk // KEY_BLOCK, KV_LORA, KEY_BLOCK), BF16), pltpu.VMEM((2, KEY_BLOCK, cols), F32),
                        pltpu.VMEM((KV_LORA, cols), F32), pltpu.VMEM((1, cols), F32), pltpu.VMEM((1, cols), F32)],
        compiler_params=_params(2),
        name="mla_attention",
    )(qm, kl, wuvt)


def _post_kernel(h_ref, oa_ref, ob_ref, oc_ref, p_ref, ggrp_ref, wout_ref, gmix_ref,
                 gpre_ref, gpost_ref, wgu_ref, wdown_ref, gple_pre_ref, wgate_ref, wproj_ref, gple_post_ref, o_ref,
                 *, d_ff, chunks):
    m = jnp.zeros(h_ref.shape, F32)
    c0 = 0
    for o_grp in (oa_ref, ob_ref, oc_ref):
        c1 = c0 + o_grp.shape[1]
        m = m + _dot(_rms(o_grp[...], ggrp_ref[:, c0:c1]).astype(BF16), wout_ref[c0:c1, :])
        c0 = c1
    h = h_ref[...] + _rms(m, gmix_ref[...])
    h = _ffn_rows(h, gpre_ref, gpost_ref, wgu_ref, wdown_ref, d_ff, chunks)
    gate = jax.nn.sigmoid(_dot(_rms(h, gple_pre_ref[...]).astype(BF16), wgate_ref[...]))
    e = _dot(p_ref[...].astype(BF16), wproj_ref[...]) * gate
    o_ref[...] = h + _rms(e, gple_post_ref[...])


def _post(h, oa, ob, oc, p, layer, consts):
    n, d = h.shape
    tm = _row_tile(n)
    row = lambda w: pl.BlockSpec((tm, w), lambda i: (i, 0))
    p_spec = pl.BlockSpec((None, tm, p.shape[2]), lambda i: (layer, i, 0))
    d_ff = consts[6].shape[1]
    return pl.pallas_call(
        functools.partial(_post_kernel, d_ff=d_ff, chunks=_ffn_chunks(d_ff)),
        grid=(n // tm,),
        in_specs=[row(d), row(W_A), row(W_B), row(W_C), p_spec] + [_layer_spec(c, layer) for c in consts],
        out_specs=row(d),
        out_shape=jax.ShapeDtypeStruct((n, d), F32),
        compiler_params=_params(1),
        name="post",
    )(h, oa, ob, oc, p, *consts)


def _rope_tables(pos):
    half = MLA_ROPE // 2
    inv = ROPE_THETA ** (-jnp.arange(half, dtype=F32) / half)
    ang = pos.astype(F32)[:, None] * inv[None, :]
    cos, sin = jnp.cos(ang), jnp.sin(ang)
    pad = jnp.zeros((pos.shape[0], LANES - MLA_ROPE), F32)
    return jnp.concatenate([cos, cos, pad], axis=1), jnp.concatenate([-sin, sin, pad], axis=1)


def _prep_weights(weights):
    (g_ff1_pre, g_ff1_post, w_ff1_gu, w_ff1_down, g_mix_pre, g_mix_post, w_in, b_f, g_bq, g_bkv, w_uq, w_ukv,
     g_grp, w_out, g_ff2_pre, g_ff2_post, w_ff2_gu, w_ff2_down, g_ple_pre, w_ple_gate, w_ple_proj, g_ple_post) = weights
    depth, d = w_in.shape[:2]
    half = MLA_ROPE // 2
    row = lambda g: g.reshape(depth, 1, -1).astype(F32)
    bf = lambda w: w.astype(BF16)
    c_kr = 3 * W_A + Q_LORA + KV_LORA
    c_c = c_kr + MLA_ROPE
    c_f = c_c + 3 * W_C
    w_in = bf(w_in)
    kr = w_in[:, :, c_kr:c_kr + MLA_ROPE]
    kr_sw = jnp.concatenate([kr[:, :, half:], kr[:, :, :half]], axis=2)
    zpad = jnp.zeros((depth, d, LANES - MLA_ROPE), BF16)
    win_p = jnp.concatenate([w_in[:, :, :c_kr], kr, zpad, kr_sw, zpad, w_in[:, :, c_c:c_f], w_in[:, :, c_f:],
                             jnp.zeros((depth, d, LANES - H_C), BF16)], axis=2)
    assert win_p.shape[2] == IN_COLS_P
    bf_p = jnp.pad(b_f.astype(F32), ((0, 0), (0, LANES - H_C))).reshape(depth, 1, LANES)
    wkvt = jnp.transpose(jnp.concatenate([w_in[:, :, W_A:3 * W_A], w_in[:, :, c_c + W_C:c_f]], axis=2), (0, 2, 1))

    wq4 = bf(w_uq).reshape(depth, Q_LORA, H_B, MLA_NOPE + MLA_ROPE)
    wkv4 = bf(w_ukv).reshape(depth, KV_LORA, H_B, MLA_NOPE + MLA_V)
    wq_n = jnp.transpose(wq4[..., :MLA_NOPE], (0, 2, 1, 3))
    wk_n = jnp.transpose(wkv4[..., :MLA_NOPE], (0, 2, 1, 3))
    wcomb = _wcomb(wq_n, wk_n)
    x1 = wq4[..., MLA_NOPE:MLA_NOPE + half]
    x2 = wq4[..., MLA_NOPE + half:]
    zq = jnp.zeros((depth, Q_LORA, H_B, LANES - MLA_ROPE), BF16)
    wqr = jnp.concatenate([jnp.concatenate([x1, x2, zq], axis=3).reshape(depth, Q_LORA, H_B * LANES),
                           jnp.concatenate([x2, x1, zq], axis=3).reshape(depth, Q_LORA, H_B * LANES)], axis=2)
    wuvt = jnp.transpose(wkv4[..., MLA_NOPE:], (0, 2, 3, 1))

    return dict(
        ff1=(row(g_ff1_pre), row(g_ff1_post), bf(w_ff1_gu), bf(w_ff1_down)),
        inproj=(row(g_mix_pre), win_p, bf_p, row(g_bq), row(g_bkv), wcomb, wqr, wkvt),
        wuvt=wuvt,
        post=(row(g_grp), bf(w_out), row(g_mix_post),
              row(g_ff2_pre), row(g_ff2_post), bf(w_ff2_gu), bf(w_ff2_down),
              row(g_ple_pre), bf(w_ple_gate), bf(w_ple_proj), row(g_ple_post)),
    )


def _pad_keys(a, tk_pad):
    return jnp.pad(a, ((0, 0), (0, tk_pad - a.shape[1])) + ((0, 0),) * (a.ndim - 2))


def _forget_sums(logf_rows, tk_pad):
    _, heads, tk = logf_rows.shape
    return _cumsum_rows(jnp.pad(logf_rows, ((0, 0), (0, F_ROWS - heads), (0, tk_pad - tk))))


def _layer(h, p, seq_len, q_pos0, past, layer, lp, tables, prev_states):
    n, d = h.shape
    b = n // seq_len
    h = _ffn(h, layer, *lp["ff1"])
    (ka_st, va_st, ckv_st, kr_st, kc_st, vc_st, lf_st,
     qa_b, ka_b, va_b, qm_b, kl_b, qc_b, kc_b, vc_b), feature_major = _inproj(h, seq_len, layer, prev_states,
                                                                                *lp["inproj"], *tables)
    if feature_major:
        state = (ka_st, va_st, ckv_st, kr_st, kc_st, vc_st, lf_st)
        lf_rows_new = lf_st[layer]
    else:
        state = (ka_st.reshape(b, seq_len, H_A, HEAD_DIM), va_st.reshape(b, seq_len, H_A, HEAD_DIM),
                 ckv_st.reshape(b, seq_len, KV_LORA), kr_st.reshape(b, seq_len, MLA_ROPE),
                 kc_st.reshape(b, seq_len, H_C, HEAD_DIM), vc_st.reshape(b, seq_len, H_C, HEAD_DIM),
                 lf_st.reshape(b, seq_len, H_C))
        lf_rows_new = jnp.transpose(state[6], (0, 2, 1))
    seq3 = lambda a: a.reshape(b, seq_len, a.shape[-1])
    tq_pad = -(-seq_len // LANES) * LANES
    to_blocks = lambda v: jnp.transpose(v.reshape(b, v.shape[1] // KEY_BLOCK, KEY_BLOCK, v.shape[2]), (0, 1, 3, 2))
    if past is None:
        n_valid = seq_len
        tk_pad = -(-tq_pad // KEY_BLOCK) * KEY_BLOCK
        ka_all, kl_all, kc_all = (_pad_keys(seq3(a), tk_pad) for a in (ka_b, kl_b, kc_b))
        if feature_major:
            va_all, vc_all = (v.reshape(b, seq_len // KEY_BLOCK, v.shape[1], KEY_BLOCK) for v in (va_b, vc_b))
        else:
            va_all, vc_all = (to_blocks(_pad_keys(seq3(v), tk_pad)) for v in (va_b, vc_b))
        kv_a, kv_c = (ka_all, va_all), (kc_all, vc_all)
        lf_rows = lf_rows_new
    else:
        assert not feature_major
        pa_k, pa_v, pb_ckv, pb_kr, pc_k, pc_v, pc_lf = past
        past_len = pa_k.shape[1]
        n_valid = past_len + seq_len
        tk_pad = -(-(past_len + tq_pad) // KEY_BLOCK) * KEY_BLOCK
        join = lambda c, new: _pad_keys(jnp.concatenate([c.reshape(b, past_len, -1).astype(BF16), seq3(new)], axis=1), tk_pad)
        if past_len % KEY_BLOCK == 0 and tk_pad == past_len + KEY_BLOCK:
            wide = lambda c: jnp.transpose(c, (0, 2, 3, 1)).reshape(b, -1, past_len)
            new_k = lambda k: _pad_keys(seq3(k), tq_pad)
            new_vt = lambda v: jnp.transpose(_pad_keys(seq3(v), tq_pad), (0, 2, 1))
            kv_a = (wide(pa_k), wide(pa_v), new_k(ka_b), new_vt(va_b))
            kv_c = (wide(pc_k), wide(pc_v), new_k(kc_b), new_vt(vc_b))
        else:
            kv_a = (join(pa_k, ka_b), to_blocks(join(pa_v, va_b)))
            kv_c = (join(pc_k, kc_b), to_blocks(join(pc_v, vc_b)))
        kl_past = jnp.concatenate([pb_ckv, pb_kr, jnp.zeros((b, past_len, LANES - MLA_ROPE), F32)], axis=-1)
        kl_all = join(kl_past, kl_b)
        lf_rows = jnp.concatenate([jnp.transpose(pc_lf, (0, 2, 1)), lf_rows_new], axis=2)
    f_rows = _forget_sums(lf_rows, tk_pad)
    f_q = f_rows[:, :, q_pos0:q_pos0 + tq_pad]
    f_k = jnp.transpose(f_rows, (0, 2, 1))
    pad_q = lambda a: jnp.pad(a, ((0, 0),) * (a.ndim - 2) + ((0, tq_pad - seq_len), (0, 0)))

    oa = _sb_attention(pad_q(seq3(qa_b)), kv_a, q_pos0, n_valid)[:, :seq_len]
    ob = _mla_attention(pad_q(qm_b.reshape(H_B, b, seq_len, 2 * LANES)), kl_all, lp["wuvt"], layer, tq_pad, q_pos0,
                        n_valid)[:, :seq_len]
    oc = _fox_attention(pad_q(seq3(qc_b)), kv_c, f_q, f_k, q_pos0, n_valid)[:, :seq_len]
    flat = lambda a: a.reshape(n, a.shape[-1])
    h = _post(h, flat(oa), flat(ob), flat(oc), p, layer, lp["post"])
    return h, state, feature_major


def _trunk(x, p, q_pos0, caches, lp, depth):
    b, t, d = x.shape
    tables = _rope_tables(q_pos0 + jnp.arange(t, dtype=jnp.int32))
    h = x.reshape(b * t, d)
    p = p.reshape(depth, b * t, -1)
    states, st, stacked_mode = [], None, False
    for i in range(depth):
        past = None if caches is None else [c[i] for c in caches]
        h, st, stacked_mode = _layer(h, p, t, q_pos0, past, i, lp, tables, st if stacked_mode else None)
        states.append(st)
    if stacked_mode:
        heads5 = lambda a, nh: jnp.transpose(a.reshape(depth, b, nh, HEAD_DIM, t), (0, 1, 4, 2, 3))
        swap = lambda a: jnp.transpose(a, (0, 1, 3, 2))
        ka, va, ckv, kr, kc, vc, lf = st
        stacked = [heads5(ka, H_A), heads5(va, H_A), ckv.reshape(depth, b, t, KV_LORA), swap(kr),
                   heads5(kc, H_C), heads5(vc, H_C), swap(lf)]
    else:
        stacked = [jnp.stack([s[j] for s in states]) for j in range(N_STATE)]
    return h.reshape(b, t, d), stacked


def kernel(x_prompt, x_sample, p_prompt, p_sample, cache_a_k, cache_a_v, cache_b_ckv, cache_b_krope, cache_c_k, cache_c_v, cache_c_logf, g_ff1_pre, g_ff1_post, w_ff1_gu, w_ff1_down, g_mix_pre, g_mix_post, w_in, b_f, g_bq, g_bkv, w_uq, w_ukv, g_grp, w_out, g_ff2_pre, g_ff2_post, w_ff2_gu, w_ff2_down, g_ple_pre, w_ple_gate, w_ple_proj, g_ple_post):
    weights = (g_ff1_pre, g_ff1_post, w_ff1_gu, w_ff1_down, g_mix_pre, g_mix_post, w_in, b_f,
               g_bq, g_bkv, w_uq, w_ukv, g_grp, w_out, g_ff2_pre, g_ff2_post, w_ff2_gu, w_ff2_down,
               g_ple_pre, w_ple_gate, w_ple_proj, g_ple_post)
    depth = w_in.shape[0]
    lp = _prep_weights(weights)
    y_prompt, sp = _trunk(x_prompt, p_prompt, 0, None, lp, depth)
    caches = (cache_a_k, cache_a_v, cache_b_ckv, cache_b_krope, cache_c_k, cache_c_v, cache_c_logf)
    y_sample, ss = _trunk(x_sample, p_sample, cache_a_k.shape[2], caches, lp, depth)
    return (y_prompt, y_sample, *sp, *ss)
```

```python
import functools
import math

import jax
import jax.numpy as jnp
from jax import lax
from jax.experimental import pallas as pl
from jax.experimental.pallas import tpu as pltpu

CHUNK = 64
HEAD_DIM = 64
H_A = 4
H_B = 8
H_C = 4
W_A = H_A * HEAD_DIM
MLA_NOPE = 64
MLA_ROPE = 32
MLA_V = 64
W_B = H_B * MLA_V
W_C = H_C * HEAD_DIM
Q_LORA = 256
KV_LORA = 128
ROPE_THETA = 10000.0
EPS = 1e-6
FFN_RES = 0.5
SB_SCALE = HEAD_DIM ** -0.5
MLA_SCALE = (MLA_NOPE + MLA_ROPE) ** -0.5
FOX_SCALE = HEAD_DIM ** -0.5
LOG2E = math.log2(math.e)

LANES = 128
KEY_BLOCK = 256
F_ROWS = 16
VMEM_LIMIT = 56 * 1024 * 1024

COL_A = 0
COL_CQ = COL_A + 3 * W_A
COL_CKV = COL_CQ + Q_LORA
COL_KRA = COL_CKV + KV_LORA
COL_KRB = COL_KRA + LANES
COL_C = COL_KRB + LANES
COL_F = COL_C + 3 * W_C
IN_COLS_P = COL_F + LANES

BF16 = jnp.bfloat16
F32 = jnp.float32


def _dot(a, b):
    return jnp.dot(a, b, preferred_element_type=F32)


def _dot_nt(a, b):
    return lax.dot_general(a, b, (((1,), (1,)), ((), ())), preferred_element_type=F32)


def _rms(x, g):
    ms = jnp.mean(x * x, axis=-1, keepdims=True)
    return x * lax.rsqrt(ms + EPS) * g


def _log_sigmoid(x):
    return jnp.minimum(x, 0.0) - jnp.log(1.0 + jnp.exp(-jnp.abs(x)))


def _softplus(x):
    return jnp.maximum(x, 0.0) + jnp.log(1.0 + jnp.exp2(jnp.abs(x) * (-LOG2E)))


def _split2(x):
    hi = x.astype(BF16)
    lo = (x - hi.astype(F32)).astype(BF16)
    return hi, lo


def _layer_spec(a, layer):
    idx = (layer,) + (0,) * (a.ndim - 1)
    return pl.BlockSpec((None,) + a.shape[1:], lambda *_: idx, pipeline_mode=pl.Buffered(1))


def _params(n_axes):
    return pltpu.CompilerParams(dimension_semantics=("arbitrary",) * n_axes, vmem_limit_bytes=VMEM_LIMIT)


def _row_tile(n):
    for tm in (512, 256):
        if n % tm == 0:
            return tm
    return n


def _ffn_rows(h, gpre_ref, gpost_ref, wgu_ref, wdown_ref, d_ff, chunks):
    xn = _rms(h, gpre_ref[...]).astype(BF16)
    acc = jnp.zeros(h.shape, F32)
    for c0, c1 in chunks:
        g = _dot(xn, wgu_ref[:, c0:c1])
        u = _dot(xn, wgu_ref[:, d_ff + c0:d_ff + c1])
        a = (g * jax.nn.sigmoid(g) * u).astype(BF16)
        acc = acc + _dot(a, wdown_ref[c0:c1, :])
    return h + FFN_RES * _rms(acc, gpost_ref[...])


def _ffn_chunks(d_ff):
    step = 4 * KEY_BLOCK
    return tuple((c, min(c + step, d_ff)) for c in range(0, d_ff, step))


def _ffn_kernel(h_ref, gpre_ref, gpost_ref, wgu_ref, wdown_ref, o_ref, *, d_ff, chunks):
    o_ref[...] = _ffn_rows(h_ref[...], gpre_ref, gpost_ref, wgu_ref, wdown_ref, d_ff, chunks)


def _ffn(h, layer, gpre, gpost, wgu, wdown):
    n, d = h.shape
    d_ff = wdown.shape[1]
    consts = [gpre, gpost, wgu, wdown]
    tm = _row_tile(n)
    chunks = _ffn_chunks(d_ff)
    row = pl.BlockSpec((tm, d), lambda i: (i, 0))
    return pl.pallas_call(
        functools.partial(_ffn_kernel, d_ff=d_ff, chunks=chunks),
        grid=(n // tm,),
        in_specs=[row] + [_layer_spec(c, layer) for c in consts],
        out_specs=row,
        out_shape=jax.ShapeDtypeStruct((n, d), F32),
        compiler_params=_params(1),
        name="ffn",
    )(h, gpre, gpost, wgu, wdown)


def _wcomb_kernel(wq_ref, wk_ref, o_ref):
    kl = wk_ref.shape[1]
    for hh in range(wq_ref.shape[0]):
        o_ref[:, hh * kl:(hh + 1) * kl] = _dot_nt(wq_ref[hh], wk_ref[hh]).astype(BF16)


def _wcomb(wq_n, wk_n):
    depth, nh, ql, dn = wq_n.shape
    kl = wk_n.shape[2]
    return pl.pallas_call(
        _wcomb_kernel,
        grid=(depth,),
        in_specs=[pl.BlockSpec((None, nh, ql, dn), lambda i: (i, 0, 0, 0)),
                  pl.BlockSpec((None, nh, kl, dn), lambda i: (i, 0, 0, 0))],
        out_specs=pl.BlockSpec((None, ql, nh * kl), lambda i: (i, 0, 0)),
        out_shape=jax.ShapeDtypeStruct((depth, ql, nh * kl), BF16),
        compiler_params=_params(1),
        name="wcomb",
    )(wq_n, wk_n)


def _inproj_kernel(*refs, feature_major, n_alias, fill_layers):
    (h_ref, gpre_ref, win_ref, bf_ref, gbq_ref, gbkv_ref, wcomb_ref, wqr_ref, wkvt_ref, cos_ref, sin_ref) = refs[:11]
    (ka_st, va_st, ckv_st, kr_st, kc_st, vc_st, lf_st,
     qa_b, ka_b, va_b, qm_b, kl_b, qc_b, kc_b, vc_b) = refs[11 + n_alias:]

    def put(st, val):
        if fill_layers:
            for slab in range(fill_layers):
                st[slab] = val
        else:
            st[...] = val

    xn = _rms(h_ref[...], gpre_ref[...]).astype(BF16)
    proj = _dot(xn, win_ref[...])
    cos = cos_ref[...]
    sin = sin_ref[...]

    qa_b[...] = (proj[:, COL_A:COL_A + W_A] * SB_SCALE).astype(BF16)
    qc_b[...] = (proj[:, COL_C:COL_C + W_C] * (FOX_SCALE * LOG2E)).astype(BF16)
    ka = proj[:, COL_A + W_A:COL_A + 2 * W_A]
    kc = proj[:, COL_C + W_C:COL_C + 2 * W_C]
    ka_b[...] = ka.astype(BF16)
    kc_b[...] = kc.astype(BF16)
    lf = _log_sigmoid(proj[:, COL_F:COL_F + LANES] + bf_ref[...])
    ckv = _rms(proj[:, COL_CKV:COL_CKV + KV_LORA], gbkv_ref[...])
    put(ckv_st, ckv)
    kr = proj[:, COL_KRA:COL_KRA + LANES] * cos + proj[:, COL_KRB:COL_KRB + LANES] * sin
    kl_b[:, :KV_LORA] = ckv.astype(BF16)
    kl_b[:, KV_LORA:] = kr.astype(BF16)

    if feature_major:
        kvt = _dot_nt(wkvt_ref[...], xn)
        for idx, st in enumerate((ka_st, va_st, kc_st, vc_st)):
            put(st, kvt[idx * W_A:(idx + 1) * W_A])
        for r in range(va_b.shape[0]):
            cols = slice(r * KEY_BLOCK, (r + 1) * KEY_BLOCK)
            va_b[r] = kvt[W_A:2 * W_A, cols].astype(BF16)
            vc_b[r] = kvt[3 * W_A:4 * W_A, cols].astype(BF16)
        put(kr_st, kr.T[:MLA_ROPE])
        put(lf_st, lf.T[:H_C])
    else:
        va = proj[:, COL_A + 2 * W_A:COL_A + 3 * W_A]
        vc = proj[:, COL_C + 2 * W_C:COL_C + 3 * W_C]
        ka_st[...] = ka
        va_st[...] = va
        kc_st[...] = kc
        vc_st[...] = vc
        va_b[...] = va.astype(BF16)
        vc_b[...] = vc.astype(BF16)
        kr_st[...] = kr[:, :MLA_ROPE]
        lf_st[...] = lf[:, :H_C]

    cqn = _rms(proj[:, COL_CQ:COL_CQ + Q_LORA], gbq_ref[...]).astype(BF16)
    qlat = _dot(cqn, wcomb_ref[...])
    qr = _dot(cqn, wqr_ref[...])
    half = H_B * LANES
    for hh in range(H_B):
        sl = slice(hh * LANES, (hh + 1) * LANES)
        rope = qr[:, sl] * cos + qr[:, half + hh * LANES:half + (hh + 1) * LANES] * sin
        qm_b[hh, :, :KV_LORA] = (qlat[:, sl] * (MLA_SCALE * LOG2E)).astype(BF16)
        qm_b[hh, :, KV_LORA:] = (rope * (MLA_SCALE * LOG2E)).astype(BF16)


N_STATE = 7


def _inproj(h, seq_len, layer, prev_states, gpre, win, bf, gbq, gbkv, wcomb, wqr, wkvt, cos_t, sin_t):
    n, d = h.shape
    b = n // seq_len
    depth = win.shape[0]
    tm = _row_tile(n)
    row = lambda w: pl.BlockSpec((tm, w), lambda i: (i, 0))
    consts = [gpre, win, bf, gbq, gbkv, wcomb, wqr, wkvt]
    feature_major = seq_len % tm == 0 and tm % KEY_BLOCK == 0
    fill_all = feature_major and prev_states is None
    if feature_major:
        per_seq = seq_len // tm
        tab = pl.BlockSpec((tm, LANES), lambda i: (i % per_seq, 0))
        lead, at = (depth, 0) if fill_all else (None, layer)
        narrow = lambda w: (pl.BlockSpec((lead, None, w, tm), lambda i: (at, i // per_seq, 0, i % per_seq)),
                            (depth, b, w, seq_len), F32)
        latent = (pl.BlockSpec((lead, tm, KV_LORA), lambda i: (at, i, 0)), (depth, n, KV_LORA), F32)
        vals = lambda w: (pl.BlockSpec((tm // KEY_BLOCK, w, KEY_BLOCK), lambda i: (i, 0, 0)),
                          (n // KEY_BLOCK, w, KEY_BLOCK), BF16)
    else:
        cos_t, sin_t = (jnp.tile(t, (b, 1)) for t in (cos_t, sin_t))
        tab = row(LANES)
        narrow = lambda w: (row(w), (n, w), F32)
        latent = narrow(KV_LORA)
        vals = lambda w: (row(w), (n, w), BF16)
    outs = [
        narrow(W_A), narrow(W_A), latent, narrow(MLA_ROPE), narrow(W_C), narrow(W_C), narrow(H_C),
        (row(W_A), (n, W_A), BF16), (row(W_A), (n, W_A), BF16), vals(W_A),
        (pl.BlockSpec((H_B, tm, 2 * LANES), lambda i: (0, i, 0)), (H_B, n, 2 * LANES), BF16),
        (row(2 * LANES), (n, 2 * LANES), BF16),
        (row(W_C), (n, W_C), BF16), (row(W_C), (n, W_C), BF16), vals(W_C),
    ]
    ins = [h, *consts, cos_t, sin_t]
    in_specs = [row(d)] + [_layer_spec(c, layer) for c in consts] + [tab, tab]
    aliases = {}
    if feature_major and prev_states is not None:
        aliases = {len(ins) + s: s for s in range(N_STATE)}
        ins = ins + list(prev_states)
        in_specs = in_specs + [pl.BlockSpec(memory_space=pl.ANY)] * N_STATE
    res = pl.pallas_call(
        functools.partial(_inproj_kernel, feature_major=feature_major, n_alias=len(aliases),
                          fill_layers=depth if fill_all else 0),
        grid=(n // tm,),
        in_specs=in_specs,
        out_specs=[o[0] for o in outs],
        out_shape=[jax.ShapeDtypeStruct(o[1], o[2]) for o in outs],
        input_output_aliases=aliases,
        compiler_params=_params(1),
        name="inproj",
    )(*ins)
    return res, feature_major


def _cumsum_kernel(x_ref, o_ref, *, n_blocks):
    r = lax.broadcasted_iota(jnp.int32, (KEY_BLOCK, KEY_BLOCK), 0)
    c = lax.broadcasted_iota(jnp.int32, (KEY_BLOCK, KEY_BLOCK), 1)
    upper = jnp.where(r <= c, 1.0, 0.0).astype(BF16)
    carry = jnp.zeros((x_ref.shape[0], 1), F32)
    for j in range(n_blocks):
        sl = slice(j * KEY_BLOCK, (j + 1) * KEY_BLOCK)
        x = x_ref[:, sl]
        hi = x.astype(BF16)
        mid, lo = _split2(x - hi.astype(F32))
        f = _dot(hi, upper) + _dot(mid, upper) + _dot(lo, upper) + carry
        o_ref[:, sl] = f * LOG2E
        carry = f[:, KEY_BLOCK - 1:KEY_BLOCK]


def _cumsum_rows(x):
    b, r, tp = x.shape
    spec = pl.BlockSpec((None, r, tp), lambda i: (i, 0, 0))
    return pl.pallas_call(
        functools.partial(_cumsum_kernel, n_blocks=tp // KEY_BLOCK),
        grid=(b,),
        in_specs=[spec],
        out_specs=spec,
        out_shape=jax.ShapeDtypeStruct(x.shape, F32),
        compiler_params=_params(1),
        name="cumsum_logf",
    )(x)


def _block_range(q_start, tq, n_valid):
    n_full = q_start // KEY_BLOCK
    last = jnp.minimum(((q_start + tq + CHUNK - 1) // CHUNK) * CHUNK, n_valid)
    n_total = (last + KEY_BLOCK - 1) // KEY_BLOCK
    return n_full, n_total


def _positions(q_start, tq, j, cols):
    k_pos = j * KEY_BLOCK + lax.broadcasted_iota(jnp.int32, (KEY_BLOCK, cols), 0)
    lane = lax.broadcasted_iota(jnp.int32, (KEY_BLOCK, cols), 1)
    q_pos = q_start + (lane if cols == tq else lane % tq)
    return q_pos, k_pos


def _transpose_blocks(src_ref, dst_ref, width):
    for j in range(dst_ref.shape[0]):
        blk = src_ref[j * KEY_BLOCK:(j + 1) * KEY_BLOCK, :width].astype(F32)
        dst_ref[j] = blk.T.astype(BF16)


def _assemble_keys_values(kt_c_ref, vt_c_ref, k_n_ref, vt_n_ref, k_ref, vt_ref):
    past_len = kt_c_ref.shape[1]
    n_past, n_new = past_len // KEY_BLOCK, k_n_ref.shape[0]
    assert past_len % KEY_BLOCK == 0 and n_new <= KEY_BLOCK and vt_ref.shape[0] == n_past + 1
    for j in range(n_past):
        cols = slice(j * KEY_BLOCK, (j + 1) * KEY_BLOCK)
        k_ref[cols, :] = kt_c_ref[:, cols].T.astype(BF16)
        vt_ref[j] = vt_c_ref[:, cols].astype(BF16)
    k_ref[past_len:past_len + n_new, :] = k_n_ref[...]
    vt_ref[n_past, :, :n_new] = vt_n_ref[...]
    if n_new < KEY_BLOCK:
        k_ref[past_len + n_new:, :] = jnp.zeros((KEY_BLOCK - n_new, k_ref.shape[1]), BF16)
        vt_ref[n_past, :, n_new:] = jnp.zeros((vt_ref.shape[1], KEY_BLOCK - n_new), BF16)


def _pipelined_blocks(n_full, produce, consume, store, load, lead):
    f = lambda i: n_full - 1 - i
    n_pairs = jnp.maximum(n_full - 1, 0) // 2
    rest = n_full - 2 * n_pairs

    def both(i, slot):
        consume(f(i), load(slot), overlap=(lambda: produce(f(i + 1)), lambda vals: store(1 - slot, vals)))

    @pl.when(n_full > 0)
    def _():
        lead((lambda: produce(f(0)), lambda vals: store(0, vals)))

    @pl.when(n_full <= 0)
    def _():
        lead(None)

    def pair_body(t, carry):
        both(2 * t, 0)
        both(2 * t + 1, 1)
        return carry

    lax.fori_loop(0, n_pairs, pair_body, 0)

    @pl.when((n_full > 0) & (rest == 1))
    def _():
        consume(f(2 * n_pairs), load(0))

    @pl.when((n_full > 0) & (rest == 2))
    def _():
        both(2 * n_pairs, 0)
        consume(f(2 * n_pairs + 1), load(1))


def _finite_or_zero(m):
    return jnp.where(m == -jnp.inf, 0.0, m)


def _head_masks(n_heads):
    lane = lax.broadcasted_iota(jnp.int32, (1, n_heads * HEAD_DIM), 1)
    return [(lane >= hh * HEAD_DIM) & (lane < (hh + 1) * HEAD_DIM) for hh in range(n_heads)]


def _sb_kernel(*refs, tq, q_pos0, n_valid, assemble):
    if assemble:
        q_ref, kt_c_ref, vt_c_ref, k_n_ref, vt_n_ref, o_ref, k_ref, vt_ref, qm_ref, z_ref, x_ref, acc_ref, c_ref = refs

        @pl.when(pl.program_id(1) == 0)
        def _():
            _assemble_keys_values(kt_c_ref, vt_c_ref, k_n_ref, vt_n_ref, k_ref, vt_ref)
    else:
        q_ref, k_ref, vt_ref, o_ref, qm_ref, z_ref, x_ref, acc_ref, c_ref = refs
    q_start = q_pos0 + pl.program_id(1) * tq
    n_full, n_total = _block_range(q_start, tq, n_valid)
    hmask = _head_masks(H_A)
    q = q_ref[...]
    for hh in range(H_A):
        qm_ref[hh] = jnp.where(hmask[hh], q, jnp.zeros_like(q))
    acc_ref[...] = jnp.zeros(acc_ref.shape, F32)
    c_ref[...] = jnp.zeros(c_ref.shape, F32)
    half = KEY_BLOCK // 2
    r = lax.broadcasted_iota(jnp.int32, (half, KEY_BLOCK), 0)
    c = lax.broadcasted_iota(jnp.int32, (half, KEY_BLOCK), 1)
    neg_upper2 = jnp.where(c % half >= r, -1.0, 0.0).astype(BF16)
    heads = range(H_A)
    rows = [slice(hh * HEAD_DIM, (hh + 1) * HEAD_DIM) for hh in heads]

    def parts(x):
        hi, lo = _split2(x)
        return jnp.concatenate([hi, lo], axis=0)

    def first_stage(j, mask=None):
        off = pl.multiple_of(j * KEY_BLOCK, KEY_BLOCK)
        kb = k_ref[pl.ds(off, KEY_BLOCK), :]
        zs = [_dot_nt(kb, qm_ref[hh]) for hh in heads]
        sps = [_softplus(z) for z in zs]
        if mask is not None:
            sps = [jnp.where(mask, sp, 0.0) for sp in sps]
        return zs, [parts(sp[:half]) for sp in sps], [parts(sp[half:]) for sp in sps]

    def second_stage(j, vals, overlap=None, mask=None):
        zs, early, late = vals
        vts = [vt_ref[j, rows[hh], :] for hh in heads]
        carry = c_ref[...]
        upcoming = overlap[0]() if overlap else None
        cum_late = [_dot(neg_upper2, late[hh]) + carry[hh:hh + 1, :] for hh in heads]
        cum_early = [_dot(neg_upper2, early[hh]) + cum_late[hh][0:1, :] for hh in heads]
        ws = [jnp.exp(zs[hh] + jnp.concatenate([cum_early[hh], cum_late[hh]], axis=0)) for hh in heads]
        if mask is not None:
            ws = [jnp.where(mask, w, 0.0) for w in ws]
        pvs = [_dot(vts[hh], ws[hh].astype(BF16)) for hh in heads]
        for hh in heads:
            acc_ref[rows[hh], :] += pvs[hh]
            c_ref[hh:hh + 1, :] = cum_early[hh][0:1, :]
        if overlap:
            overlap[1](upcoming)

    def store(slot, vals):
        zs, early, late = vals
        for hh in heads:
            z_ref[slot, hh] = zs[hh]
            x_ref[slot, hh, 0] = early[hh]
            x_ref[slot, hh, 1] = late[hh]

    def load(slot):
        return ([z_ref[slot, hh] for hh in heads], [x_ref[slot, hh, 0] for hh in heads],
                [x_ref[slot, hh, 1] for hh in heads])

    def masked_block(j, overlap=None):
        q_pos, k_pos = _positions(q_start, tq, j, tq)
        mask = k_pos < q_pos
        second_stage(j, first_stage(j, mask), overlap=overlap, mask=mask)

    def masked_body(i, carry):
        masked_block(n_total - 1 - i)
        return carry

    lax.fori_loop(0, n_total - n_full - 1, masked_body, 0)
    _pipelined_blocks(n_full, first_stage, second_stage, store, load, lead=functools.partial(masked_block, n_full))
    o_ref[...] = acc_ref[...].T


def _kv_operands(kv):
    specs = [pl.BlockSpec((None,) + a.shape[1:], lambda bi, qi, nd=a.ndim: (bi,) + (0,) * (nd - 1)) for a in kv]
    if len(kv) == 2:
        return specs, []
    w, past_len = kv[0].shape[1:]
    tk = past_len + KEY_BLOCK
    return specs, [pltpu.VMEM((tk, w), BF16), pltpu.VMEM((tk // KEY_BLOCK, w, KEY_BLOCK), BF16)]


def _sb_attention(q, kv, q_pos0, n_valid):
    b, tq_all, w = q.shape
    tq = min(tq_all, KEY_BLOCK)
    qspec = pl.BlockSpec((None, tq, w), lambda bi, qi: (bi, qi, 0))
    kv_specs, kv_scratch = _kv_operands(kv)
    return pl.pallas_call(
        functools.partial(_sb_kernel, tq=tq, q_pos0=q_pos0, n_valid=n_valid, assemble=bool(kv_scratch)),
        grid=(b, tq_all // tq),
        in_specs=[qspec] + kv_specs,
        out_specs=qspec,
        out_shape=jax.ShapeDtypeStruct((b, tq_all, w), F32),
        scratch_shapes=kv_scratch + [
            pltpu.VMEM((H_A, tq, w), BF16), pltpu.VMEM((2, H_A, KEY_BLOCK, tq), F32),
            pltpu.VMEM((2, H_A, 2, KEY_BLOCK, tq), BF16), pltpu.VMEM((w, tq), F32), pltpu.VMEM((8, tq), F32)],
        compiler_params=_params(2),
        name="sb_attention",
    )(q, *kv)


def _fox_kernel(*refs, tq, q_pos0, n_valid, assemble):
    if assemble:
        (q_ref, kt_c_ref, vt_c_ref, k_n_ref, vt_n_ref, fq_ref, fk_ref, o_ref,
         k_ref, vt_ref, qm_ref, z_ref, acc_ref, m_ref, l_ref) = refs

        @pl.when(pl.program_id(1) == 0)
        def _():
            _assemble_keys_values(kt_c_ref, vt_c_ref, k_n_ref, vt_n_ref, k_ref, vt_ref)
    else:
        q_ref, k_ref, vt_ref, fq_ref, fk_ref, o_ref, qm_ref, z_ref, acc_ref, m_ref, l_ref = refs
    q_start = q_pos0 + pl.program_id(1) * tq
    n_full, n_total = _block_range(q_start, tq, n_valid)
    hmask = _head_masks(H_C)
    q = q_ref[...]
    for hh in range(H_C):
        qm_ref[hh] = jnp.where(hmask[hh], q, jnp.zeros_like(q))
    acc_ref[...] = jnp.zeros(acc_ref.shape, F32)
    m_ref[...] = jnp.full(m_ref.shape, -jnp.inf, F32)
    l_ref[...] = jnp.zeros(l_ref.shape, F32)

    heads = range(H_C)
    rows = [slice(hh * HEAD_DIM, (hh + 1) * HEAD_DIM) for hh in heads]

    def scores(j):
        off = pl.multiple_of(j * KEY_BLOCK, KEY_BLOCK)
        kb = k_ref[pl.ds(off, KEY_BLOCK), :]
        return [_dot_nt(kb, qm_ref[hh]) for hh in heads]

    def step(j, qk, overlap=None, mask=None):
        off = pl.multiple_of(j * KEY_BLOCK, KEY_BLOCK)
        fk = fk_ref[pl.ds(off, KEY_BLOCK), :]
        vts = [vt_ref[j, rows[hh], :] for hh in heads]
        fq, m_all, l_all = fq_ref[...], m_ref[...], l_ref[...]
        accs = [acc_ref[rows[hh], :] for hh in heads]
        upcoming = overlap[0]() if overlap else None
        zs = [qk[hh] + (fq[hh:hh + 1, :] - fk[:, hh:hh + 1]) for hh in heads]
        if mask is not None:
            zs = [jnp.where(mask, z, -jnp.inf) for z in zs]
        m_new = [jnp.maximum(m_all[hh:hh + 1, :], jnp.max(zs[hh], axis=0, keepdims=True)) for hh in heads]
        m_use = [_finite_or_zero(m) for m in m_new]
        ps = [jnp.exp2(zs[hh] - m_use[hh]) for hh in heads]
        alpha = [jnp.exp2(m_all[hh:hh + 1, :] - m_use[hh]) for hh in heads]
        l_new = [alpha[hh] * l_all[hh:hh + 1, :] + jnp.sum(ps[hh], axis=0, keepdims=True) for hh in heads]
        pvs = [_dot(vts[hh], ps[hh].astype(BF16)) for hh in heads]
        for hh in heads:
            m_ref[hh:hh + 1, :] = m_new[hh]
            l_ref[hh:hh + 1, :] = l_new[hh]
            acc_ref[rows[hh], :] = accs[hh] * alpha[hh] + pvs[hh]
        if overlap:
            overlap[1](upcoming)

    def store(slot, qk):
        for hh in heads:
            z_ref[slot, hh] = qk[hh]

    def load(slot):
        return [z_ref[slot, hh] for hh in heads]

    def masked_block(j, overlap=None):
        q_pos, k_pos = _positions(q_start, tq, j, tq)
        step(j, scores(j), overlap=overlap, mask=k_pos <= q_pos)

    def masked_body(i, carry):
        masked_block(n_total - 1 - i)
        return carry

    lax.fori_loop(0, n_total - n_full - 1, masked_body, 0)
    _pipelined_blocks(n_full, scores, step, store, load, lead=functools.partial(masked_block, n_full))
    for hh in range(H_C):
        rows = slice(hh * HEAD_DIM, (hh + 1) * HEAD_DIM)
        acc_ref[rows, :] = acc_ref[rows, :] * (1.0 / l_ref[hh:hh + 1, :])
    o_ref[...] = acc_ref[...].T


def _fox_attention(q, kv, fq, fk, q_pos0, n_valid):
    b, tq_all, w = q.shape
    tk = fk.shape[1]
    tq = min(tq_all, KEY_BLOCK)
    qspec = pl.BlockSpec((None, tq, w), lambda bi, qi: (bi, qi, 0))
    kv_specs, kv_scratch = _kv_operands(kv)
    fqspec = pl.BlockSpec((None, F_ROWS, tq), lambda bi, qi: (bi, 0, qi))
    fkspec = pl.BlockSpec((None, tk, F_ROWS), lambda bi, qi: (bi, 0, 0))
    return pl.pallas_call(
        functools.partial(_fox_kernel, tq=tq, q_pos0=q_pos0, n_valid=n_valid, assemble=bool(kv_scratch)),
        grid=(b, tq_all // tq),
        in_specs=[qspec] + kv_specs + [fqspec, fkspec],
        out_specs=qspec,
        out_shape=jax.ShapeDtypeStruct((b, tq_all, w), F32),
        scratch_shapes=kv_scratch + [
            pltpu.VMEM((H_C, tq, w), BF16), pltpu.VMEM((2, H_C, KEY_BLOCK, tq), F32),
            pltpu.VMEM((w, tq), F32), pltpu.VMEM((8, tq), F32), pltpu.VMEM((8, tq), F32)],
        compiler_params=_params(2),
        name="fox_attention",
    )(q, *kv, fq, fk)


def _assemble_latent_keys(ckv_c_ref, krt_c_ref, kl_n_ref, kl_ref):
    past_len, n_new = ckv_c_ref.shape[0], kl_n_ref.shape[0]
    assert past_len % KEY_BLOCK == 0 and n_new <= KEY_BLOCK and kl_ref.shape[0] == past_len + KEY_BLOCK
    lane_pad = jnp.zeros((LANES - krt_c_ref.shape[0], KEY_BLOCK), F32)
    for j in range(past_len // KEY_BLOCK):
        rows = slice(j * KEY_BLOCK, (j + 1) * KEY_BLOCK)
        kl_ref[rows, :KV_LORA] = ckv_c_ref[rows, :].astype(BF16)
        kl_ref[rows, KV_LORA:] = jnp.concatenate([krt_c_ref[:, rows], lane_pad], axis=0).T.astype(BF16)
    kl_ref[past_len:past_len + n_new, :] = kl_n_ref[...]
    if n_new < KEY_BLOCK:
        kl_ref[past_len + n_new:, :] = jnp.zeros((KEY_BLOCK - n_new, kl_ref.shape[1]), BF16)


def _mla_kernel(*refs, tq, q_pos0, n_valid, assemble):
    if assemble:
        q_ref, ckv_c_ref, krt_c_ref, kl_n_ref, wuvt_ref, o_ref, kl_ref, ct_ref, z_ref, acc_ref, m_ref, l_ref = refs
    else:
        q_ref, kl_ref, wuvt_ref, o_ref, ct_ref, z_ref, acc_ref, m_ref, l_ref = refs

    @pl.when(pl.program_id(1) == 0)
    def _():
        if assemble:
            _assemble_latent_keys(ckv_c_ref, krt_c_ref, kl_n_ref, kl_ref)
        _transpose_blocks(kl_ref, ct_ref, KV_LORA)

    q_start = q_pos0 + pl.program_id(1) * tq
    n_full, n_total = _block_range(q_start, tq, n_valid)
    cols = H_B * tq
    qs = q_ref[...].reshape(cols, 2 * LANES)
    acc_ref[...] = jnp.zeros(acc_ref.shape, F32)
    m_ref[...] = jnp.full(m_ref.shape, -jnp.inf, F32)
    l_ref[...] = jnp.zeros(l_ref.shape, F32)

    group = KEY_BLOCK
    groups = [slice(g * group, (g + 1) * group) for g in range(cols // group)]

    gs = range(len(groups))

    def scores(j):
        off = pl.multiple_of(j * KEY_BLOCK, KEY_BLOCK)
        kb = kl_ref[pl.ds(off, KEY_BLOCK), :]
        return [_dot_nt(kb, qs[g]) for g in groups]

    def step(j, zs, overlap=None):
        ct = ct_ref[j]
        m_all, l_all = m_ref[...], l_ref[...]
        accs = [acc_ref[:, g] for g in groups]
        upcoming = overlap[0]() if overlap else None
        m_new = [jnp.maximum(m_all[:, groups[gi]], jnp.max(zs[gi], axis=0, keepdims=True)) for gi in gs]
        m_use = [_finite_or_zero(m) for m in m_new]
        ps = [jnp.exp2(zs[gi] - m_use[gi]) for gi in gs]
        alpha = [jnp.exp2(m_all[:, groups[gi]] - m_use[gi]) for gi in gs]
        l_new = [alpha[gi] * l_all[:, groups[gi]] + jnp.sum(ps[gi], axis=0, keepdims=True) for gi in gs]
        pvs = [_dot(ct, ps[gi].astype(BF16)) for gi in gs]
        for gi, g in enumerate(groups):
            m_ref[:, g] = m_new[gi]
            l_ref[:, g] = l_new[gi]
            acc_ref[:, g] = accs[gi] * alpha[gi] + pvs[gi]
        if overlap:
            overlap[1](upcoming)

    def store(slot, zs):
        for gi, g in enumerate(groups):
            z_ref[slot, :, g] = zs[gi]

    def load(slot):
        return [z_ref[slot, :, g] for g in groups]

    def masked_block(j, overlap=None):
        q_pos, k_pos = _positions(q_start, tq, j, group)
        mask = (k_pos // CHUNK <= q_pos // CHUNK) & (k_pos < n_valid)
        step(j, [jnp.where(mask, z, -jnp.inf) for z in scores(j)], overlap=overlap)

    def masked_body(i, carry):
        masked_block(n_total - 1 - i)
        return carry

    lax.fori_loop(0, n_total - n_full - 1, masked_body, 0)
    _pipelined_blocks(n_full, scores, step, store, load, lead=functools.partial(masked_block, n_full))

    lat = (acc_ref[...] * (1.0 / l_ref[...])).astype(BF16)
    heads = [_dot(wuvt_ref[hh], lat[:, hh * tq:(hh + 1) * tq]) for hh in range(H_B)]
    o_ref[...] = jnp.concatenate(heads, axis=0).T


def _mla_attention(qm, keys, wuvt, layer, tq_all, q_pos0, n_valid):
    b, w = qm.shape[1], qm.shape[3]
    assemble = len(keys) > 1
    tk = keys[0].shape[1] + KEY_BLOCK if assemble else keys[0].shape[1]
    tq = KEY_BLOCK if tq_all % KEY_BLOCK == 0 else LANES
    cols = H_B * tq
    key_specs = [pl.BlockSpec((None,) + a.shape[1:], lambda bi, qi: (bi, 0, 0)) for a in keys]
    return pl.pallas_call(
        functools.partial(_mla_kernel, tq=tq, q_pos0=q_pos0, n_valid=n_valid, assemble=assemble),
        grid=(b, tq_all // tq),
        in_specs=[pl.BlockSpec((H_B, None, tq, w), lambda bi, qi: (0, bi, qi, 0))] + key_specs + [_layer_spec(wuvt, layer)],
        out_specs=pl.BlockSpec((None, tq, W_B), lambda bi, qi: (bi, qi, 0)),
        out_shape=jax.ShapeDtypeStruct((b, tq_all, W_B), F32),
        scratch_shapes=([pltpu.VMEM((tk, w), BF16)] if assemble else []) + [
            pltpu.VMEM((tk // KEY_BLOCK, KV_LORA, KEY_BLOCK), BF16), pltpu.VMEM((2, KEY_BLOCK, cols), F32),
            pltpu.VMEM((KV_LORA, cols), F32), pltpu.VMEM((1, cols), F32), pltpu.VMEM((1, cols), F32)],
        compiler_params=_params(2),
        name="mla_attention",
    )(qm, *keys, wuvt)


def _post_kernel(h_ref, oa_ref, ob_ref, oc_ref, p_ref, ggrp_ref, wout_ref, gmix_ref,
                 gpre_ref, gpost_ref, wgu_ref, wdown_ref, gple_pre_ref, wgate_ref, wproj_ref, gple_post_ref, o_ref,
                 *, d_ff, chunks):
    m = jnp.zeros(h_ref.shape, F32)
    c0 = 0
    for o_grp in (oa_ref, ob_ref, oc_ref):
        c1 = c0 + o_grp.shape[1]
        m = m + _dot(_rms(o_grp[...], ggrp_ref[:, c0:c1]).astype(BF16), wout_ref[c0:c1, :])
        c0 = c1
    h = h_ref[...] + _rms(m, gmix_ref[...])
    h = _ffn_rows(h, gpre_ref, gpost_ref, wgu_ref, wdown_ref, d_ff, chunks)
    gate = jax.nn.sigmoid(_dot(_rms(h, gple_pre_ref[...]).astype(BF16), wgate_ref[...]))
    e = _dot(p_ref[...].astype(BF16), wproj_ref[...]) * gate
    o_ref[...] = h + _rms(e, gple_post_ref[...])


def _post(h, oa, ob, oc, p, layer, consts):
    n, d = h.shape
    tm = _row_tile(n)
    row = lambda w: pl.BlockSpec((tm, w), lambda i: (i, 0))
    p_spec = pl.BlockSpec((None, tm, p.shape[2]), lambda i: (layer, i, 0))
    d_ff = consts[6].shape[1]
    return pl.pallas_call(
        functools.partial(_post_kernel, d_ff=d_ff, chunks=_ffn_chunks(d_ff)),
        grid=(n // tm,),
        in_specs=[row(d), row(W_A), row(W_B), row(W_C), p_spec] + [_layer_spec(c, layer) for c in consts],
        out_specs=row(d),
        out_shape=jax.ShapeDtypeStruct((n, d), F32),
        compiler_params=_params(1),
        name="post",
    )(h, oa, ob, oc, p, *consts)


def _rope_tables(pos):
    half = MLA_ROPE // 2
    inv = ROPE_THETA ** (-jnp.arange(half, dtype=F32) / half)
    ang = pos.astype(F32)[:, None] * inv[None, :]
    cos, sin = jnp.cos(ang), jnp.sin(ang)
    pad = jnp.zeros((pos.shape[0], LANES - MLA_ROPE), F32)
    return jnp.concatenate([cos, cos, pad], axis=1), jnp.concatenate([-sin, sin, pad], axis=1)


def _prep_weights(weights):
    (g_ff1_pre, g_ff1_post, w_ff1_gu, w_ff1_down, g_mix_pre, g_mix_post, w_in, b_f, g_bq, g_bkv, w_uq, w_ukv,
     g_grp, w_out, g_ff2_pre, g_ff2_post, w_ff2_gu, w_ff2_down, g_ple_pre, w_ple_gate, w_ple_proj, g_ple_post) = weights
    depth, d = w_in.shape[:2]
    half = MLA_ROPE // 2
    row = lambda g: g.reshape(depth, 1, -1).astype(F32)
    bf = lambda w: w.astype(BF16)
    c_kr = 3 * W_A + Q_LORA + KV_LORA
    c_c = c_kr + MLA_ROPE
    c_f = c_c + 3 * W_C
    w_in = bf(w_in)
    kr = w_in[:, :, c_kr:c_kr + MLA_ROPE]
    kr_sw = jnp.concatenate([kr[:, :, half:], kr[:, :, :half]], axis=2)
    zpad = jnp.zeros((depth, d, LANES - MLA_ROPE), BF16)
    win_p = jnp.concatenate([w_in[:, :, :c_kr], kr, zpad, kr_sw, zpad, w_in[:, :, c_c:c_f], w_in[:, :, c_f:],
                             jnp.zeros((depth, d, LANES - H_C), BF16)], axis=2)
    assert win_p.shape[2] == IN_COLS_P
    bf_p = jnp.pad(b_f.astype(F32), ((0, 0), (0, LANES - H_C))).reshape(depth, 1, LANES)
    wkvt = jnp.transpose(jnp.concatenate([w_in[:, :, W_A:3 * W_A], w_in[:, :, c_c + W_C:c_f]], axis=2), (0, 2, 1))

    wq4 = bf(w_uq).reshape(depth, Q_LORA, H_B, MLA_NOPE + MLA_ROPE)
    wkv4 = bf(w_ukv).reshape(depth, KV_LORA, H_B, MLA_NOPE + MLA_V)
    wq_n = jnp.transpose(wq4[..., :MLA_NOPE], (0, 2, 1, 3))
    wk_n = jnp.transpose(wkv4[..., :MLA_NOPE], (0, 2, 1, 3))
    wcomb = _wcomb(wq_n, wk_n)
    x1 = wq4[..., MLA_NOPE:MLA_NOPE + half]
    x2 = wq4[..., MLA_NOPE + half:]
    zq = jnp.zeros((depth, Q_LORA, H_B, LANES - MLA_ROPE), BF16)
    wqr = jnp.concatenate([jnp.concatenate([x1, x2, zq], axis=3).reshape(depth, Q_LORA, H_B * LANES),
                           jnp.concatenate([x2, x1, zq], axis=3).reshape(depth, Q_LORA, H_B * LANES)], axis=2)
    wuvt = jnp.transpose(wkv4[..., MLA_NOPE:], (0, 2, 3, 1))

    return dict(
        ff1=(row(g_ff1_pre), row(g_ff1_post), bf(w_ff1_gu), bf(w_ff1_down)),
        inproj=(row(g_mix_pre), win_p, bf_p, row(g_bq), row(g_bkv), wcomb, wqr, wkvt),
        wuvt=wuvt,
        post=(row(g_grp), bf(w_out), row(g_mix_post),
              row(g_ff2_pre), row(g_ff2_post), bf(w_ff2_gu), bf(w_ff2_down),
              row(g_ple_pre), bf(w_ple_gate), bf(w_ple_proj), row(g_ple_post)),
    )


def _pad_keys(a, tk_pad):
    return jnp.pad(a, ((0, 0), (0, tk_pad - a.shape[1])) + ((0, 0),) * (a.ndim - 2))


def _forget_sums(logf_rows, tk_pad):
    _, heads, tk = logf_rows.shape
    return _cumsum_rows(jnp.pad(logf_rows, ((0, 0), (0, F_ROWS - heads), (0, tk_pad - tk))))


def _layer(h, p, seq_len, q_pos0, past, layer, lp, tables, prev_states):
    n, d = h.shape
    b = n // seq_len
    h = _ffn(h, layer, *lp["ff1"])
    (ka_st, va_st, ckv_st, kr_st, kc_st, vc_st, lf_st,
     qa_b, ka_b, va_b, qm_b, kl_b, qc_b, kc_b, vc_b), feature_major = _inproj(h, seq_len, layer, prev_states,
                                                                                *lp["inproj"], *tables)
    if feature_major:
        state = (ka_st, va_st, ckv_st, kr_st, kc_st, vc_st, lf_st)
        lf_rows_new = lf_st[layer]
    else:
        state = (ka_st.reshape(b, seq_len, H_A, HEAD_DIM), va_st.reshape(b, seq_len, H_A, HEAD_DIM),
                 ckv_st.reshape(b, seq_len, KV_LORA), kr_st.reshape(b, seq_len, MLA_ROPE),
                 kc_st.reshape(b, seq_len, H_C, HEAD_DIM), vc_st.reshape(b, seq_len, H_C, HEAD_DIM),
                 lf_st.reshape(b, seq_len, H_C))
        lf_rows_new = jnp.transpose(state[6], (0, 2, 1))
    seq3 = lambda a: a.reshape(b, seq_len, a.shape[-1])
    tq_pad = -(-seq_len // LANES) * LANES
    to_blocks = lambda v: jnp.transpose(v.reshape(b, v.shape[1] // KEY_BLOCK, KEY_BLOCK, v.shape[2]), (0, 1, 3, 2))
    if past is None:
        n_valid = seq_len
        tk_pad = -(-tq_pad // KEY_BLOCK) * KEY_BLOCK
        ka_all, kl_all, kc_all = (_pad_keys(seq3(a), tk_pad) for a in (ka_b, kl_b, kc_b))
        if feature_major:
            va_all, vc_all = (v.reshape(b, seq_len // KEY_BLOCK, v.shape[1], KEY_BLOCK) for v in (va_b, vc_b))
        else:
            va_all, vc_all = (to_blocks(_pad_keys(seq3(v), tk_pad)) for v in (va_b, vc_b))
        kv_a, kv_c, keys_b = (ka_all, va_all), (kc_all, vc_all), (kl_all,)
        lf_rows = lf_rows_new
    else:
        assert not feature_major
        pa_k, pa_v, pb_ckv, pb_kr, pc_k, pc_v, pc_lf = past
        past_len = pa_k.shape[1]
        n_valid = past_len + seq_len
        tk_pad = -(-(past_len + tq_pad) // KEY_BLOCK) * KEY_BLOCK
        join = lambda c, new: _pad_keys(jnp.concatenate([c.reshape(b, past_len, -1).astype(BF16), seq3(new)], axis=1), tk_pad)
        if past_len % KEY_BLOCK == 0 and tk_pad == past_len + KEY_BLOCK:
            wide = lambda c: jnp.transpose(c, (0, 2, 3, 1)).reshape(b, -1, past_len)
            new_k = lambda k: _pad_keys(seq3(k), tq_pad)
            new_vt = lambda v: jnp.transpose(_pad_keys(seq3(v), tq_pad), (0, 2, 1))
            kv_a = (wide(pa_k), wide(pa_v), new_k(ka_b), new_vt(va_b))
            kv_c = (wide(pc_k), wide(pc_v), new_k(kc_b), new_vt(vc_b))
            keys_b = (pb_ckv, jnp.transpose(pb_kr, (0, 2, 1)), new_k(kl_b))
        else:
            kv_a = (join(pa_k, ka_b), to_blocks(join(pa_v, va_b)))
            kv_c = (join(pc_k, kc_b), to_blocks(join(pc_v, vc_b)))
            kl_past = jnp.concatenate([pb_ckv, pb_kr, jnp.zeros((b, past_len, LANES - MLA_ROPE), F32)], axis=-1)
            keys_b = (join(kl_past, kl_b),)
        lf_rows = jnp.concatenate([jnp.transpose(pc_lf, (0, 2, 1)), lf_rows_new], axis=2)
    f_rows = _forget_sums(lf_rows, tk_pad)
    f_q = f_rows[:, :, q_pos0:q_pos0 + tq_pad]
    f_k = jnp.transpose(f_rows, (0, 2, 1))
    pad_q = lambda a: jnp.pad(a, ((0, 0),) * (a.ndim - 2) + ((0, tq_pad - seq_len), (0, 0)))

    oa = _sb_attention(pad_q(seq3(qa_b)), kv_a, q_pos0, n_valid)[:, :seq_len]
    ob = _mla_attention(pad_q(qm_b.reshape(H_B, b, seq_len, 2 * LANES)), keys_b, lp["wuvt"], layer, tq_pad, q_pos0,
                        n_valid)[:, :seq_len]
    oc = _fox_attention(pad_q(seq3(qc_b)), kv_c, f_q, f_k, q_pos0, n_valid)[:, :seq_len]
    flat = lambda a: a.reshape(n, a.shape[-1])
    h = _post(h, flat(oa), flat(ob), flat(oc), p, layer, lp["post"])
    return h, state, feature_major


def _trunk(x, p, q_pos0, caches, lp, depth):
    b, t, d = x.shape
    tables = _rope_tables(q_pos0 + jnp.arange(t, dtype=jnp.int32))
    h = x.reshape(b * t, d)
    p = p.reshape(depth, b * t, -1)
    states, st, stacked_mode = [], None, False
    for i in range(depth):
        past = None if caches is None else [c[i] for c in caches]
        h, st, stacked_mode = _layer(h, p, t, q_pos0, past, i, lp, tables, st if stacked_mode else None)
        states.append(st)
    if stacked_mode:
        heads5 = lambda a, nh: jnp.transpose(a.reshape(depth, b, nh, HEAD_DIM, t), (0, 1, 4, 2, 3))
        swap = lambda a: jnp.transpose(a, (0, 1, 3, 2))
        ka, va, ckv, kr, kc, vc, lf = st
        stacked = [heads5(ka, H_A), heads5(va, H_A), ckv.reshape(depth, b, t, KV_LORA), swap(kr),
                   heads5(kc, H_C), heads5(vc, H_C), swap(lf)]
    else:
        stacked = [jnp.stack([s[j] for s in states]) for j in range(N_STATE)]
    return h.reshape(b, t, d), stacked


def kernel(x_prompt, x_sample, p_prompt, p_sample, cache_a_k, cache_a_v, cache_b_ckv, cache_b_krope, cache_c_k, cache_c_v, cache_c_logf, g_ff1_pre, g_ff1_post, w_ff1_gu, w_ff1_down, g_mix_pre, g_mix_post, w_in, b_f, g_bq, g_bkv, w_uq, w_ukv, g_grp, w_out, g_ff2_pre, g_ff2_post, w_ff2_gu, w_ff2_down, g_ple_pre, w_ple_gate, w_ple_proj, g_ple_post):
    weights = (g_ff1_pre, g_ff1_post, w_ff1_gu, w_ff1_down, g_mix_pre, g_mix_post, w_in, b_f,
               g_bq, g_bkv, w_uq, w_ukv, g_grp, w_out, g_ff2_pre, g_ff2_post, w_ff2_gu, w_ff2_down,
               g_ple_pre, w_ple_gate, w_ple_proj, g_ple_post)
    depth = w_in.shape[0]
    lp = _prep_weights(weights)
    y_prompt, sp = _trunk(x_prompt, p_prompt, 0, None, lp, depth)
    caches = (cache_a_k, cache_a_v, cache_b_ckv, cache_b_krope, cache_c_k, cache_c_v, cache_c_logf)
    y_sample, ss = _trunk(x_sample, p_sample, cache_a_k.shape[2], caches, lp, depth)
    return (y_prompt, y_sample, *sp, *ss)
```

```python
import functools
import math

import jax
import jax.numpy as jnp
from jax import lax
from jax.experimental import pallas as pl
from jax.experimental.pallas import tpu as pltpu

CHUNK = 64
HEAD_DIM = 64
H_A = 4
H_B = 8
H_C = 4
W_A = H_A * HEAD_DIM
MLA_NOPE = 64
MLA_ROPE = 32
MLA_V = 64
W_B = H_B * MLA_V
W_C = H_C * HEAD_DIM
Q_LORA = 256
KV_LORA = 128
ROPE_THETA = 10000.0
EPS = 1e-6
FFN_RES = 0.5
SB_SCALE = HEAD_DIM ** -0.5
MLA_SCALE = (MLA_NOPE + MLA_ROPE) ** -0.5
FOX_SCALE = HEAD_DIM ** -0.5
LOG2E = math.log2(math.e)

LANES = 128
KEY_BLOCK = 256
F_ROWS = 16
VMEM_LIMIT = 56 * 1024 * 1024

COL_A = 0
COL_CQ = COL_A + 3 * W_A
COL_CKV = COL_CQ + Q_LORA
COL_KRA = COL_CKV + KV_LORA
COL_KRB = COL_KRA + LANES
COL_C = COL_KRB + LANES
COL_F = COL_C + 3 * W_C
IN_COLS_P = COL_F + LANES

BF16 = jnp.bfloat16
F32 = jnp.float32


def _dot(a, b):
    return jnp.dot(a, b, preferred_element_type=F32)


def _dot_nt(a, b):
    return lax.dot_general(a, b, (((1,), (1,)), ((), ())), preferred_element_type=F32)


def _rms(x, g):
    ms = jnp.mean(x * x, axis=-1, keepdims=True)
    return x * lax.rsqrt(ms + EPS) * g


def _log_sigmoid(x):
    return jnp.minimum(x, 0.0) - jnp.log(1.0 + jnp.exp(-jnp.abs(x)))


def _softplus(x):
    return jnp.maximum(x, 0.0) + jnp.log(1.0 + jnp.exp2(jnp.abs(x) * (-LOG2E)))


def _split2(x):
    hi = x.astype(BF16)
    lo = (x - hi.astype(F32)).astype(BF16)
    return hi, lo


def _layer_spec(a, layer):
    idx = (layer,) + (0,) * (a.ndim - 1)
    return pl.BlockSpec((None,) + a.shape[1:], lambda *_: idx, pipeline_mode=pl.Buffered(1))


def _params(n_axes):
    return pltpu.CompilerParams(dimension_semantics=("arbitrary",) * n_axes, vmem_limit_bytes=VMEM_LIMIT)


def _row_tile(n):
    for tm in (512, 256):
        if n % tm == 0:
            return tm
    return n


def _ffn_rows(h, gpre_ref, gpost_ref, wgu_ref, wdown_ref, d_ff, chunks):
    xn = _rms(h, gpre_ref[...]).astype(BF16)
    acc = jnp.zeros(h.shape, F32)
    for c0, c1 in chunks:
        g = _dot(xn, wgu_ref[:, c0:c1])
        u = _dot(xn, wgu_ref[:, d_ff + c0:d_ff + c1])
        a = (g * jax.nn.sigmoid(g) * u).astype(BF16)
        acc = acc + _dot(a, wdown_ref[c0:c1, :])
    return h + FFN_RES * _rms(acc, gpost_ref[...])


def _ffn_chunks(d_ff):
    step = 4 * KEY_BLOCK
    return tuple((c, min(c + step, d_ff)) for c in range(0, d_ff, step))


def _ffn_kernel(h_ref, gpre_ref, gpost_ref, wgu_ref, wdown_ref, o_ref, *, d_ff, chunks):
    o_ref[...] = _ffn_rows(h_ref[...], gpre_ref, gpost_ref, wgu_ref, wdown_ref, d_ff, chunks)


def _ffn(h, layer, gpre, gpost, wgu, wdown):
    n, d = h.shape
    d_ff = wdown.shape[1]
    consts = [gpre, gpost, wgu, wdown]
    tm = _row_tile(n)
    chunks = _ffn_chunks(d_ff)
    row = pl.BlockSpec((tm, d), lambda i: (i, 0))
    return pl.pallas_call(
        functools.partial(_ffn_kernel, d_ff=d_ff, chunks=chunks),
        grid=(n // tm,),
        in_specs=[row] + [_layer_spec(c, layer) for c in consts],
        out_specs=row,
        out_shape=jax.ShapeDtypeStruct((n, d), F32),
        compiler_params=_params(1),
        name="ffn",
    )(h, gpre, gpost, wgu, wdown)


def _wcomb_kernel(wq_ref, wk_ref, o_ref):
    kl = wk_ref.shape[1]
    for hh in range(wq_ref.shape[0]):
        o_ref[:, hh * kl:(hh + 1) * kl] = _dot_nt(wq_ref[hh], wk_ref[hh]).astype(BF16)


def _wcomb(wq_n, wk_n):
    depth, nh, ql, dn = wq_n.shape
    kl = wk_n.shape[2]
    return pl.pallas_call(
        _wcomb_kernel,
        grid=(depth,),
        in_specs=[pl.BlockSpec((None, nh, ql, dn), lambda i: (i, 0, 0, 0)),
                  pl.BlockSpec((None, nh, kl, dn), lambda i: (i, 0, 0, 0))],
        out_specs=pl.BlockSpec((None, ql, nh * kl), lambda i: (i, 0, 0)),
        out_shape=jax.ShapeDtypeStruct((depth, ql, nh * kl), BF16),
        compiler_params=_params(1),
        name="wcomb",
    )(wq_n, wk_n)


def _inproj_kernel(*refs, feature_major, n_alias, fill_layers):
    (h_ref, gpre_ref, win_ref, bf_ref, gbq_ref, gbkv_ref, wcomb_ref, wqr_ref, wkvt_ref, cos_ref, sin_ref) = refs[:11]
    (ka_st, va_st, ckv_st, kr_st, kc_st, vc_st, lf_st,
     qa_b, ka_b, va_b, qm_b, kl_b, qc_b, kc_b, vc_b) = refs[11 + n_alias:]

    def put(st, val):
        if fill_layers:
            for slab in range(fill_layers):
                st[slab] = val
        else:
            st[...] = val

    xn = _rms(h_ref[...], gpre_ref[...]).astype(BF16)
    proj = _dot(xn, win_ref[...])
    cos = cos_ref[...]
    sin = sin_ref[...]

    qa_b[...] = (proj[:, COL_A:COL_A + W_A] * SB_SCALE).astype(BF16)
    qc_b[...] = (proj[:, COL_C:COL_C + W_C] * (FOX_SCALE * LOG2E)).astype(BF16)
    ka = proj[:, COL_A + W_A:COL_A + 2 * W_A]
    kc = proj[:, COL_C + W_C:COL_C + 2 * W_C]
    ka_b[...] = ka.astype(BF16)
    kc_b[...] = kc.astype(BF16)
    lf = _log_sigmoid(proj[:, COL_F:COL_F + LANES] + bf_ref[...])
    ckv = _rms(proj[:, COL_CKV:COL_CKV + KV_LORA], gbkv_ref[...])
    put(ckv_st, ckv)
    kr = proj[:, COL_KRA:COL_KRA + LANES] * cos + proj[:, COL_KRB:COL_KRB + LANES] * sin
    kl_b[:, :KV_LORA] = ckv.astype(BF16)
    kl_b[:, KV_LORA:] = kr.astype(BF16)

    if feature_major:
        kvt = _dot_nt(wkvt_ref[...], xn)
        for idx, st in enumerate((ka_st, va_st, kc_st, vc_st)):
            put(st, kvt[idx * W_A:(idx + 1) * W_A])
        for r in range(va_b.shape[0]):
            cols = slice(r * KEY_BLOCK, (r + 1) * KEY_BLOCK)
            va_b[r] = kvt[W_A:2 * W_A, cols].astype(BF16)
            vc_b[r] = kvt[3 * W_A:4 * W_A, cols].astype(BF16)
        put(kr_st, kr.T[:MLA_ROPE])
        put(lf_st, lf.T[:H_C])
    else:
        va = proj[:, COL_A + 2 * W_A:COL_A + 3 * W_A]
        vc = proj[:, COL_C + 2 * W_C:COL_C + 3 * W_C]
        ka_st[...] = ka
        va_st[...] = va
        kc_st[...] = kc
        vc_st[...] = vc
        va_b[...] = va.astype(BF16)
        vc_b[...] = vc.astype(BF16)
        kr_st[...] = kr[:, :MLA_ROPE]
        lf_st[...] = lf[:, :H_C]

    cqn = _rms(proj[:, COL_CQ:COL_CQ + Q_LORA], gbq_ref[...]).astype(BF16)
    qlat = _dot(cqn, wcomb_ref[...])
    qr = _dot(cqn, wqr_ref[...])
    half = H_B * LANES
    for hh in range(H_B):
        sl = slice(hh * LANES, (hh + 1) * LANES)
        rope = qr[:, sl] * cos + qr[:, half + hh * LANES:half + (hh + 1) * LANES] * sin
        qm_b[hh, :, :KV_LORA] = (qlat[:, sl] * (MLA_SCALE * LOG2E)).astype(BF16)
        qm_b[hh, :, KV_LORA:] = (rope * (MLA_SCALE * LOG2E)).astype(BF16)


N_STATE = 7


def _inproj(h, seq_len, layer, prev_states, gpre, win, bf, gbq, gbkv, wcomb, wqr, wkvt, cos_t, sin_t):
    n, d = h.shape
    b = n // seq_len
    depth = win.shape[0]
    tm = _row_tile(n)
    row = lambda w: pl.BlockSpec((tm, w), lambda i: (i, 0))
    consts = [gpre, win, bf, gbq, gbkv, wcomb, wqr, wkvt]
    feature_major = seq_len % tm == 0 and tm % KEY_BLOCK == 0
    fill_all = feature_major and prev_states is None
    if feature_major:
        per_seq = seq_len // tm
        tab = pl.BlockSpec((tm, LANES), lambda i: (i % per_seq, 0))
        lead, at = (depth, 0) if fill_all else (None, layer)
        narrow = lambda w: (pl.BlockSpec((lead, None, w, tm), lambda i: (at, i // per_seq, 0, i % per_seq)),
                            (depth, b, w, seq_len), F32)
        latent = (pl.BlockSpec((lead, tm, KV_LORA), lambda i: (at, i, 0)), (depth, n, KV_LORA), F32)
        vals = lambda w: (pl.BlockSpec((tm // KEY_BLOCK, w, KEY_BLOCK), lambda i: (i, 0, 0)),
                          (n // KEY_BLOCK, w, KEY_BLOCK), BF16)
    else:
        cos_t, sin_t = (jnp.tile(t, (b, 1)) for t in (cos_t, sin_t))
        tab = row(LANES)
        narrow = lambda w: (row(w), (n, w), F32)
        latent = narrow(KV_LORA)
        vals = lambda w: (row(w), (n, w), BF16)
    outs = [
        narrow(W_A), narrow(W_A), latent, narrow(MLA_ROPE), narrow(W_C), narrow(W_C), narrow(H_C),
        (row(W_A), (n, W_A), BF16), (row(W_A), (n, W_A), BF16), vals(W_A),
        (pl.BlockSpec((H_B, tm, 2 * LANES), lambda i: (0, i, 0)), (H_B, n, 2 * LANES), BF16),
        (row(2 * LANES), (n, 2 * LANES), BF16),
        (row(W_C), (n, W_C), BF16), (row(W_C), (n, W_C), BF16), vals(W_C),
    ]
    ins = [h, *consts, cos_t, sin_t]
    in_specs = [row(d)] + [_layer_spec(c, layer) for c in consts] + [tab, tab]
    aliases = {}
    if feature_major and prev_states is not None:
        aliases = {len(ins) + s: s for s in range(N_STATE)}
        ins = ins + list(prev_states)
        in_specs = in_specs + [pl.BlockSpec(memory_space=pl.ANY)] * N_STATE
    res = pl.pallas_call(
        functools.partial(_inproj_kernel, feature_major=feature_major, n_alias=len(aliases),
                          fill_layers=depth if fill_all else 0),
        grid=(n // tm,),
        in_specs=in_specs,
        out_specs=[o[0] for o in outs],
        out_shape=[jax.ShapeDtypeStruct(o[1], o[2]) for o in outs],
        input_output_aliases=aliases,
        compiler_params=_params(1),
        name="inproj",
    )(*ins)
    return res, feature_major


def _cumsum_kernel(x_ref, o_ref, *, n_blocks):
    r = lax.broadcasted_iota(jnp.int32, (KEY_BLOCK, KEY_BLOCK), 0)
    c = lax.broadcasted_iota(jnp.int32, (KEY_BLOCK, KEY_BLOCK), 1)
    upper = jnp.where(r <= c, 1.0, 0.0).astype(BF16)
    carry = jnp.zeros((x_ref.shape[0], 1), F32)
    for j in range(n_blocks):
        sl = slice(j * KEY_BLOCK, (j + 1) * KEY_BLOCK)
        x = x_ref[:, sl]
        hi = x.astype(BF16)
        mid, lo = _split2(x - hi.astype(F32))
        f = _dot(hi, upper) + _dot(mid, upper) + _dot(lo, upper) + carry
        o_ref[:, sl] = f * LOG2E
        carry = f[:, KEY_BLOCK - 1:KEY_BLOCK]


def _cumsum_rows(x):
    b, r, tp = x.shape
    spec = pl.BlockSpec((None, r, tp), lambda i: (i, 0, 0))
    return pl.pallas_call(
        functools.partial(_cumsum_kernel, n_blocks=tp // KEY_BLOCK),
        grid=(b,),
        in_specs=[spec],
        out_specs=spec,
        out_shape=jax.ShapeDtypeStruct(x.shape, F32),
        compiler_params=_params(1),
        name="cumsum_logf",
    )(x)


def _block_range(q_start, tq, n_valid):
    n_full = q_start // KEY_BLOCK
    last = jnp.minimum(((q_start + tq + CHUNK - 1) // CHUNK) * CHUNK, n_valid)
    n_total = (last + KEY_BLOCK - 1) // KEY_BLOCK
    return n_full, n_total


def _positions(q_start, tq, j, cols):
    k_pos = j * KEY_BLOCK + lax.broadcasted_iota(jnp.int32, (KEY_BLOCK, cols), 0)
    lane = lax.broadcasted_iota(jnp.int32, (KEY_BLOCK, cols), 1)
    q_pos = q_start + (lane if cols == tq else lane % tq)
    return q_pos, k_pos


def _transpose_blocks(src_ref, dst_ref, width):
    for j in range(dst_ref.shape[0]):
        blk = src_ref[j * KEY_BLOCK:(j + 1) * KEY_BLOCK, :width].astype(F32)
        dst_ref[j] = blk.T.astype(BF16)


def _assemble_keys_values(kt_c_ref, vt_c_ref, k_n_ref, vt_n_ref, k_ref, vt_ref):
    past_len = kt_c_ref.shape[1]
    n_past, n_new = past_len // KEY_BLOCK, k_n_ref.shape[0]
    assert past_len % KEY_BLOCK == 0 and n_new <= KEY_BLOCK and vt_ref.shape[0] == n_past + 1
    for j in range(n_past):
        cols = slice(j * KEY_BLOCK, (j + 1) * KEY_BLOCK)
        k_ref[cols, :] = kt_c_ref[:, cols].T.astype(BF16)
        vt_ref[j] = vt_c_ref[:, cols].astype(BF16)
    k_ref[past_len:past_len + n_new, :] = k_n_ref[...]
    vt_ref[n_past, :, :n_new] = vt_n_ref[...]
    if n_new < KEY_BLOCK:
        k_ref[past_len + n_new:, :] = jnp.zeros((KEY_BLOCK - n_new, k_ref.shape[1]), BF16)
        vt_ref[n_past, :, n_new:] = jnp.zeros((vt_ref.shape[1], KEY_BLOCK - n_new), BF16)


def _pipelined_blocks(n_full, produce, consume, store, load, lead):
    f = lambda i: n_full - 1 - i
    n_pairs = jnp.maximum(n_full - 1, 0) // 2
    rest = n_full - 2 * n_pairs

    def both(i, slot):
        consume(f(i), load(slot), overlap=(lambda: produce(f(i + 1)), lambda vals: store(1 - slot, vals)))

    @pl.when(n_full > 0)
    def _():
        lead((lambda: produce(f(0)), lambda vals: store(0, vals)))

    @pl.when(n_full <= 0)
    def _():
        lead(None)

    def pair_body(t, carry):
        both(2 * t, 0)
        both(2 * t + 1, 1)
        return carry

    lax.fori_loop(0, n_pairs, pair_body, 0)

    @pl.when((n_full > 0) & (rest == 1))
    def _():
        consume(f(2 * n_pairs), load(0))

    @pl.when((n_full > 0) & (rest == 2))
    def _():
        both(2 * n_pairs, 0)
        consume(f(2 * n_pairs + 1), load(1))


def _finite_or_zero(m):
    return jnp.where(m == -jnp.inf, 0.0, m)


def _head_masks(n_heads):
    lane = lax.broadcasted_iota(jnp.int32, (1, n_heads * HEAD_DIM), 1)
    return [(lane >= hh * HEAD_DIM) & (lane < (hh + 1) * HEAD_DIM) for hh in range(n_heads)]


def _sb_kernel(*refs, tq, q_pos0, n_valid, assemble):
    if assemble:
        q_ref, kt_c_ref, vt_c_ref, k_n_ref, vt_n_ref, o_ref, k_ref, vt_ref, qm_ref, z_ref, x_ref, acc_ref, c_ref = refs

        @pl.when(pl.program_id(1) == 0)
        def _():
            _assemble_keys_values(kt_c_ref, vt_c_ref, k_n_ref, vt_n_ref, k_ref, vt_ref)
    else:
        q_ref, k_ref, vt_ref, o_ref, qm_ref, z_ref, x_ref, acc_ref, c_ref = refs
    q_start = q_pos0 + pl.program_id(1) * tq
    n_full, n_total = _block_range(q_start, tq, n_valid)
    hmask = _head_masks(H_A)
    q = q_ref[...]
    for hh in range(H_A):
        qm_ref[hh] = jnp.where(hmask[hh], q, jnp.zeros_like(q))
    acc_ref[...] = jnp.zeros(acc_ref.shape, F32)
    c_ref[...] = jnp.zeros(c_ref.shape, F32)
    half = KEY_BLOCK // 2
    r = lax.broadcasted_iota(jnp.int32, (half, KEY_BLOCK), 0)
    c = lax.broadcasted_iota(jnp.int32, (half, KEY_BLOCK), 1)
    neg_upper2 = jnp.where(c % half >= r, -1.0, 0.0).astype(BF16)
    heads = range(H_A)
    rows = [slice(hh * HEAD_DIM, (hh + 1) * HEAD_DIM) for hh in heads]

    def parts(x):
        hi, lo = _split2(x)
        return jnp.concatenate([hi, lo], axis=0)

    def first_stage(j, mask=None):
        off = pl.multiple_of(j * KEY_BLOCK, KEY_BLOCK)
        kb = k_ref[pl.ds(off, KEY_BLOCK), :]
        zs = [_dot_nt(kb, qm_ref[hh]) for hh in heads]
        sps = [_softplus(z) for z in zs]
        if mask is not None:
            sps = [jnp.where(mask, sp, 0.0) for sp in sps]
        return zs, [parts(sp[:half]) for sp in sps], [parts(sp[half:]) for sp in sps]

    def second_stage(j, vals, overlap=None, mask=None):
        zs, early, late = vals
        vts = [vt_ref[j, rows[hh], :] for hh in heads]
        carry = c_ref[...]
        upcoming = overlap[0]() if overlap else None
        cum_late = [_dot(neg_upper2, late[hh]) + carry[hh:hh + 1, :] for hh in heads]
        cum_early = [_dot(neg_upper2, early[hh]) + cum_late[hh][0:1, :] for hh in heads]
        ws = [jnp.exp(zs[hh] + jnp.concatenate([cum_early[hh], cum_late[hh]], axis=0)) for hh in heads]
        if mask is not None:
            ws = [jnp.where(mask, w, 0.0) for w in ws]
        pvs = [_dot(vts[hh], ws[hh].astype(BF16)) for hh in heads]
        for hh in heads:
            acc_ref[rows[hh], :] += pvs[hh]
            c_ref[hh:hh + 1, :] = cum_early[hh][0:1, :]
        if overlap:
            overlap[1](upcoming)

    def store(slot, vals):
        zs, early, late = vals
        for hh in heads:
            z_ref[slot, hh] = zs[hh]
            x_ref[slot, hh, 0] = early[hh]
            x_ref[slot, hh, 1] = late[hh]

    def load(slot):
        return ([z_ref[slot, hh] for hh in heads], [x_ref[slot, hh, 0] for hh in heads],
                [x_ref[slot, hh, 1] for hh in heads])

    def masked_block(j, overlap=None):
        q_pos, k_pos = _positions(q_start, tq, j, tq)
        mask = k_pos < q_pos
        second_stage(j, first_stage(j, mask), overlap=overlap, mask=mask)

    def masked_body(i, carry):
        masked_block(n_total - 1 - i)
        return carry

    lax.fori_loop(0, n_total - n_full - 1, masked_body, 0)
    _pipelined_blocks(n_full, first_stage, second_stage, store, load, lead=functools.partial(masked_block, n_full))
    o_ref[...] = acc_ref[...].T


def _batch_spec(a, layer=None):
    if layer is None:
        return pl.BlockSpec((None,) + a.shape[1:], lambda bi, qi, nd=a.ndim: (bi,) + (0,) * (nd - 1))
    return pl.BlockSpec((None, None) + a.shape[2:], lambda bi, qi, nd=a.ndim: (layer, bi) + (0,) * (nd - 2))


def _kv_operands(kv, layer):
    if len(kv) == 2:
        return [_batch_spec(a) for a in kv], []
    specs = [_batch_spec(a, layer) for a in kv[:2]] + [_batch_spec(a) for a in kv[2:]]
    w, past_len = kv[0].shape[2:]
    tk = past_len + KEY_BLOCK
    return specs, [pltpu.VMEM((tk, w), BF16), pltpu.VMEM((tk // KEY_BLOCK, w, KEY_BLOCK), BF16)]


def _sb_attention(q, kv, layer, q_pos0, n_valid):
    b, tq_all, w = q.shape
    tq = min(tq_all, KEY_BLOCK)
    qspec = pl.BlockSpec((None, tq, w), lambda bi, qi: (bi, qi, 0))
    kv_specs, kv_scratch = _kv_operands(kv, layer)
    return pl.pallas_call(
        functools.partial(_sb_kernel, tq=tq, q_pos0=q_pos0, n_valid=n_valid, assemble=bool(kv_scratch)),
        grid=(b, tq_all // tq),
        in_specs=[qspec] + kv_specs,
        out_specs=qspec,
        out_shape=jax.ShapeDtypeStruct((b, tq_all, w), F32),
        scratch_shapes=kv_scratch + [
            pltpu.VMEM((H_A, tq, w), BF16), pltpu.VMEM((2, H_A, KEY_BLOCK, tq), F32),
            pltpu.VMEM((2, H_A, 2, KEY_BLOCK, tq), BF16), pltpu.VMEM((w, tq), F32), pltpu.VMEM((8, tq), F32)],
        compiler_params=_params(2),
        name="sb_attention",
    )(q, *kv)


def _fox_kernel(*refs, tq, q_pos0, n_valid, assemble):
    if assemble:
        (q_ref, kt_c_ref, vt_c_ref, k_n_ref, vt_n_ref, fq_ref, fk_ref, o_ref,
         k_ref, vt_ref, qm_ref, z_ref, acc_ref, m_ref, l_ref) = refs

        @pl.when(pl.program_id(1) == 0)
        def _():
            _assemble_keys_values(kt_c_ref, vt_c_ref, k_n_ref, vt_n_ref, k_ref, vt_ref)
    else:
        q_ref, k_ref, vt_ref, fq_ref, fk_ref, o_ref, qm_ref, z_ref, acc_ref, m_ref, l_ref = refs
    q_start = q_pos0 + pl.program_id(1) * tq
    n_full, n_total = _block_range(q_start, tq, n_valid)
    hmask = _head_masks(H_C)
    q = q_ref[...]
    for hh in range(H_C):
        qm_ref[hh] = jnp.where(hmask[hh], q, jnp.zeros_like(q))
    acc_ref[...] = jnp.zeros(acc_ref.shape, F32)
    m_ref[...] = jnp.full(m_ref.shape, -jnp.inf, F32)
    l_ref[...] = jnp.zeros(l_ref.shape, F32)

    heads = range(H_C)
    rows = [slice(hh * HEAD_DIM, (hh + 1) * HEAD_DIM) for hh in heads]

    def scores(j):
        off = pl.multiple_of(j * KEY_BLOCK, KEY_BLOCK)
        kb = k_ref[pl.ds(off, KEY_BLOCK), :]
        return [_dot_nt(kb, qm_ref[hh]) for hh in heads]

    def step(j, qk, overlap=None, mask=None):
        off = pl.multiple_of(j * KEY_BLOCK, KEY_BLOCK)
        fk = fk_ref[pl.ds(off, KEY_BLOCK), :]
        vts = [vt_ref[j, rows[hh], :] for hh in heads]
        fq, m_all, l_all = fq_ref[...], m_ref[...], l_ref[...]
        accs = [acc_ref[rows[hh], :] for hh in heads]
        upcoming = overlap[0]() if overlap else None
        zs = [qk[hh] + (fq[hh:hh + 1, :] - fk[:, hh:hh + 1]) for hh in heads]
        if mask is not None:
            zs = [jnp.where(mask, z, -jnp.inf) for z in zs]
        m_new = [jnp.maximum(m_all[hh:hh + 1, :], jnp.max(zs[hh], axis=0, keepdims=True)) for hh in heads]
        m_use = [_finite_or_zero(m) for m in m_new]
        ps = [jnp.exp2(zs[hh] - m_use[hh]) for hh in heads]
        alpha = [jnp.exp2(m_all[hh:hh + 1, :] - m_use[hh]) for hh in heads]
        l_new = [alpha[hh] * l_all[hh:hh + 1, :] + jnp.sum(ps[hh], axis=0, keepdims=True) for hh in heads]
        pvs = [_dot(vts[hh], ps[hh].astype(BF16)) for hh in heads]
        for hh in heads:
            m_ref[hh:hh + 1, :] = m_new[hh]
            l_ref[hh:hh + 1, :] = l_new[hh]
            acc_ref[rows[hh], :] = accs[hh] * alpha[hh] + pvs[hh]
        if overlap:
            overlap[1](upcoming)

    def store(slot, qk):
        for hh in heads:
            z_ref[slot, hh] = qk[hh]

    def load(slot):
        return [z_ref[slot, hh] for hh in heads]

    def masked_block(j, overlap=None):
        q_pos, k_pos = _positions(q_start, tq, j, tq)
        step(j, scores(j), overlap=overlap, mask=k_pos <= q_pos)

    def masked_body(i, carry):
        masked_block(n_total - 1 - i)
        return carry

    lax.fori_loop(0, n_total - n_full - 1, masked_body, 0)
    _pipelined_blocks(n_full, scores, step, store, load, lead=functools.partial(masked_block, n_full))
    for hh in range(H_C):
        rows = slice(hh * HEAD_DIM, (hh + 1) * HEAD_DIM)
        acc_ref[rows, :] = acc_ref[rows, :] * (1.0 / l_ref[hh:hh + 1, :])
    o_ref[...] = acc_ref[...].T


def _fox_attention(q, kv, layer, fq, fk, q_pos0, n_valid):
    b, tq_all, w = q.shape
    tk = fk.shape[1]
    tq = min(tq_all, KEY_BLOCK)
    qspec = pl.BlockSpec((None, tq, w), lambda bi, qi: (bi, qi, 0))
    kv_specs, kv_scratch = _kv_operands(kv, layer)
    fqspec = pl.BlockSpec((None, F_ROWS, tq), lambda bi, qi: (bi, 0, qi))
    fkspec = pl.BlockSpec((None, tk, F_ROWS), lambda bi, qi: (bi, 0, 0))
    return pl.pallas_call(
        functools.partial(_fox_kernel, tq=tq, q_pos0=q_pos0, n_valid=n_valid, assemble=bool(kv_scratch)),
        grid=(b, tq_all // tq),
        in_specs=[qspec] + kv_specs + [fqspec, fkspec],
        out_specs=qspec,
        out_shape=jax.ShapeDtypeStruct((b, tq_all, w), F32),
        scratch_shapes=kv_scratch + [
            pltpu.VMEM((H_C, tq, w), BF16), pltpu.VMEM((2, H_C, KEY_BLOCK, tq), F32),
            pltpu.VMEM((w, tq), F32), pltpu.VMEM((8, tq), F32), pltpu.VMEM((8, tq), F32)],
        compiler_params=_params(2),
        name="fox_attention",
    )(q, *kv, fq, fk)


def _assemble_latent_keys(ckv_c_ref, krt_c_ref, kl_n_ref, kl_ref):
    past_len, n_new = ckv_c_ref.shape[0], kl_n_ref.shape[0]
    assert past_len % KEY_BLOCK == 0 and n_new <= KEY_BLOCK and kl_ref.shape[0] == past_len + KEY_BLOCK
    lane_pad = jnp.zeros((LANES - krt_c_ref.shape[0], KEY_BLOCK), F32)
    for j in range(past_len // KEY_BLOCK):
        rows = slice(j * KEY_BLOCK, (j + 1) * KEY_BLOCK)
        kl_ref[rows, :KV_LORA] = ckv_c_ref[rows, :].astype(BF16)
        kl_ref[rows, KV_LORA:] = jnp.concatenate([krt_c_ref[:, rows], lane_pad], axis=0).T.astype(BF16)
    kl_ref[past_len:past_len + n_new, :] = kl_n_ref[...]
    if n_new < KEY_BLOCK:
        kl_ref[past_len + n_new:, :] = jnp.zeros((KEY_BLOCK - n_new, kl_ref.shape[1]), BF16)


def _mla_kernel(*refs, tq, q_pos0, n_valid, assemble):
    if assemble:
        q_ref, ckv_c_ref, krt_c_ref, kl_n_ref, wuvt_ref, o_ref, kl_ref, ct_ref, z_ref, acc_ref, m_ref, l_ref = refs
    else:
        q_ref, kl_ref, wuvt_ref, o_ref, ct_ref, z_ref, acc_ref, m_ref, l_ref = refs

    @pl.when(pl.program_id(1) == 0)
    def _():
        if assemble:
            _assemble_latent_keys(ckv_c_ref, krt_c_ref, kl_n_ref, kl_ref)
        _transpose_blocks(kl_ref, ct_ref, KV_LORA)

    q_start = q_pos0 + pl.program_id(1) * tq
    n_full, n_total = _block_range(q_start, tq, n_valid)
    cols = H_B * tq
    qs = q_ref[...].reshape(cols, 2 * LANES)
    acc_ref[...] = jnp.zeros(acc_ref.shape, F32)
    m_ref[...] = jnp.full(m_ref.shape, -jnp.inf, F32)
    l_ref[...] = jnp.zeros(l_ref.shape, F32)

    group = KEY_BLOCK
    groups = [slice(g * group, (g + 1) * group) for g in range(cols // group)]

    gs = range(len(groups))

    def scores(j):
        off = pl.multiple_of(j * KEY_BLOCK, KEY_BLOCK)
        kb = kl_ref[pl.ds(off, KEY_BLOCK), :]
        return [_dot_nt(kb, qs[g]) for g in groups]

    def step(j, zs, overlap=None):
        ct = ct_ref[j]
        m_all, l_all = m_ref[...], l_ref[...]
        accs = [acc_ref[:, g] for g in groups]
        upcoming = overlap[0]() if overlap else None
        m_new = [jnp.maximum(m_all[:, groups[gi]], jnp.max(zs[gi], axis=0, keepdims=True)) for gi in gs]
        m_use = [_finite_or_zero(m) for m in m_new]
        ps = [jnp.exp2(zs[gi] - m_use[gi]) for gi in gs]
        alpha = [jnp.exp2(m_all[:, groups[gi]] - m_use[gi]) for gi in gs]
        l_new = [alpha[gi] * l_all[:, groups[gi]] + jnp.sum(ps[gi], axis=0, keepdims=True) for gi in gs]
        pvs = [_dot(ct, ps[gi].astype(BF16)) for gi in gs]
        for gi, g in enumerate(groups):
            m_ref[:, g] = m_new[gi]
            l_ref[:, g] = l_new[gi]
            acc_ref[:, g] = accs[gi] * alpha[gi] + pvs[gi]
        if overlap:
            overlap[1](upcoming)

    def store(slot, zs):
        for gi, g in enumerate(groups):
            z_ref[slot, :, g] = zs[gi]

    def load(slot):
        return [z_ref[slot, :, g] for g in groups]

    def masked_block(j, overlap=None):
        q_pos, k_pos = _positions(q_start, tq, j, group)
        mask = (k_pos // CHUNK <= q_pos // CHUNK) & (k_pos < n_valid)
        step(j, [jnp.where(mask, z, -jnp.inf) for z in scores(j)], overlap=overlap)

    def masked_body(i, carry):
        masked_block(n_total - 1 - i)
        return carry

    lax.fori_loop(0, n_total - n_full - 1, masked_body, 0)
    _pipelined_blocks(n_full, scores, step, store, load, lead=functools.partial(masked_block, n_full))

    lat = (acc_ref[...] * (1.0 / l_ref[...])).astype(BF16)
    heads = [_dot(wuvt_ref[hh], lat[:, hh * tq:(hh + 1) * tq]) for hh in range(H_B)]
    o_ref[...] = jnp.concatenate(heads, axis=0).T


def _mla_attention(qm, keys, wuvt, layer, tq_all, q_pos0, n_valid):
    b, w = qm.shape[1], qm.shape[3]
    assemble = len(keys) > 1
    tk = keys[0].shape[2] + KEY_BLOCK if assemble else keys[0].shape[1]
    tq = KEY_BLOCK if tq_all % KEY_BLOCK == 0 else LANES
    cols = H_B * tq
    if assemble:
        key_specs = [_batch_spec(keys[0], layer), _batch_spec(keys[1], layer), _batch_spec(keys[2])]
    else:
        key_specs = [_batch_spec(keys[0])]
    return pl.pallas_call(
        functools.partial(_mla_kernel, tq=tq, q_pos0=q_pos0, n_valid=n_valid, assemble=assemble),
        grid=(b, tq_all // tq),
        in_specs=[pl.BlockSpec((H_B, None, tq, w), lambda bi, qi: (0, bi, qi, 0))] + key_specs + [_layer_spec(wuvt, layer)],
        out_specs=pl.BlockSpec((None, tq, W_B), lambda bi, qi: (bi, qi, 0)),
        out_shape=jax.ShapeDtypeStruct((b, tq_all, W_B), F32),
        scratch_shapes=([pltpu.VMEM((tk, w), BF16)] if assemble else []) + [
            pltpu.VMEM((tk // KEY_BLOCK, KV_LORA, KEY_BLOCK), BF16), pltpu.VMEM((2, KEY_BLOCK, cols), F32),
            pltpu.VMEM((KV_LORA, cols), F32), pltpu.VMEM((1, cols), F32), pltpu.VMEM((1, cols), F32)],
        compiler_params=_params(2),
        name="mla_attention",
    )(qm, *keys, wuvt)


def _post_kernel(h_ref, oa_ref, ob_ref, oc_ref, p_ref, ggrp_ref, wout_ref, gmix_ref,
                 gpre_ref, gpost_ref, wgu_ref, wdown_ref, gple_pre_ref, wgate_ref, wproj_ref, gple_post_ref, o_ref,
                 *, d_ff, chunks):
    m = jnp.zeros(h_ref.shape, F32)
    c0 = 0
    for o_grp in (oa_ref, ob_ref, oc_ref):
        c1 = c0 + o_grp.shape[1]
        m = m + _dot(_rms(o_grp[...], ggrp_ref[:, c0:c1]).astype(BF16), wout_ref[c0:c1, :])
        c0 = c1
    h = h_ref[...] + _rms(m, gmix_ref[...])
    h = _ffn_rows(h, gpre_ref, gpost_ref, wgu_ref, wdown_ref, d_ff, chunks)
    gate = jax.nn.sigmoid(_dot(_rms(h, gple_pre_ref[...]).astype(BF16), wgate_ref[...]))
    e = _dot(p_ref[...].astype(BF16), wproj_ref[...]) * gate
    o_ref[...] = h + _rms(e, gple_post_ref[...])


def _post(h, oa, ob, oc, p, layer, consts):
    n, d = h.shape
    tm = _row_tile(n)
    row = lambda w: pl.BlockSpec((tm, w), lambda i: (i, 0))
    p_spec = pl.BlockSpec((None, tm, p.shape[2]), lambda i: (layer, i, 0))
    d_ff = consts[6].shape[1]
    return pl.pallas_call(
        functools.partial(_post_kernel, d_ff=d_ff, chunks=_ffn_chunks(d_ff)),
        grid=(n // tm,),
        in_specs=[row(d), row(W_A), row(W_B), row(W_C), p_spec] + [_layer_spec(c, layer) for c in consts],
        out_specs=row(d),
        out_shape=jax.ShapeDtypeStruct((n, d), F32),
        compiler_params=_params(1),
        name="post",
    )(h, oa, ob, oc, p, *consts)


def _rope_tables(pos):
    half = MLA_ROPE // 2
    inv = ROPE_THETA ** (-jnp.arange(half, dtype=F32) / half)
    ang = pos.astype(F32)[:, None] * inv[None, :]
    cos, sin = jnp.cos(ang), jnp.sin(ang)
    pad = jnp.zeros((pos.shape[0], LANES - MLA_ROPE), F32)
    return jnp.concatenate([cos, cos, pad], axis=1), jnp.concatenate([-sin, sin, pad], axis=1)


def _prep_weights(weights):
    (g_ff1_pre, g_ff1_post, w_ff1_gu, w_ff1_down, g_mix_pre, g_mix_post, w_in, b_f, g_bq, g_bkv, w_uq, w_ukv,
     g_grp, w_out, g_ff2_pre, g_ff2_post, w_ff2_gu, w_ff2_down, g_ple_pre, w_ple_gate, w_ple_proj, g_ple_post) = weights
    depth, d = w_in.shape[:2]
    half = MLA_ROPE // 2
    row = lambda g: g.reshape(depth, 1, -1).astype(F32)
    bf = lambda w: w.astype(BF16)
    c_kr = 3 * W_A + Q_LORA + KV_LORA
    c_c = c_kr + MLA_ROPE
    c_f = c_c + 3 * W_C
    w_in = bf(w_in)
    kr = w_in[:, :, c_kr:c_kr + MLA_ROPE]
    kr_sw = jnp.concatenate([kr[:, :, half:], kr[:, :, :half]], axis=2)
    zpad = jnp.zeros((depth, d, LANES - MLA_ROPE), BF16)
    win_p = jnp.concatenate([w_in[:, :, :c_kr], kr, zpad, kr_sw, zpad, w_in[:, :, c_c:c_f], w_in[:, :, c_f:],
                             jnp.zeros((depth, d, LANES - H_C), BF16)], axis=2)
    assert win_p.shape[2] == IN_COLS_P
    bf_p = jnp.pad(b_f.astype(F32), ((0, 0), (0, LANES - H_C))).reshape(depth, 1, LANES)
    wkvt = jnp.transpose(jnp.concatenate([w_in[:, :, W_A:3 * W_A], w_in[:, :, c_c + W_C:c_f]], axis=2), (0, 2, 1))

    wq4 = bf(w_uq).reshape(depth, Q_LORA, H_B, MLA_NOPE + MLA_ROPE)
    wkv4 = bf(w_ukv).reshape(depth, KV_LORA, H_B, MLA_NOPE + MLA_V)
    wq_n = jnp.transpose(wq4[..., :MLA_NOPE], (0, 2, 1, 3))
    wk_n = jnp.transpose(wkv4[..., :MLA_NOPE], (0, 2, 1, 3))
    wcomb = _wcomb(wq_n, wk_n)
    x1 = wq4[..., MLA_NOPE:MLA_NOPE + half]
    x2 = wq4[..., MLA_NOPE + half:]
    zq = jnp.zeros((depth, Q_LORA, H_B, LANES - MLA_ROPE), BF16)
    wqr = jnp.concatenate([jnp.concatenate([x1, x2, zq], axis=3).reshape(depth, Q_LORA, H_B * LANES),
                           jnp.concatenate([x2, x1, zq], axis=3).reshape(depth, Q_LORA, H_B * LANES)], axis=2)
    wuvt = jnp.transpose(wkv4[..., MLA_NOPE:], (0, 2, 3, 1))

    return dict(
        ff1=(row(g_ff1_pre), row(g_ff1_post), bf(w_ff1_gu), bf(w_ff1_down)),
        inproj=(row(g_mix_pre), win_p, bf_p, row(g_bq), row(g_bkv), wcomb, wqr, wkvt),
        wuvt=wuvt,
        post=(row(g_grp), bf(w_out), row(g_mix_post),
              row(g_ff2_pre), row(g_ff2_post), bf(w_ff2_gu), bf(w_ff2_down),
              row(g_ple_pre), bf(w_ple_gate), bf(w_ple_proj), row(g_ple_post)),
    )


def _pad_keys(a, tk_pad):
    return jnp.pad(a, ((0, 0), (0, tk_pad - a.shape[1])) + ((0, 0),) * (a.ndim - 2))


def _forget_sums(logf_rows, tk_pad):
    _, heads, tk = logf_rows.shape
    return _cumsum_rows(jnp.pad(logf_rows, ((0, 0), (0, F_ROWS - heads), (0, tk_pad - tk))))


def _layer(h, p, seq_len, q_pos0, past, layer, lp, tables, prev_states):
    n, d = h.shape
    b = n // seq_len
    h = _ffn(h, layer, *lp["ff1"])
    (ka_st, va_st, ckv_st, kr_st, kc_st, vc_st, lf_st,
     qa_b, ka_b, va_b, qm_b, kl_b, qc_b, kc_b, vc_b), feature_major = _inproj(h, seq_len, layer, prev_states,
                                                                                *lp["inproj"], *tables)
    if feature_major:
        state = (ka_st, va_st, ckv_st, kr_st, kc_st, vc_st, lf_st)
        lf_rows_new = lf_st[layer]
    else:
        state = (ka_st.reshape(b, seq_len, H_A, HEAD_DIM), va_st.reshape(b, seq_len, H_A, HEAD_DIM),
                 ckv_st.reshape(b, seq_len, KV_LORA), kr_st.reshape(b, seq_len, MLA_ROPE),
                 kc_st.reshape(b, seq_len, H_C, HEAD_DIM), vc_st.reshape(b, seq_len, H_C, HEAD_DIM),
                 lf_st.reshape(b, seq_len, H_C))
        lf_rows_new = jnp.transpose(state[6], (0, 2, 1))
    seq3 = lambda a: a.reshape(b, seq_len, a.shape[-1])
    tq_pad = -(-seq_len // LANES) * LANES
    to_blocks = lambda v: jnp.transpose(v.reshape(b, v.shape[1] // KEY_BLOCK, KEY_BLOCK, v.shape[2]), (0, 1, 3, 2))
    if past is None:
        n_valid = seq_len
        tk_pad = -(-tq_pad // KEY_BLOCK) * KEY_BLOCK
        ka_all, kl_all, kc_all = (_pad_keys(seq3(a), tk_pad) for a in (ka_b, kl_b, kc_b))
        if feature_major:
            va_all, vc_all = (v.reshape(b, seq_len // KEY_BLOCK, v.shape[1], KEY_BLOCK) for v in (va_b, vc_b))
        else:
            va_all, vc_all = (to_blocks(_pad_keys(seq3(v), tk_pad)) for v in (va_b, vc_b))
        kv_a, kv_c, keys_b = (ka_all, va_all), (kc_all, vc_all), (kl_all,)
        lf_rows = lf_rows_new
    else:
        assert not feature_major
        past_len = past[0].shape[2]
        n_valid = past_len + seq_len
        tk_pad = -(-(past_len + tq_pad) // KEY_BLOCK) * KEY_BLOCK
        if past_len % KEY_BLOCK == 0 and tk_pad == past_len + KEY_BLOCK:
            ca_k, ca_v, cb_ckv, cb_kr, cc_k, cc_v, _ = past
            wide = lambda c: jnp.transpose(c, (0, 1, 3, 4, 2)).reshape(c.shape[0], b, -1, past_len)
            new_k = lambda k: _pad_keys(seq3(k), tq_pad)
            new_vt = lambda v: jnp.transpose(_pad_keys(seq3(v), tq_pad), (0, 2, 1))
            kv_a = (wide(ca_k), wide(ca_v), new_k(ka_b), new_vt(va_b))
            kv_c = (wide(cc_k), wide(cc_v), new_k(kc_b), new_vt(vc_b))
            keys_b = (cb_ckv, jnp.transpose(cb_kr, (0, 1, 3, 2)), new_k(kl_b))
        else:
            pa_k, pa_v, pb_ckv, pb_kr, pc_k, pc_v, _ = (c[layer] for c in past)
            join = lambda c, new: _pad_keys(
                jnp.concatenate([c.reshape(b, past_len, -1).astype(BF16), seq3(new)], axis=1), tk_pad)
            kv_a = (join(pa_k, ka_b), to_blocks(join(pa_v, va_b)))
            kv_c = (join(pc_k, kc_b), to_blocks(join(pc_v, vc_b)))
            kl_past = jnp.concatenate([pb_ckv, pb_kr, jnp.zeros((b, past_len, LANES - MLA_ROPE), F32)], axis=-1)
            keys_b = (join(kl_past, kl_b),)
        lf_rows = jnp.concatenate([jnp.transpose(past[6][layer], (0, 2, 1)), lf_rows_new], axis=2)
    f_rows = _forget_sums(lf_rows, tk_pad)
    f_q = f_rows[:, :, q_pos0:q_pos0 + tq_pad]
    f_k = jnp.transpose(f_rows, (0, 2, 1))
    pad_q = lambda a: jnp.pad(a, ((0, 0),) * (a.ndim - 2) + ((0, tq_pad - seq_len), (0, 0)))

    oa = _sb_attention(pad_q(seq3(qa_b)), kv_a, layer, q_pos0, n_valid)[:, :seq_len]
    ob = _mla_attention(pad_q(qm_b.reshape(H_B, b, seq_len, 2 * LANES)), keys_b, lp["wuvt"], layer, tq_pad, q_pos0,
                        n_valid)[:, :seq_len]
    oc = _fox_attention(pad_q(seq3(qc_b)), kv_c, layer, f_q, f_k, q_pos0, n_valid)[:, :seq_len]
    flat = lambda a: a.reshape(n, a.shape[-1])
    h = _post(h, flat(oa), flat(ob), flat(oc), p, layer, lp["post"])
    return h, state, feature_major


def _trunk(x, p, q_pos0, caches, lp, depth):
    b, t, d = x.shape
    tables = _rope_tables(q_pos0 + jnp.arange(t, dtype=jnp.int32))
    h = x.reshape(b * t, d)
    p = p.reshape(depth, b * t, -1)
    states, st, stacked_mode = [], None, False
    for i in range(depth):
        h, st, stacked_mode = _layer(h, p, t, q_pos0, caches, i, lp, tables, st if stacked_mode else None)
        states.append(st)
    if stacked_mode:
        heads5 = lambda a, nh: jnp.transpose(a.reshape(depth, b, nh, HEAD_DIM, t), (0, 1, 4, 2, 3))
        swap = lambda a: jnp.transpose(a, (0, 1, 3, 2))
        ka, va, ckv, kr, kc, vc, lf = st
        stacked = [heads5(ka, H_A), heads5(va, H_A), ckv.reshape(depth, b, t, KV_LORA), swap(kr),
                   heads5(kc, H_C), heads5(vc, H_C), swap(lf)]
    else:
        stacked = [jnp.stack([s[j] for s in states]) for j in range(N_STATE)]
    return h.reshape(b, t, d), stacked


def kernel(x_prompt, x_sample, p_prompt, p_sample, cache_a_k, cache_a_v, cache_b_ckv, cache_b_krope, cache_c_k, cache_c_v, cache_c_logf, g_ff1_pre, g_ff1_post, w_ff1_gu, w_ff1_down, g_mix_pre, g_mix_post, w_in, b_f, g_bq, g_bkv, w_uq, w_ukv, g_grp, w_out, g_ff2_pre, g_ff2_post, w_ff2_gu, w_ff2_down, g_ple_pre, w_ple_gate, w_ple_proj, g_ple_post):
    weights = (g_ff1_pre, g_ff1_post, w_ff1_gu, w_ff1_down, g_mix_pre, g_mix_post, w_in, b_f,
               g_bq, g_bkv, w_uq, w_ukv, g_grp, w_out, g_ff2_pre, g_ff2_post, w_ff2_gu, w_ff2_down,
               g_ple_pre, w_ple_gate, w_ple_proj, g_ple_post)
    depth = w_in.shape[0]
    lp = _prep_weights(weights)
    y_prompt, sp = _trunk(x_prompt, p_prompt, 0, None, lp, depth)
    caches = (cache_a_k, cache_a_v, cache_b_ckv, cache_b_krope, cache_c_k, cache_c_v, cache_c_logf)
    y_sample, ss = _trunk(x_sample, p_sample, cache_a_k.shape[2], caches, lp, depth)
    return (y_prompt, y_sample, *sp, *ss)
```

```python
import functools
import math

import jax
import jax.numpy as jnp
from jax import lax
from jax.experimental import pallas as pl
from jax.experimental.pallas import tpu as pltpu

CHUNK = 64
HEAD_DIM = 64
H_A = 4
H_B = 8
H_C = 4
W_A = H_A * HEAD_DIM
MLA_NOPE = 64
MLA_ROPE = 32
MLA_V = 64
W_B = H_B * MLA_V
W_C = H_C * HEAD_DIM
Q_LORA = 256
KV_LORA = 128
ROPE_THETA = 10000.0
EPS = 1e-6
FFN_RES = 0.5
SB_SCALE = HEAD_DIM ** -0.5
MLA_SCALE = (MLA_NOPE + MLA_ROPE) ** -0.5
FOX_SCALE = HEAD_DIM ** -0.5
LOG2E = math.log2(math.e)

LANES = 128
KEY_BLOCK = 256
F_ROWS = 16
VMEM_LIMIT = 56 * 1024 * 1024

COL_A = 0
COL_CQ = COL_A + 3 * W_A
COL_CKV = COL_CQ + Q_LORA
COL_KRA = COL_CKV + KV_LORA
COL_KRB = COL_KRA + LANES
COL_C = COL_KRB + LANES
COL_F = COL_C + 3 * W_C
IN_COLS_P = COL_F + LANES

BF16 = jnp.bfloat16
F32 = jnp.float32


def _dot(a, b):
    return jnp.dot(a, b, preferred_element_type=F32)


def _dot_nt(a, b):
    return lax.dot_general(a, b, (((1,), (1,)), ((), ())), preferred_element_type=F32)


def _rms(x, g):
    ms = jnp.mean(x * x, axis=-1, keepdims=True)
    return x * lax.rsqrt(ms + EPS) * g


def _log_sigmoid(x):
    return jnp.minimum(x, 0.0) - jnp.log(1.0 + jnp.exp(-jnp.abs(x)))


def _softplus(x):
    return jnp.maximum(x, 0.0) + jnp.log(1.0 + jnp.exp2(jnp.abs(x) * (-LOG2E)))


def _split2(x):
    hi = x.astype(BF16)
    lo = (x - hi.astype(F32)).astype(BF16)
    return hi, lo


def _layer_spec(a, layer):
    idx = (layer,) + (0,) * (a.ndim - 1)
    return pl.BlockSpec((None,) + a.shape[1:], lambda *_: idx, pipeline_mode=pl.Buffered(1))


def _params(n_axes):
    return pltpu.CompilerParams(dimension_semantics=("arbitrary",) * n_axes, vmem_limit_bytes=VMEM_LIMIT)


def _row_tile(n):
    for tm in (512, 256):
        if n % tm == 0:
            return tm
    return n


def _ffn_rows(h, gpre_ref, gpost_ref, wgu_ref, wdown_ref, d_ff, chunks):
    xn = _rms(h, gpre_ref[...]).astype(BF16)
    acc = jnp.zeros(h.shape, F32)
    for c0, c1 in chunks:
        g = _dot(xn, wgu_ref[:, c0:c1])
        u = _dot(xn, wgu_ref[:, d_ff + c0:d_ff + c1])
        a = (g * jax.nn.sigmoid(g) * u).astype(BF16)
        acc = acc + _dot(a, wdown_ref[c0:c1, :])
    return h + FFN_RES * _rms(acc, gpost_ref[...])


def _ffn_chunks(d_ff):
    step = 4 * KEY_BLOCK
    return tuple((c, min(c + step, d_ff)) for c in range(0, d_ff, step))


def _ffn_kernel(h_ref, gpre_ref, gpost_ref, wgu_ref, wdown_ref, o_ref, *, d_ff, chunks):
    o_ref[...] = _ffn_rows(h_ref[...], gpre_ref, gpost_ref, wgu_ref, wdown_ref, d_ff, chunks)


def _ffn(h, layer, gpre, gpost, wgu, wdown):
    n, d = h.shape
    d_ff = wdown.shape[1]
    consts = [gpre, gpost, wgu, wdown]
    tm = _row_tile(n)
    chunks = _ffn_chunks(d_ff)
    row = pl.BlockSpec((tm, d), lambda i: (i, 0))
    return pl.pallas_call(
        functools.partial(_ffn_kernel, d_ff=d_ff, chunks=chunks),
        grid=(n // tm,),
        in_specs=[row] + [_layer_spec(c, layer) for c in consts],
        out_specs=row,
        out_shape=jax.ShapeDtypeStruct((n, d), F32),
        compiler_params=_params(1),
        name="ffn",
    )(h, gpre, gpost, wgu, wdown)


def _wcomb_kernel(wq_ref, wk_ref, o_ref):
    kl = wk_ref.shape[1]
    for hh in range(wq_ref.shape[0]):
        o_ref[:, hh * kl:(hh + 1) * kl] = _dot_nt(wq_ref[hh], wk_ref[hh]).astype(BF16)


def _wcomb(wq_n, wk_n):
    depth, nh, ql, dn = wq_n.shape
    kl = wk_n.shape[2]
    return pl.pallas_call(
        _wcomb_kernel,
        grid=(depth,),
        in_specs=[pl.BlockSpec((None, nh, ql, dn), lambda i: (i, 0, 0, 0)),
                  pl.BlockSpec((None, nh, kl, dn), lambda i: (i, 0, 0, 0))],
        out_specs=pl.BlockSpec((None, ql, nh * kl), lambda i: (i, 0, 0)),
        out_shape=jax.ShapeDtypeStruct((depth, ql, nh * kl), BF16),
        compiler_params=_params(1),
        name="wcomb",
    )(wq_n, wk_n)


def _inproj_kernel(*refs, feature_major, n_alias, fill_layers):
    (h_ref, gpre_ref, wmain_ref, wrest_ref, bf_ref, gbq_ref, gbkv_ref, wcomb_ref, wqr_ref, wkvt_ref,
     cos_ref, sin_ref) = refs[:N_INPROJ_IN]
    (ka_st, va_st, ckv_st, kr_st, kc_st, vc_st, lf_st,
     qa_b, ka_b, va_b, qm_b, kl_b, qc_b, kc_b, vc_b) = refs[N_INPROJ_IN + n_alias:]

    def put(st, val):
        if fill_layers:
            for slab in range(fill_layers):
                st[slab] = val
        else:
            st[...] = val

    xn = _rms(h_ref[...], gpre_ref[...]).astype(BF16)
    proj_main = _dot(xn, wmain_ref[...])
    proj_rest = _dot(xn, wrest_ref[...])

    def seg(c0, width):
        return proj_main[:, c0:c0 + width] if c0 < COL_KRA else proj_rest[:, c0 - COL_KRA:c0 - COL_KRA + width]

    cos = cos_ref[...]
    sin = sin_ref[...]

    qa_b[...] = (seg(COL_A, W_A) * SB_SCALE).astype(BF16)
    qc_b[...] = (seg(COL_C, W_C) * (FOX_SCALE * LOG2E)).astype(BF16)
    ka = seg(COL_A + W_A, W_A)
    kc = seg(COL_C + W_C, W_C)
    ka_b[...] = ka.astype(BF16)
    kc_b[...] = kc.astype(BF16)
    lf = _log_sigmoid(seg(COL_F, LANES) + bf_ref[...])
    ckv = _rms(seg(COL_CKV, KV_LORA), gbkv_ref[...])
    put(ckv_st, ckv)
    kr = seg(COL_KRA, LANES) * cos + seg(COL_KRB, LANES) * sin
    kl_b[:, :KV_LORA] = ckv.astype(BF16)
    kl_b[:, KV_LORA:] = kr.astype(BF16)

    if feature_major:
        kvt = _dot_nt(wkvt_ref[...], xn)
        for idx, st in enumerate((ka_st, va_st, kc_st, vc_st)):
            put(st, kvt[idx * W_A:(idx + 1) * W_A])
        for r in range(va_b.shape[0]):
            cols = slice(r * KEY_BLOCK, (r + 1) * KEY_BLOCK)
            va_b[r] = kvt[W_A:2 * W_A, cols].astype(BF16)
            vc_b[r] = kvt[3 * W_A:4 * W_A, cols].astype(BF16)
        put(kr_st, kr.T[:MLA_ROPE])
        put(lf_st, lf.T[:H_C])
    else:
        va = seg(COL_A + 2 * W_A, W_A)
        vc = seg(COL_C + 2 * W_C, W_C)
        ka_st[...] = ka
        va_st[...] = va
        kc_st[...] = kc
        vc_st[...] = vc
        va_b[...] = va.astype(BF16)
        vc_b[...] = vc.astype(BF16)
        kr_st[...] = kr[:, :MLA_ROPE]
        lf_st[...] = lf[:, :H_C]

    cqn = _rms(seg(COL_CQ, Q_LORA), gbq_ref[...]).astype(BF16)
    qlat = _dot(cqn, wcomb_ref[...])
    qr = _dot(cqn, wqr_ref[...])
    half = H_B * LANES
    for hh in range(H_B):
        sl = slice(hh * LANES, (hh + 1) * LANES)
        rope = qr[:, sl] * cos + qr[:, half + hh * LANES:half + (hh + 1) * LANES] * sin
        qm_b[hh, :, :KV_LORA] = (qlat[:, sl] * (MLA_SCALE * LOG2E)).astype(BF16)
        qm_b[hh, :, KV_LORA:] = (rope * (MLA_SCALE * LOG2E)).astype(BF16)


N_STATE = 7
N_INPROJ_IN = 12


def _inproj(h, seq_len, layer, prev_states, gpre, wmain, wrest, bf, gbq, gbkv, wcomb, wqr, wkvt, cos_t, sin_t):
    n, d = h.shape
    b = n // seq_len
    depth = wmain.shape[0]
    tm = _row_tile(n)
    row = lambda w: pl.BlockSpec((tm, w), lambda i: (i, 0))
    consts = [gpre, wmain, wrest, bf, gbq, gbkv, wcomb, wqr, wkvt]
    feature_major = seq_len % tm == 0 and tm % KEY_BLOCK == 0
    fill_all = feature_major and prev_states is None
    if feature_major:
        per_seq = seq_len // tm
        tab = pl.BlockSpec((tm, LANES), lambda i: (i % per_seq, 0))
        lead, at = (depth, 0) if fill_all else (None, layer)
        narrow = lambda w: (pl.BlockSpec((lead, None, w, tm), lambda i: (at, i // per_seq, 0, i % per_seq)),
                            (depth, b, w, seq_len), F32)
        latent = (pl.BlockSpec((lead, tm, KV_LORA), lambda i: (at, i, 0)), (depth, n, KV_LORA), F32)
        vals = lambda w: (pl.BlockSpec((tm // KEY_BLOCK, w, KEY_BLOCK), lambda i: (i, 0, 0)),
                          (n // KEY_BLOCK, w, KEY_BLOCK), BF16)
    else:
        cos_t, sin_t = (jnp.tile(t, (b, 1)) for t in (cos_t, sin_t))
        tab = row(LANES)
        narrow = lambda w: (row(w), (n, w), F32)
        latent = narrow(KV_LORA)
        vals = lambda w: (row(w), (n, w), BF16)
    outs = [
        narrow(W_A), narrow(W_A), latent, narrow(MLA_ROPE), narrow(W_C), narrow(W_C), narrow(H_C),
        (row(W_A), (n, W_A), BF16), (row(W_A), (n, W_A), BF16), vals(W_A),
        (pl.BlockSpec((H_B, tm, 2 * LANES), lambda i: (0, i, 0)), (H_B, n, 2 * LANES), BF16),
        (row(2 * LANES), (n, 2 * LANES), BF16),
        (row(W_C), (n, W_C), BF16), (row(W_C), (n, W_C), BF16), vals(W_C),
    ]
    ins = [h, *consts, cos_t, sin_t]
    in_specs = [row(d)] + [_layer_spec(c, layer) for c in consts] + [tab, tab]
    aliases = {}
    if feature_major and prev_states is not None:
        aliases = {len(ins) + s: s for s in range(N_STATE)}
        ins = ins + list(prev_states)
        in_specs = in_specs + [pl.BlockSpec(memory_space=pl.ANY)] * N_STATE
    res = pl.pallas_call(
        functools.partial(_inproj_kernel, feature_major=feature_major, n_alias=len(aliases),
                          fill_layers=depth if fill_all else 0),
        grid=(n // tm,),
        in_specs=in_specs,
        out_specs=[o[0] for o in outs],
        out_shape=[jax.ShapeDtypeStruct(o[1], o[2]) for o in outs],
        input_output_aliases=aliases,
        compiler_params=_params(1),
        name="inproj",
    )(*ins)
    return res, feature_major


def _cumsum_kernel(x_ref, o_ref, *, n_blocks):
    r = lax.broadcasted_iota(jnp.int32, (KEY_BLOCK, KEY_BLOCK), 0)
    c = lax.broadcasted_iota(jnp.int32, (KEY_BLOCK, KEY_BLOCK), 1)
    upper = jnp.where(r <= c, 1.0, 0.0).astype(BF16)
    carry = jnp.zeros((x_ref.shape[0], 1), F32)
    for j in range(n_blocks):
        sl = slice(j * KEY_BLOCK, (j + 1) * KEY_BLOCK)
        x = x_ref[:, sl]
        hi = x.astype(BF16)
        mid, lo = _split2(x - hi.astype(F32))
        f = _dot(hi, upper) + _dot(mid, upper) + _dot(lo, upper) + carry
        o_ref[:, sl] = f * LOG2E
        carry = f[:, KEY_BLOCK - 1:KEY_BLOCK]


def _cumsum_rows(x):
    b, r, tp = x.shape
    spec = pl.BlockSpec((None, r, tp), lambda i: (i, 0, 0))
    return pl.pallas_call(
        functools.partial(_cumsum_kernel, n_blocks=tp // KEY_BLOCK),
        grid=(b,),
        in_specs=[spec],
        out_specs=spec,
        out_shape=jax.ShapeDtypeStruct(x.shape, F32),
        compiler_params=_params(1),
        name="cumsum_logf",
    )(x)


def _block_range(q_start, tq, n_valid):
    n_full = q_start // KEY_BLOCK
    last = jnp.minimum(((q_start + tq + CHUNK - 1) // CHUNK) * CHUNK, n_valid)
    n_total = (last + KEY_BLOCK - 1) // KEY_BLOCK
    return n_full, n_total


def _positions(q_start, tq, j, cols):
    k_pos = j * KEY_BLOCK + lax.broadcasted_iota(jnp.int32, (KEY_BLOCK, cols), 0)
    lane = lax.broadcasted_iota(jnp.int32, (KEY_BLOCK, cols), 1)
    q_pos = q_start + (lane if cols == tq else lane % tq)
    return q_pos, k_pos


def _transpose_blocks(src_ref, dst_ref, width):
    for j in range(dst_ref.shape[0]):
        blk = src_ref[j * KEY_BLOCK:(j + 1) * KEY_BLOCK, :width].astype(F32)
        dst_ref[j] = blk.T.astype(BF16)


def _assemble_keys_values(kt_c_ref, vt_c_ref, k_n_ref, vt_n_ref, k_ref, vt_ref):
    past_len = kt_c_ref.shape[1]
    n_past, n_new = past_len // KEY_BLOCK, k_n_ref.shape[0]
    assert past_len % KEY_BLOCK == 0 and n_new <= KEY_BLOCK and vt_ref.shape[0] == n_past + 1
    for j in range(n_past):
        cols = slice(j * KEY_BLOCK, (j + 1) * KEY_BLOCK)
        k_ref[cols, :] = kt_c_ref[:, cols].T.astype(BF16)
        vt_ref[j] = vt_c_ref[:, cols].astype(BF16)
    k_ref[past_len:past_len + n_new, :] = k_n_ref[...]
    vt_ref[n_past, :, :n_new] = vt_n_ref[...]
    if n_new < KEY_BLOCK:
        k_ref[past_len + n_new:, :] = jnp.zeros((KEY_BLOCK - n_new, k_ref.shape[1]), BF16)
        vt_ref[n_past, :, n_new:] = jnp.zeros((vt_ref.shape[1], KEY_BLOCK - n_new), BF16)


def _pipelined_blocks(n_full, produce, consume, store, load, lead):
    f = lambda i: n_full - 1 - i
    n_pairs = jnp.maximum(n_full - 1, 0) // 2
    rest = n_full - 2 * n_pairs

    def both(i, slot):
        consume(f(i), load(slot), overlap=(lambda: produce(f(i + 1)), lambda vals: store(1 - slot, vals)))

    @pl.when(n_full > 0)
    def _():
        lead((lambda: produce(f(0)), lambda vals: store(0, vals)))

    @pl.when(n_full <= 0)
    def _():
        lead(None)

    def pair_body(t, carry):
        both(2 * t, 0)
        both(2 * t + 1, 1)
        return carry

    lax.fori_loop(0, n_pairs, pair_body, 0)

    @pl.when((n_full > 0) & (rest == 1))
    def _():
        consume(f(2 * n_pairs), load(0))

    @pl.when((n_full > 0) & (rest == 2))
    def _():
        both(2 * n_pairs, 0)
        consume(f(2 * n_pairs + 1), load(1))


def _finite_or_zero(m):
    return jnp.where(m == -jnp.inf, 0.0, m)


def _head_masks(n_heads):
    lane = lax.broadcasted_iota(jnp.int32, (1, n_heads * HEAD_DIM), 1)
    return [(lane >= hh * HEAD_DIM) & (lane < (hh + 1) * HEAD_DIM) for hh in range(n_heads)]


def _sb_kernel(*refs, tq, q_pos0, n_valid, assemble):
    if assemble:
        q_ref, kt_c_ref, vt_c_ref, k_n_ref, vt_n_ref, o_ref, k_ref, vt_ref, qm_ref, z_ref, x_ref, acc_ref, c_ref = refs

        @pl.when(pl.program_id(1) == 0)
        def _():
            _assemble_keys_values(kt_c_ref, vt_c_ref, k_n_ref, vt_n_ref, k_ref, vt_ref)
    else:
        q_ref, k_ref, vt_ref, o_ref, qm_ref, z_ref, x_ref, acc_ref, c_ref = refs
    q_start = q_pos0 + pl.program_id(1) * tq
    n_full, n_total = _block_range(q_start, tq, n_valid)
    hmask = _head_masks(H_A)
    q = q_ref[...]
    for hh in range(H_A):
        qm_ref[hh] = jnp.where(hmask[hh], q, jnp.zeros_like(q))
    acc_ref[...] = jnp.zeros(acc_ref.shape, F32)
    c_ref[...] = jnp.zeros(c_ref.shape, F32)
    half = KEY_BLOCK // 2
    r = lax.broadcasted_iota(jnp.int32, (half, KEY_BLOCK), 0)
    c = lax.broadcasted_iota(jnp.int32, (half, KEY_BLOCK), 1)
    neg_upper2 = jnp.where(c % half >= r, -1.0, 0.0).astype(BF16)
    heads = range(H_A)
    rows = [slice(hh * HEAD_DIM, (hh + 1) * HEAD_DIM) for hh in heads]

    def parts(x):
        hi, lo = _split2(x)
        return jnp.concatenate([hi, lo], axis=0)

    def first_stage(j, mask=None):
        off = pl.multiple_of(j * KEY_BLOCK, KEY_BLOCK)
        kb = k_ref[pl.ds(off, KEY_BLOCK), :]
        zs = [_dot_nt(kb, qm_ref[hh]) for hh in heads]
        sps = [_softplus(z) for z in zs]
        if mask is not None:
            sps = [jnp.where(mask, sp, 0.0) for sp in sps]
        return zs, [parts(sp[:half]) for sp in sps], [parts(sp[half:]) for sp in sps]

    def second_stage(j, vals, overlap=None, mask=None):
        zs, early, late = vals
        vts = [vt_ref[j, rows[hh], :] for hh in heads]
        carry = c_ref[...]
        upcoming = overlap[0]() if overlap else None
        cum_late = [_dot(neg_upper2, late[hh]) + carry[hh:hh + 1, :] for hh in heads]
        cum_early = [_dot(neg_upper2, early[hh]) + cum_late[hh][0:1, :] for hh in heads]
        ws = [jnp.exp(zs[hh] + jnp.concatenate([cum_early[hh], cum_late[hh]], axis=0)) for hh in heads]
        if mask is not None:
            ws = [jnp.where(mask, w, 0.0) for w in ws]
        pvs = [_dot(vts[hh], ws[hh].astype(BF16)) for hh in heads]
        for hh in heads:
            acc_ref[rows[hh], :] += pvs[hh]
            c_ref[hh:hh + 1, :] = cum_early[hh][0:1, :]
        if overlap:
            overlap[1](upcoming)

    def store(slot, vals):
        zs, early, late = vals
        for hh in heads:
            z_ref[slot, hh] = zs[hh]
            x_ref[slot, hh, 0] = early[hh]
            x_ref[slot, hh, 1] = late[hh]

    def load(slot):
        return ([z_ref[slot, hh] for hh in heads], [x_ref[slot, hh, 0] for hh in heads],
                [x_ref[slot, hh, 1] for hh in heads])

    def masked_block(j, overlap=None):
        q_pos, k_pos = _positions(q_start, tq, j, tq)
        mask = k_pos < q_pos
        second_stage(j, first_stage(j, mask), overlap=overlap, mask=mask)

    def masked_body(i, carry):
        masked_block(n_total - 1 - i)
        return carry

    lax.fori_loop(0, n_total - n_full - 1, masked_body, 0)
    _pipelined_blocks(n_full, first_stage, second_stage, store, load, lead=functools.partial(masked_block, n_full))
    o_ref[...] = acc_ref[...].T


def _batch_spec(a, layer=None):
    if layer is None:
        return pl.BlockSpec((None,) + a.shape[1:], lambda bi, qi, nd=a.ndim: (bi,) + (0,) * (nd - 1))
    return pl.BlockSpec((None, None) + a.shape[2:], lambda bi, qi, nd=a.ndim: (layer, bi) + (0,) * (nd - 2))


def _kv_operands(kv, layer):
    if len(kv) == 2:
        return [_batch_spec(a) for a in kv], []
    specs = [_batch_spec(a, layer) for a in kv[:2]] + [_batch_spec(a) for a in kv[2:]]
    w, past_len = kv[0].shape[2:]
    tk = past_len + KEY_BLOCK
    return specs, [pltpu.VMEM((tk, w), BF16), pltpu.VMEM((tk // KEY_BLOCK, w, KEY_BLOCK), BF16)]


def _sb_attention(q, kv, layer, q_pos0, n_valid):
    b, tq_all, w = q.shape
    tq = min(tq_all, KEY_BLOCK)
    qspec = pl.BlockSpec((None, tq, w), lambda bi, qi: (bi, qi, 0))
    kv_specs, kv_scratch = _kv_operands(kv, layer)
    return pl.pallas_call(
        functools.partial(_sb_kernel, tq=tq, q_pos0=q_pos0, n_valid=n_valid, assemble=bool(kv_scratch)),
        grid=(b, tq_all // tq),
        in_specs=[qspec] + kv_specs,
        out_specs=qspec,
        out_shape=jax.ShapeDtypeStruct((b, tq_all, w), F32),
        scratch_shapes=kv_scratch + [
            pltpu.VMEM((H_A, tq, w), BF16), pltpu.VMEM((2, H_A, KEY_BLOCK, tq), F32),
            pltpu.VMEM((2, H_A, 2, KEY_BLOCK, tq), BF16), pltpu.VMEM((w, tq), F32), pltpu.VMEM((8, tq), F32)],
        compiler_params=_params(2),
        name="sb_attention",
    )(q, *kv)


def _fox_kernel(*refs, tq, q_pos0, n_valid, assemble):
    if assemble:
        (q_ref, kt_c_ref, vt_c_ref, k_n_ref, vt_n_ref, fq_ref, fk_ref, o_ref,
         k_ref, vt_ref, qm_ref, z_ref, acc_ref, m_ref, l_ref) = refs

        @pl.when(pl.program_id(1) == 0)
        def _():
            _assemble_keys_values(kt_c_ref, vt_c_ref, k_n_ref, vt_n_ref, k_ref, vt_ref)
    else:
        q_ref, k_ref, vt_ref, fq_ref, fk_ref, o_ref, qm_ref, z_ref, acc_ref, m_ref, l_ref = refs
    q_start = q_pos0 + pl.program_id(1) * tq
    n_full, n_total = _block_range(q_start, tq, n_valid)
    hmask = _head_masks(H_C)
    q = q_ref[...]
    for hh in range(H_C):
        qm_ref[hh] = jnp.where(hmask[hh], q, jnp.zeros_like(q))
    acc_ref[...] = jnp.zeros(acc_ref.shape, F32)
    m_ref[...] = jnp.full(m_ref.shape, -jnp.inf, F32)
    l_ref[...] = jnp.zeros(l_ref.shape, F32)

    heads = range(H_C)
    rows = [slice(hh * HEAD_DIM, (hh + 1) * HEAD_DIM) for hh in heads]

    def scores(j):
        off = pl.multiple_of(j * KEY_BLOCK, KEY_BLOCK)
        kb = k_ref[pl.ds(off, KEY_BLOCK), :]
        return [_dot_nt(kb, qm_ref[hh]) for hh in heads]

    def step(j, qk, overlap=None, mask=None):
        off = pl.multiple_of(j * KEY_BLOCK, KEY_BLOCK)
        fk = fk_ref[pl.ds(off, KEY_BLOCK), :]
        vts = [vt_ref[j, rows[hh], :] for hh in heads]
        fq, m_all, l_all = fq_ref[...], m_ref[...], l_ref[...]
        accs = [acc_ref[rows[hh], :] for hh in heads]
        upcoming = overlap[0]() if overlap else None
        zs = [qk[hh] + (fq[hh:hh + 1, :] - fk[:, hh:hh + 1]) for hh in heads]
        if mask is not None:
            zs = [jnp.where(mask, z, -jnp.inf) for z in zs]
        m_new = [jnp.maximum(m_all[hh:hh + 1, :], jnp.max(zs[hh], axis=0, keepdims=True)) for hh in heads]
        m_use = [_finite_or_zero(m) for m in m_new]
        ps = [jnp.exp2(zs[hh] - m_use[hh]) for hh in heads]
        alpha = [jnp.exp2(m_all[hh:hh + 1, :] - m_use[hh]) for hh in heads]
        l_new = [alpha[hh] * l_all[hh:hh + 1, :] + jnp.sum(ps[hh], axis=0, keepdims=True) for hh in heads]
        pvs = [_dot(vts[hh], ps[hh].astype(BF16)) for hh in heads]
        for hh in heads:
            m_ref[hh:hh + 1, :] = m_new[hh]
            l_ref[hh:hh + 1, :] = l_new[hh]
            acc_ref[rows[hh], :] = accs[hh] * alpha[hh] + pvs[hh]
        if overlap:
            overlap[1](upcoming)

    def store(slot, qk):
        for hh in heads:
            z_ref[slot, hh] = qk[hh]

    def load(slot):
        return [z_ref[slot, hh] for hh in heads]

    def masked_block(j, overlap=None):
        q_pos, k_pos = _positions(q_start, tq, j, tq)
        step(j, scores(j), overlap=overlap, mask=k_pos <= q_pos)

    def masked_body(i, carry):
        masked_block(n_total - 1 - i)
        return carry

    lax.fori_loop(0, n_total - n_full - 1, masked_body, 0)
    _pipelined_blocks(n_full, scores, step, store, load, lead=functools.partial(masked_block, n_full))
    for hh in range(H_C):
        rows = slice(hh * HEAD_DIM, (hh + 1) * HEAD_DIM)
        acc_ref[rows, :] = acc_ref[rows, :] * (1.0 / l_ref[hh:hh + 1, :])
    o_ref[...] = acc_ref[...].T


def _fox_attention(q, kv, layer, fq, fk, q_pos0, n_valid):
    b, tq_all, w = q.shape
    tk = fk.shape[1]
    tq = min(tq_all, KEY_BLOCK)
    qspec = pl.BlockSpec((None, tq, w), lambda bi, qi: (bi, qi, 0))
    kv_specs, kv_scratch = _kv_operands(kv, layer)
    fqspec = pl.BlockSpec((None, F_ROWS, tq), lambda bi, qi: (bi, 0, qi))
    fkspec = pl.BlockSpec((None, tk, F_ROWS), lambda bi, qi: (bi, 0, 0))
    return pl.pallas_call(
        functools.partial(_fox_kernel, tq=tq, q_pos0=q_pos0, n_valid=n_valid, assemble=bool(kv_scratch)),
        grid=(b, tq_all // tq),
        in_specs=[qspec] + kv_specs + [fqspec, fkspec],
        out_specs=qspec,
        out_shape=jax.ShapeDtypeStruct((b, tq_all, w), F32),
        scratch_shapes=kv_scratch + [
            pltpu.VMEM((H_C, tq, w), BF16), pltpu.VMEM((2, H_C, KEY_BLOCK, tq), F32),
            pltpu.VMEM((w, tq), F32), pltpu.VMEM((8, tq), F32), pltpu.VMEM((8, tq), F32)],
        compiler_params=_params(2),
        name="fox_attention",
    )(q, *kv, fq, fk)


def _assemble_latent_keys(ckv_c_ref, krt_c_ref, kl_n_ref, kl_ref):
    past_len, n_new = ckv_c_ref.shape[0], kl_n_ref.shape[0]
    assert past_len % KEY_BLOCK == 0 and n_new <= KEY_BLOCK and kl_ref.shape[0] == past_len + KEY_BLOCK
    lane_pad = jnp.zeros((LANES - krt_c_ref.shape[0], KEY_BLOCK), F32)
    for j in range(past_len // KEY_BLOCK):
        rows = slice(j * KEY_BLOCK, (j + 1) * KEY_BLOCK)
        kl_ref[rows, :KV_LORA] = ckv_c_ref[rows, :].astype(BF16)
        kl_ref[rows, KV_LORA:] = jnp.concatenate([krt_c_ref[:, rows], lane_pad], axis=0).T.astype(BF16)
    kl_ref[past_len:past_len + n_new, :] = kl_n_ref[...]
    if n_new < KEY_BLOCK:
        kl_ref[past_len + n_new:, :] = jnp.zeros((KEY_BLOCK - n_new, kl_ref.shape[1]), BF16)


def _mla_kernel(*refs, tq, q_pos0, n_valid, assemble):
    if assemble:
        q_ref, ckv_c_ref, krt_c_ref, kl_n_ref, wuvt_ref, o_ref, kl_ref, ct_ref, z_ref, acc_ref, m_ref, l_ref = refs
    else:
        q_ref, kl_ref, wuvt_ref, o_ref, ct_ref, z_ref, acc_ref, m_ref, l_ref = refs

    @pl.when(pl.program_id(1) == 0)
    def _():
        if assemble:
            _assemble_latent_keys(ckv_c_ref, krt_c_ref, kl_n_ref, kl_ref)
        _transpose_blocks(kl_ref, ct_ref, KV_LORA)

    q_start = q_pos0 + pl.program_id(1) * tq
    n_full, n_total = _block_range(q_start, tq, n_valid)
    cols = H_B * tq
    qs = q_ref[...].reshape(cols, 2 * LANES)
    acc_ref[...] = jnp.zeros(acc_ref.shape, F32)
    m_ref[...] = jnp.full(m_ref.shape, -jnp.inf, F32)
    l_ref[...] = jnp.zeros(l_ref.shape, F32)

    group = KEY_BLOCK
    groups = [slice(g * group, (g + 1) * group) for g in range(cols // group)]

    gs = range(len(groups))

    def scores(j):
        off = pl.multiple_of(j * KEY_BLOCK, KEY_BLOCK)
        kb = kl_ref[pl.ds(off, KEY_BLOCK), :]
        return [_dot_nt(kb, qs[g]) for g in groups]

    def step(j, zs, overlap=None):
        ct = ct_ref[j]
        m_all, l_all = m_ref[...], l_ref[...]
        accs = [acc_ref[:, g] for g in groups]
        upcoming = overlap[0]() if overlap else None
        m_new = [jnp.maximum(m_all[:, groups[gi]], jnp.max(zs[gi], axis=0, keepdims=True)) for gi in gs]
        m_use = [_finite_or_zero(m) for m in m_new]
        ps = [jnp.exp2(zs[gi] - m_use[gi]) for gi in gs]
        alpha = [jnp.exp2(m_all[:, groups[gi]] - m_use[gi]) for gi in gs]
        l_new = [alpha[gi] * l_all[:, groups[gi]] + jnp.sum(ps[gi], axis=0, keepdims=True) for gi in gs]
        pvs = [_dot(ct, ps[gi].astype(BF16)) for gi in gs]
        for gi, g in enumerate(groups):
            m_ref[:, g] = m_new[gi]
            l_ref[:, g] = l_new[gi]
            acc_ref[:, g] = accs[gi] * alpha[gi] + pvs[gi]
        if overlap:
            overlap[1](upcoming)

    def store(slot, zs):
        for gi, g in enumerate(groups):
            z_ref[slot, :, g] = zs[gi]

    def load(slot):
        return [z_ref[slot, :, g] for g in groups]

    def masked_block(j, overlap=None):
        q_pos, k_pos = _positions(q_start, tq, j, group)
        mask = (k_pos // CHUNK <= q_pos // CHUNK) & (k_pos < n_valid)
        step(j, [jnp.where(mask, z, -jnp.inf) for z in scores(j)], overlap=overlap)

    def masked_body(i, carry):
        masked_block(n_total - 1 - i)
        return carry

    lax.fori_loop(0, n_total - n_full - 1, masked_body, 0)
    _pipelined_blocks(n_full, scores, step, store, load, lead=functools.partial(masked_block, n_full))

    lat = (acc_ref[...] * (1.0 / l_ref[...])).astype(BF16)
    heads = [_dot(wuvt_ref[hh], lat[:, hh * tq:(hh + 1) * tq]) for hh in range(H_B)]
    o_ref[...] = jnp.concatenate(heads, axis=0).T


def _mla_attention(qm, keys, wuvt, layer, tq_all, q_pos0, n_valid):
    b, w = qm.shape[1], qm.shape[3]
    assemble = len(keys) > 1
    tk = keys[0].shape[2] + KEY_BLOCK if assemble else keys[0].shape[1]
    tq = KEY_BLOCK if tq_all % KEY_BLOCK == 0 else LANES
    cols = H_B * tq
    if assemble:
        key_specs = [_batch_spec(keys[0], layer), _batch_spec(keys[1], layer), _batch_spec(keys[2])]
    else:
        key_specs = [_batch_spec(keys[0])]
    return pl.pallas_call(
        functools.partial(_mla_kernel, tq=tq, q_pos0=q_pos0, n_valid=n_valid, assemble=assemble),
        grid=(b, tq_all // tq),
        in_specs=[pl.BlockSpec((H_B, None, tq, w), lambda bi, qi: (0, bi, qi, 0))] + key_specs + [_layer_spec(wuvt, layer)],
        out_specs=pl.BlockSpec((None, tq, W_B), lambda bi, qi: (bi, qi, 0)),
        out_shape=jax.ShapeDtypeStruct((b, tq_all, W_B), F32),
        scratch_shapes=([pltpu.VMEM((tk, w), BF16)] if assemble else []) + [
            pltpu.VMEM((tk // KEY_BLOCK, KV_LORA, KEY_BLOCK), BF16), pltpu.VMEM((2, KEY_BLOCK, cols), F32),
            pltpu.VMEM((KV_LORA, cols), F32), pltpu.VMEM((1, cols), F32), pltpu.VMEM((1, cols), F32)],
        compiler_params=_params(2),
        name="mla_attention",
    )(qm, *keys, wuvt)


def _post_kernel(h_ref, oa_ref, ob_ref, oc_ref, p_ref, ggrp_ref, wout_ref, gmix_ref,
                 gpre_ref, gpost_ref, wgu_ref, wdown_ref, gple_pre_ref, wgate_ref, wproj_ref, gple_post_ref, o_ref,
                 *, d_ff, chunks):
    m = jnp.zeros(h_ref.shape, F32)
    c0 = 0
    for o_grp in (oa_ref, ob_ref, oc_ref):
        c1 = c0 + o_grp.shape[1]
        m = m + _dot(_rms(o_grp[...], ggrp_ref[:, c0:c1]).astype(BF16), wout_ref[c0:c1, :])
        c0 = c1
    h = h_ref[...] + _rms(m, gmix_ref[...])
    h = _ffn_rows(h, gpre_ref, gpost_ref, wgu_ref, wdown_ref, d_ff, chunks)
    gate = jax.nn.sigmoid(_dot(_rms(h, gple_pre_ref[...]).astype(BF16), wgate_ref[...]))
    e = _dot(p_ref[...].astype(BF16), wproj_ref[...]) * gate
    o_ref[...] = h + _rms(e, gple_post_ref[...])


def _post(h, oa, ob, oc, p, layer, consts):
    n, d = h.shape
    tm = _row_tile(n)
    row = lambda w: pl.BlockSpec((tm, w), lambda i: (i, 0))
    p_spec = pl.BlockSpec((None, tm, p.shape[2]), lambda i: (layer, i, 0))
    d_ff = consts[6].shape[1]
    return pl.pallas_call(
        functools.partial(_post_kernel, d_ff=d_ff, chunks=_ffn_chunks(d_ff)),
        grid=(n // tm,),
        in_specs=[row(d), row(W_A), row(W_B), row(W_C), p_spec] + [_layer_spec(c, layer) for c in consts],
        out_specs=row(d),
        out_shape=jax.ShapeDtypeStruct((n, d), F32),
        compiler_params=_params(1),
        name="post",
    )(h, oa, ob, oc, p, *consts)


def _rope_tables(pos):
    half = MLA_ROPE // 2
    inv = ROPE_THETA ** (-jnp.arange(half, dtype=F32) / half)
    ang = pos.astype(F32)[:, None] * inv[None, :]
    cos, sin = jnp.cos(ang), jnp.sin(ang)
    pad = jnp.zeros((pos.shape[0], LANES - MLA_ROPE), F32)
    return jnp.concatenate([cos, cos, pad], axis=1), jnp.concatenate([-sin, sin, pad], axis=1)


def _prep_weights(weights):
    (g_ff1_pre, g_ff1_post, w_ff1_gu, w_ff1_down, g_mix_pre, g_mix_post, w_in, b_f, g_bq, g_bkv, w_uq, w_ukv,
     g_grp, w_out, g_ff2_pre, g_ff2_post, w_ff2_gu, w_ff2_down, g_ple_pre, w_ple_gate, w_ple_proj, g_ple_post) = weights
    depth, d = w_in.shape[:2]
    half = MLA_ROPE // 2
    row = lambda g: g.reshape(depth, 1, -1).astype(F32)
    bf = lambda w: w.astype(BF16)
    c_kr = 3 * W_A + Q_LORA + KV_LORA
    c_c = c_kr + MLA_ROPE
    c_f = c_c + 3 * W_C
    w_in = bf(w_in)
    kr = w_in[:, :, c_kr:c_kr + MLA_ROPE]
    kr_sw = jnp.concatenate([kr[:, :, half:], kr[:, :, :half]], axis=2)
    zpad = jnp.zeros((depth, d, LANES - MLA_ROPE), BF16)
    w_main = w_in[:, :, :c_kr]
    w_rest = jnp.concatenate([kr, zpad, kr_sw, zpad, w_in[:, :, c_c:c_f], w_in[:, :, c_f:],
                              jnp.zeros((depth, d, LANES - H_C), BF16)], axis=2)
    assert c_kr == COL_KRA and c_kr + w_rest.shape[2] == IN_COLS_P
    bf_p = jnp.pad(b_f.astype(F32), ((0, 0), (0, LANES - H_C))).reshape(depth, 1, LANES)
    wkvt = jnp.transpose(jnp.concatenate([w_in[:, :, W_A:3 * W_A], w_in[:, :, c_c + W_C:c_f]], axis=2), (0, 2, 1))

    wq4 = bf(w_uq).reshape(depth, Q_LORA, H_B, MLA_NOPE + MLA_ROPE)
    wkv4 = bf(w_ukv).reshape(depth, KV_LORA, H_B, MLA_NOPE + MLA_V)
    wq_n = jnp.transpose(wq4[..., :MLA_NOPE], (0, 2, 1, 3))
    wk_n = jnp.transpose(wkv4[..., :MLA_NOPE], (0, 2, 1, 3))
    wcomb = _wcomb(wq_n, wk_n)
    x1 = wq4[..., MLA_NOPE:MLA_NOPE + half]
    x2 = wq4[..., MLA_NOPE + half:]
    zq = jnp.zeros((depth, Q_LORA, H_B, LANES - MLA_ROPE), BF16)
    wqr = jnp.concatenate([jnp.concatenate([x1, x2, zq], axis=3).reshape(depth, Q_LORA, H_B * LANES),
                           jnp.concatenate([x2, x1, zq], axis=3).reshape(depth, Q_LORA, H_B * LANES)], axis=2)
    wuvt = jnp.transpose(wkv4[..., MLA_NOPE:], (0, 2, 3, 1))

    return dict(
        ff1=(row(g_ff1_pre), row(g_ff1_post), bf(w_ff1_gu), bf(w_ff1_down)),
        inproj=(row(g_mix_pre), w_main, w_rest, bf_p, row(g_bq), row(g_bkv), wcomb, wqr, wkvt),
        wuvt=wuvt,
        post=(row(g_grp), bf(w_out), row(g_mix_post),
              row(g_ff2_pre), row(g_ff2_post), bf(w_ff2_gu), bf(w_ff2_down),
              row(g_ple_pre), bf(w_ple_gate), bf(w_ple_proj), row(g_ple_post)),
    )


def _pad_keys(a, tk_pad):
    return jnp.pad(a, ((0, 0), (0, tk_pad - a.shape[1])) + ((0, 0),) * (a.ndim - 2))


def _forget_sums(logf_rows, tk_pad):
    _, heads, tk = logf_rows.shape
    return _cumsum_rows(jnp.pad(logf_rows, ((0, 0), (0, F_ROWS - heads), (0, tk_pad - tk))))


def _layer(h, p, seq_len, q_pos0, past, layer, lp, tables, prev_states):
    n, d = h.shape
    b = n // seq_len
    h = _ffn(h, layer, *lp["ff1"])
    (ka_st, va_st, ckv_st, kr_st, kc_st, vc_st, lf_st,
     qa_b, ka_b, va_b, qm_b, kl_b, qc_b, kc_b, vc_b), feature_major = _inproj(h, seq_len, layer, prev_states,
                                                                                *lp["inproj"], *tables)
    if feature_major:
        state = (ka_st, va_st, ckv_st, kr_st, kc_st, vc_st, lf_st)
        lf_rows_new = lf_st[layer]
    else:
        state = (ka_st.reshape(b, seq_len, H_A, HEAD_DIM), va_st.reshape(b, seq_len, H_A, HEAD_DIM),
                 ckv_st.reshape(b, seq_len, KV_LORA), kr_st.reshape(b, seq_len, MLA_ROPE),
                 kc_st.reshape(b, seq_len, H_C, HEAD_DIM), vc_st.reshape(b, seq_len, H_C, HEAD_DIM),
                 lf_st.reshape(b, seq_len, H_C))
        lf_rows_new = jnp.transpose(state[6], (0, 2, 1))
    seq3 = lambda a: a.reshape(b, seq_len, a.shape[-1])
    tq_pad = -(-seq_len // LANES) * LANES
    to_blocks = lambda v: jnp.transpose(v.reshape(b, v.shape[1] // KEY_BLOCK, KEY_BLOCK, v.shape[2]), (0, 1, 3, 2))
    if past is None:
        n_valid = seq_len
        tk_pad = -(-tq_pad // KEY_BLOCK) * KEY_BLOCK
        ka_all, kl_all, kc_all = (_pad_keys(seq3(a), tk_pad) for a in (ka_b, kl_b, kc_b))
        if feature_major:
            va_all, vc_all = (v.reshape(b, seq_len // KEY_BLOCK, v.shape[1], KEY_BLOCK) for v in (va_b, vc_b))
        else:
            va_all, vc_all = (to_blocks(_pad_keys(seq3(v), tk_pad)) for v in (va_b, vc_b))
        kv_a, kv_c, keys_b = (ka_all, va_all), (kc_all, vc_all), (kl_all,)
        lf_rows = lf_rows_new
    else:
        assert not feature_major
        past_len = past[0].shape[2]
        n_valid = past_len + seq_len
        tk_pad = -(-(past_len + tq_pad) // KEY_BLOCK) * KEY_BLOCK
        if past_len % KEY_BLOCK == 0 and tk_pad == past_len + KEY_BLOCK:
            ca_k, ca_v, cb_ckv, cb_kr, cc_k, cc_v, _ = past
            wide = lambda c: jnp.transpose(c, (0, 1, 3, 4, 2)).reshape(c.shape[0], b, -1, past_len)
            new_k = lambda k: _pad_keys(seq3(k), tq_pad)
            new_vt = lambda v: jnp.transpose(_pad_keys(seq3(v), tq_pad), (0, 2, 1))
            kv_a = (wide(ca_k), wide(ca_v), new_k(ka_b), new_vt(va_b))
            kv_c = (wide(cc_k), wide(cc_v), new_k(kc_b), new_vt(vc_b))
            keys_b = (cb_ckv, jnp.transpose(cb_kr, (0, 1, 3, 2)), new_k(kl_b))
        else:
            pa_k, pa_v, pb_ckv, pb_kr, pc_k, pc_v, _ = (c[layer] for c in past)
            join = lambda c, new: _pad_keys(
                jnp.concatenate([c.reshape(b, past_len, -1).astype(BF16), seq3(new)], axis=1), tk_pad)
            kv_a = (join(pa_k, ka_b), to_blocks(join(pa_v, va_b)))
            kv_c = (join(pc_k, kc_b), to_blocks(join(pc_v, vc_b)))
            kl_past = jnp.concatenate([pb_ckv, pb_kr, jnp.zeros((b, past_len, LANES - MLA_ROPE), F32)], axis=-1)
            keys_b = (join(kl_past, kl_b),)
        lf_rows = jnp.concatenate([jnp.transpose(past[6][layer], (0, 2, 1)), lf_rows_new], axis=2)
    f_rows = _forget_sums(lf_rows, tk_pad)
    f_q = f_rows[:, :, q_pos0:q_pos0 + tq_pad]
    f_k = jnp.transpose(f_rows, (0, 2, 1))
    pad_q = lambda a: jnp.pad(a, ((0, 0),) * (a.ndim - 2) + ((0, tq_pad - seq_len), (0, 0)))

    oa = _sb_attention(pad_q(seq3(qa_b)), kv_a, layer, q_pos0, n_valid)[:, :seq_len]
    ob = _mla_attention(pad_q(qm_b.reshape(H_B, b, seq_len, 2 * LANES)), keys_b, lp["wuvt"], layer, tq_pad, q_pos0,
                        n_valid)[:, :seq_len]
    oc = _fox_attention(pad_q(seq3(qc_b)), kv_c, layer, f_q, f_k, q_pos0, n_valid)[:, :seq_len]
    flat = lambda a: a.reshape(n, a.shape[-1])
    h = _post(h, flat(oa), flat(ob), flat(oc), p, layer, lp["post"])
    return h, state, feature_major


def _trunk(x, p, q_pos0, caches, lp, depth):
    b, t, d = x.shape
    tables = _rope_tables(q_pos0 + jnp.arange(t, dtype=jnp.int32))
    h = x.reshape(b * t, d)
    p = p.reshape(depth, b * t, -1)
    states, st, stacked_mode = [], None, False
    for i in range(depth):
        h, st, stacked_mode = _layer(h, p, t, q_pos0, caches, i, lp, tables, st if stacked_mode else None)
        states.append(st)
    if stacked_mode:
        heads5 = lambda a, nh: jnp.transpose(a.reshape(depth, b, nh, HEAD_DIM, t), (0, 1, 4, 2, 3))
        swap = lambda a: jnp.transpose(a, (0, 1, 3, 2))
        ka, va, ckv, kr, kc, vc, lf = st
        stacked = [heads5(ka, H_A), heads5(va, H_A), ckv.reshape(depth, b, t, KV_LORA), swap(kr),
                   heads5(kc, H_C), heads5(vc, H_C), swap(lf)]
    else:
        stacked = [jnp.stack([s[j] for s in states]) for j in range(N_STATE)]
    return h.reshape(b, t, d), stacked


def kernel(x_prompt, x_sample, p_prompt, p_sample, cache_a_k, cache_a_v, cache_b_ckv, cache_b_krope, cache_c_k, cache_c_v, cache_c_logf, g_ff1_pre, g_ff1_post, w_ff1_gu, w_ff1_down, g_mix_pre, g_mix_post, w_in, b_f, g_bq, g_bkv, w_uq, w_ukv, g_grp, w_out, g_ff2_pre, g_ff2_post, w_ff2_gu, w_ff2_down, g_ple_pre, w_ple_gate, w_ple_proj, g_ple_post):
    weights = (g_ff1_pre, g_ff1_post, w_ff1_gu, w_ff1_down, g_mix_pre, g_mix_post, w_in, b_f,
               g_bq, g_bkv, w_uq, w_ukv, g_grp, w_out, g_ff2_pre, g_ff2_post, w_ff2_gu, w_ff2_down,
               g_ple_pre, w_ple_gate, w_ple_proj, g_ple_post)
    depth = w_in.shape[0]
    lp = _prep_weights(weights)
    y_prompt, sp = _trunk(x_prompt, p_prompt, 0, None, lp, depth)
    caches = (cache_a_k, cache_a_v, cache_b_ckv, cache_b_krope, cache_c_k, cache_c_v, cache_c_logf)
    y_sample, ss = _trunk(x_sample, p_sample, cache_a_k.shape[2], caches, lp, depth)
    return (y_prompt, y_sample, *sp, *ss)
```
